```python
import jax, jax.numpy as jnp
from jax import lax
import numpy as np

D_MODEL = 2048
BATCH = 4
SEQ = 2048
DEPTH = 2
DEC_BATCH = 128
DEC_SEQ = 1
PAST_LEN = 16384
PAGE_SIZE = 128

N_META = 16
CHUNK = 128
CONV_W = 4
EPS = 1e-6
D_RNN = D_MODEL // 2
RG_BLOCKS = 8
RG_BS = D_RNN // RG_BLOCKS
RG_C = 8.0
RET_HEADS = 4
RET_DK = D_MODEL // 16
RET_DV = D_MODEL // 8
ROPE_BASE = 10000.0
D_M = D_MODEL // 2
M_HEADS = 4
M_DH = D_M // M_HEADS
M_QKV_BS = 4
M_NBLK = D_M // M_QKV_BS
N_BRANCH = 3
D_BRANCH = D_MODEL // 2
D_IN = D_RNN + 2 * RET_HEADS * RET_DK + 2 * RET_HEADS * RET_DV + 2 * D_M + N_BRANCH * D_MODEL
N_GROUPS = 4
EXP_PER_GROUP = 8
N_EXPERTS = N_GROUPS * EXP_PER_GROUP
TOP_K = 2
D_FF = D_MODEL // 2
MOE_BLOCK = 64

kernel_name = 'hybrid_rglru_retention_mlstm_hmoe_step'


def rmsnorm(x, g):
    xf = x.astype(jnp.float32)
    y = xf * lax.rsqrt(jnp.mean(xf * xf, axis=-1, keepdims=True) + EPS)
    return (y * g).astype(x.dtype)


def head_norm(y, g):
    yf = y.astype(jnp.float32)
    mu = jnp.mean(yf, axis=-1, keepdims=True)
    var = jnp.mean(jnp.square(yf - mu), axis=-1, keepdims=True)
    yn = (yf - mu) * lax.rsqrt(var + EPS)
    return yn.reshape(*y.shape[:-2], -1) * g


def causal_conv(x, buf, w, b):
    L = x.shape[1]
    xx = jnp.concatenate([buf.astype(x.dtype), x], axis=1)
    y = b + w[0] * xx[:, 0:L]
    for j in range(1, CONV_W):
        y = y + w[j] * xx[:, j:j + L]
    return y, xx[:, L:]


def blockdiag(x, w):
    nb, bs, _ = w.shape
    xb = x.reshape(*x.shape[:-1], nb, bs)
    return jnp.einsum('...nb,nbc->...nc', xb, w).reshape(x.shape)


def rope(x, pos):
    half = x.shape[-1] // 2
    freq = ROPE_BASE ** (-jnp.arange(half, dtype=jnp.float32) / half)
    ang = pos[:, None] * freq[None, :]
    cos = jnp.cos(ang)[None, :, None, :]
    sin = jnp.sin(ang)[None, :, None, :]
    x1, x2 = x[..., :half], x[..., half:]
    return jnp.concatenate([x1 * cos - x2 * sin, x1 * sin + x2 * cos], axis=-1).astype(x.dtype)


def linear_scan(a, b, h0):
    def comb(l, r):
        return l[0] * r[0], r[0] * l[1] + r[1]
    a_cum, b_cum = lax.associative_scan(comb, (a, b), axis=1)
    h = a_cum * h0[:, None] + b_cum
    return h, h[:, -1]


def rg_lru(xc, h0, wa, ba, wx, bx, lam):
    xf = xc.astype(jnp.float32)
    r = jax.nn.sigmoid(blockdiag(xf, wa) + ba)
    i = jax.nn.sigmoid(blockdiag(xf, wx) + bx)
    log_a = -RG_C * r * jax.nn.softplus(-lam)
    a = jnp.exp(log_a)
    mult = jnp.sqrt(-jnp.expm1(2.0 * log_a))
    h, h_last = linear_scan(a, mult * i * xf, h0.astype(jnp.float32))
    return h, h_last.astype(h0.dtype)


def retention_chunk(S, q, k, v):
    L = q.shape[1]
    log_g = jnp.log1p(-jnp.exp2(-5.0 - jnp.arange(RET_HEADS, dtype=jnp.float32)))
    t = jnp.arange(L, dtype=jnp.float32)
    rel = t[:, None] - t[None, :]
    decay = jnp.where(rel >= 0, jnp.exp(jnp.maximum(rel, 0.0)[None] * log_g[:, None, None]), 0.0)
    qf, kf, vf = q.astype(jnp.float32), k.astype(jnp.float32), v.astype(jnp.float32)
    Sf = S.astype(jnp.float32)
    scores = jnp.einsum('bthd,bshd->bhts', qf, kf) * decay[None]
    intra = jnp.einsum('bhts,bshv->bthv', scores, vf)
    cross = jnp.einsum('bthd,bhdv->bthv', qf, Sf) * jnp.exp((t[:, None] + 1.0) * log_g[None, :])[None, :, :, None]
    k_dec = kf * jnp.exp((L - 1.0 - t)[:, None] * log_g[None, :])[None, :, :, None]
    S_new = jnp.exp(L * log_g)[None, :, None, None] * Sf + jnp.einsum('bshd,bshv->bhdv', k_dec, vf)
    return intra + cross, S_new.astype(S.dtype)


def mlstm_chunk(state, q, k, v, ig, lf):
    C, n, m = state
    L = q.shape[1]
    qf, kf, vf = q.astype(jnp.float32), k.astype(jnp.float32), v.astype(jnp.float32)
    Cf, nf, mf = C.astype(jnp.float32), n.astype(jnp.float32), m.astype(jnp.float32)
    b = jnp.cumsum(lf.astype(jnp.float32), axis=1).transpose(0, 2, 1)
    igt = ig.astype(jnp.float32).transpose(0, 2, 1)
    causal = jnp.tril(jnp.ones((L, L), dtype=bool))
    log_d = jnp.where(causal, b[..., :, None] - b[..., None, :] + igt[..., None, :], -jnp.inf)
    log_inter = b + mf[..., None]
    m_t = jnp.maximum(log_inter, jnp.max(log_d, axis=-1))
    d_m = jnp.exp(log_d - m_t[..., None])
    w_inter = jnp.exp(log_inter - m_t)
    scores = jnp.einsum('bthd,bshd->bhts', qf, kf) * d_m
    num = jnp.einsum('bhts,bshv->bhtv', scores, vf) + w_inter[..., None] * jnp.einsum('bthd,bhdv->bhtv', qf, Cf)
    den = jnp.sum(scores, axis=-1) + w_inter * jnp.einsum('bthd,bhd->bht', qf, nf)
    h = num / jnp.maximum(jnp.abs(den), jnp.exp(-m_t))[..., None]
    m_end = m_t[..., -1]
    w_c = jnp.exp(b[..., -1:] - b + igt - m_end[..., None])
    decay_c = jnp.exp(b[..., -1] + mf - m_end)
    C_new = decay_c[..., None, None] * Cf + jnp.einsum('bhs,bshd,bshv->bhdv', w_c, kf, vf)
    n_new = decay_c[..., None] * nf + jnp.einsum('bhs,bshd->bhd', w_c, kf)
    return h.transpose(0, 2, 1, 3), (C_new.astype(C.dtype), n_new.astype(n.dtype), m_end.astype(m.dtype))


def run_chunked(fn, state0, xs):
    out0, st = fn(state0, *[a[:, :N_META] for a in xs])
    tail = [a[:, N_META:] for a in xs]
    bsz = tail[0].shape[0]
    n_c = tail[0].shape[1] // CHUNK
    tail = [jnp.moveaxis(a.reshape(bsz, n_c, CHUNK, *a.shape[2:]), 1, 0) for a in tail]

    def step(s, inp):
        o, s2 = fn(s, *inp)
        return s2, o

    st, outs = lax.scan(step, st, tail)
    outs = jnp.moveaxis(outs, 0, 1)
    outs = outs.reshape(bsz, n_c * CHUNK, *outs.shape[3:])
    return jnp.concatenate([out0, outs], axis=1), st


def mixer(h, pos, st, p, chunked):
    rg_h, rg_conv, ret_S, mC, mn, mm, m_conv = st
    B, L, _ = h.shape
    z = h @ p['w_in']
    sizes = [D_RNN, RET_HEADS * RET_DK, RET_HEADS * RET_DK, RET_HEADS * RET_DV, RET_HEADS * RET_DV, D_M, D_M]
    splits = [int(s) for s in np.cumsum(sizes)]
    xr, rq, rk, rv, rgate, xm, mo, mgate = jnp.split(z, splits, axis=-1)

    xc, rg_conv_new = causal_conv(xr, rg_conv, p['rg_conv_w'], p['rg_conv_b'])
    y_a, rg_h_new = rg_lru(xc, rg_h, p['rg_wa'], p['rg_ba'], p['rg_wx'], p['rg_bx'], p['rg_lambda'])

    q = rope(rq.reshape(B, L, RET_HEADS, RET_DK), pos)
    k = rope(rk.reshape(B, L, RET_HEADS, RET_DK), pos) * (RET_DK ** -0.5)
    v = rv.reshape(B, L, RET_HEADS, RET_DV)
    if chunked:
        o_b, ret_S_new = run_chunked(retention_chunk, ret_S, (q, k, v))
    else:
        o_b, ret_S_new = retention_chunk(ret_S, q, k, v)
    y_b = head_norm(o_b, p['ret_norm']) * jax.nn.silu(rgate.astype(jnp.float32))

    xmc, m_conv_new = causal_conv(xm, m_conv, p['m_conv_w'], p['m_conv_b'])
    xmc = jax.nn.silu(xmc)
    qm = blockdiag(xmc, p['m_wq'])
    km = blockdiag(xmc, p['m_wk']) * (M_DH ** -0.5)
    vm = blockdiag(xm, p['m_wv'])
    gates = jnp.concatenate([qm, km, vm], axis=-1) @ p['m_w_if'] + p['m_b_if']
    ig = gates[..., :M_HEADS].astype(jnp.float32)
    lf = jax.nn.log_sigmoid(gates[..., M_HEADS:].astype(jnp.float32))
    qh = qm.reshape(B, L, M_HEADS, M_DH)
    kh = km.reshape(B, L, M_HEADS, M_DH)
    vh = vm.reshape(B, L, M_HEADS, M_DH)
    if chunked:
        hm, (mC_new, mn_new, mm_new) = run_chunked(mlstm_chunk, (mC, mn, mm), (qh, kh, vh, ig, lf))
    else:
        hm, (mC_new, mn_new, mm_new) = mlstm_chunk((mC, mn, mm), qh, kh, vh, ig, lf)
    hm = jax.nn.sigmoid(mo.astype(jnp.float32)).reshape(B, L, M_HEADS, M_DH) * hm
    y_c = head_norm(hm, p['m_norm']) + p['m_skip'] * xmc

    branches = jnp.stack([y_a.astype(h.dtype), y_b.astype(h.dtype), y_c.astype(h.dtype)], axis=2)
    proj = jnp.einsum('blnc,ncd->blnd', branches, p['w_branch'])
    g = jax.nn.sigmoid(mgate.reshape(B, L, N_BRANCH, D_MODEL))
    merged = jnp.sum(g * proj, axis=2)
    out = (merged @ p['w_out']).astype(h.dtype)
    return out, (rg_h_new, rg_conv_new, ret_S_new, mC_new, mn_new, mm_new, m_conv_new)


def grouped_experts(xt, expert, wts, w_gate, w_up, w_down):
    T, D = xt.shape
    A = T * TOP_K
    flat_e = expert.reshape(A)
    order = jnp.argsort(flat_e)
    e_sorted = flat_e[order]
    tok = order // TOP_K
    counts = jnp.bincount(flat_e, length=N_EXPERTS)
    padded = (counts + MOE_BLOCK - 1) // MOE_BLOCK * MOE_BLOCK
    pad_end = jnp.cumsum(padded)
    pad_start = pad_end - padded
    start = jnp.cumsum(counts) - counts
    dest = pad_start[e_sorted] + jnp.arange(A) - start[e_sorted]
    n_blocks = -(-A // MOE_BLOCK) + N_EXPERTS
    rows = jnp.full((n_blocks * MOE_BLOCK,), T, dtype=jnp.int32).at[dest].set(tok.astype(jnp.int32))
    block_e = jnp.minimum(jnp.searchsorted(pad_end, jnp.arange(n_blocks) * MOE_BLOCK, side='right'), N_EXPERTS - 1)
    xpad = jnp.concatenate([xt, jnp.zeros((1, D), xt.dtype)], axis=0)
    xb = xpad[rows].reshape(n_blocks, MOE_BLOCK, D)

    def expert_block(args):
        xblk, e = args
        return (jax.nn.silu(xblk @ w_gate[e]) * (xblk @ w_up[e])) @ w_down[e]

    yb = lax.map(expert_block, (xb, block_e)).reshape(n_blocks * MOE_BLOCK, D)
    contrib = yb[dest] * wts.reshape(A)[order][:, None].astype(yb.dtype)
    return jnp.zeros((T, D), yb.dtype).at[tok].add(contrib)


def moe(h, p):
    B, L, D = h.shape
    T = B * L
    xt = h.reshape(T, D)
    glog = (xt @ p['moe_w_group'] + p['moe_b_group']).astype(jnp.float32)
    gprob = jax.nn.softmax(glog, axis=-1)
    _, gidx = lax.top_k(glog, 1)
    elog = (xt @ p['moe_w_expert'] + p['moe_b_expert']).astype(jnp.float32).reshape(T, N_GROUPS, EXP_PER_GROUP)
    elog_g = jnp.take_along_axis(elog, gidx[:, :, None], axis=1)[:, 0]
    ev, eidx = lax.top_k(elog_g, TOP_K)
    wts = jax.nn.softmax(ev, axis=-1) * jnp.take_along_axis(gprob, gidx, axis=1)
    expert = gidx * EXP_PER_GROUP + eidx
    y = grouped_experts(xt, expert, wts, p['moe_w_gate'], p['moe_w_up'], p['moe_w_down'])
    return y.reshape(B, L, D).astype(h.dtype)


def layer(x, pos, st, p, chunked):
    mix, st_new = mixer(rmsnorm(x, p['norm_mix']), pos, st, p, chunked)
    x = x + mix
    x = x + moe(rmsnorm(x, p['norm_ffn']), p)
    return x, st_new


def zero_state(bsz):
    f32 = jnp.float32
    return (jnp.zeros((bsz, D_RNN), f32), jnp.zeros((bsz, CONV_W - 1, D_RNN), f32),
            jnp.zeros((bsz, RET_HEADS, RET_DK, RET_DV), f32), jnp.zeros((bsz, M_HEADS, M_DH, M_DH), f32),
            jnp.zeros((bsz, M_HEADS, M_DH), f32), jnp.zeros((bsz, M_HEADS), f32),
            jnp.zeros((bsz, CONV_W - 1, D_M), f32))


def setup_inputs(seed: int = 0) -> dict:
    key = jax.random.key(seed)
    ks = iter(jax.random.split(key, 48))

    def nrm(shape, scale=1.0):
        return jax.random.normal(next(ks), shape, jnp.float32) * scale

    def gain(shape):
        return 1.0 + nrm(shape, 0.02)

    a0 = jax.random.uniform(next(ks), (DEPTH, D_RNN), jnp.float32, minval=0.9, maxval=0.999)
    m_b_if = jnp.concatenate([nrm((DEPTH, M_HEADS), 0.1),
                              jnp.linspace(3.0, 6.0, M_HEADS, dtype=jnp.float32)[None] + nrm((DEPTH, M_HEADS), 0.01)], axis=-1)
    return {
        'x_prompt': nrm((BATCH, SEQ, D_MODEL)),
        'x_sample': nrm((DEC_BATCH, DEC_SEQ, D_MODEL)),
        'state_rglru_h': nrm((DEPTH, DEC_BATCH, D_RNN), 0.5),
        'state_rglru_conv': nrm((DEPTH, DEC_BATCH, CONV_W - 1, D_RNN)),
        'state_ret': nrm((DEPTH, DEC_BATCH, RET_HEADS, RET_DK, RET_DV), 0.1),
        'state_mlstm_C': nrm((DEPTH, DEC_BATCH, M_HEADS, M_DH, M_DH), 0.1),
        'state_mlstm_n': nrm((DEPTH, DEC_BATCH, M_HEADS, M_DH), 0.1),
        'state_mlstm_m': nrm((DEPTH, DEC_BATCH, M_HEADS)),
        'state_mlstm_conv': nrm((DEPTH, DEC_BATCH, CONV_W - 1, D_M)),
        'meta_tokens': nrm((N_META, D_MODEL)),
        'norm_mix': gain((DEPTH, D_MODEL)),
        'norm_ffn': gain((DEPTH, D_MODEL)),
        'norm_final': gain((D_MODEL,)),
        'w_in': nrm((DEPTH, D_MODEL, D_IN), D_MODEL ** -0.5),
        'rg_conv_w': nrm((DEPTH, CONV_W, D_RNN), CONV_W ** -0.5),
        'rg_conv_b': nrm((DEPTH, D_RNN), 0.02),
        'rg_wa': nrm((DEPTH, RG_BLOCKS, RG_BS, RG_BS), RG_BS ** -0.5),
        'rg_ba': nrm((DEPTH, D_RNN), 0.02),
        'rg_wx': nrm((DEPTH, RG_BLOCKS, RG_BS, RG_BS), RG_BS ** -0.5),
        'rg_bx': nrm((DEPTH, D_RNN), 0.02),
        'rg_lambda': jnp.log(a0) - jnp.log1p(-a0),
        'ret_norm': gain((DEPTH, RET_HEADS * RET_DV)),
        'm_conv_w': nrm((DEPTH, CONV_W, D_M), CONV_W ** -0.5),
        'm_conv_b': nrm((DEPTH, D_M), 0.02),
        'm_wq': nrm((DEPTH, M_NBLK, M_QKV_BS, M_QKV_BS), M_QKV_BS ** -0.5),
        'm_wk': nrm((DEPTH, M_NBLK, M_QKV_BS, M_QKV_BS), M_QKV_BS ** -0.5),
        'm_wv': nrm((DEPTH, M_NBLK, M_QKV_BS, M_QKV_BS), M_QKV_BS ** -0.5),
        'm_w_if': nrm((DEPTH, 3 * D_M, 2 * M_HEADS), (3 * D_M) ** -0.5),
        'm_b_if': m_b_if,
        'm_norm': gain((DEPTH, D_M)),
        'm_skip': gain((DEPTH, D_M)),
        'w_branch': nrm((DEPTH, N_BRANCH, D_BRANCH, D_MODEL), D_BRANCH ** -0.5),
        'w_out': nrm((DEPTH, D_MODEL, D_MODEL), D_MODEL ** -0.5),
        'moe_w_group': nrm((DEPTH, D_MODEL, N_GROUPS), D_MODEL ** -0.5),
        'moe_b_group': nrm((DEPTH, N_GROUPS), 0.01),
        'moe_w_expert': nrm((DEPTH, D_MODEL, N_EXPERTS), D_MODEL ** -0.5),
        'moe_b_expert': nrm((DEPTH, N_EXPERTS), 0.01),
        'moe_w_gate': nrm((DEPTH, N_EXPERTS, D_MODEL, D_FF), D_MODEL ** -0.5),
        'moe_w_up': nrm((DEPTH, N_EXPERTS, D_MODEL, D_FF), D_MODEL ** -0.5),
        'moe_w_down': nrm((DEPTH, N_EXPERTS, D_FF, D_MODEL), D_FF ** -0.5),
    }


def reference(x_prompt, x_sample, state_rglru_h, state_rglru_conv, state_ret, state_mlstm_C, state_mlstm_n,
              state_mlstm_m, state_mlstm_conv, meta_tokens, norm_mix, norm_ffn, norm_final, w_in, rg_conv_w,
              rg_conv_b, rg_wa, rg_ba, rg_wx, rg_bx, rg_lambda, ret_norm, m_conv_w, m_conv_b, m_wq, m_wk, m_wv,
              m_w_if, m_b_if, m_norm, m_skip, w_branch, w_out, moe_w_group, moe_b_group, moe_w_expert,
              moe_b_expert, moe_w_gate, moe_w_up, moe_w_down):
    bp = x_prompt.shape[0]
    meta = jnp.broadcast_to(meta_tokens.astype(x_prompt.dtype)[None], (bp, N_META, D_MODEL))
    xp = jnp.concatenate([meta, x_prompt], axis=1)
    pos_p = jnp.arange(xp.shape[1], dtype=jnp.float32)
    xs = x_sample
    pos_s = PAST_LEN + jnp.arange(x_sample.shape[1], dtype=jnp.float32)
    prompt_states = []
    sample_states = []
    for l in range(DEPTH):
        p = {'norm_mix': norm_mix[l], 'norm_ffn': norm_ffn[l], 'w_in': w_in[l],
             'rg_conv_w': rg_conv_w[l], 'rg_conv_b': rg_conv_b[l], 'rg_wa': rg_wa[l], 'rg_ba': rg_ba[l],
             'rg_wx': rg_wx[l], 'rg_bx': rg_bx[l], 'rg_lambda': rg_lambda[l], 'ret_norm': ret_norm[l],
             'm_conv_w': m_conv_w[l], 'm_conv_b': m_conv_b[l], 'm_wq': m_wq[l], 'm_wk': m_wk[l], 'm_wv': m_wv[l],
             'm_w_if': m_w_if[l], 'm_b_if': m_b_if[l], 'm_norm': m_norm[l], 'm_skip': m_skip[l],
             'w_branch': w_branch[l], 'w_out': w_out[l],
             'moe_w_group': moe_w_group[l], 'moe_b_group': moe_b_group[l],
             'moe_w_expert': moe_w_expert[l], 'moe_b_expert': moe_b_expert[l],
             'moe_w_gate': moe_w_gate[l], 'moe_w_up': moe_w_up[l], 'moe_w_down': moe_w_down[l]}
        st_s0 = (state_rglru_h[l], state_rglru_conv[l], state_ret[l], state_mlstm_C[l], state_mlstm_n[l],
                 state_mlstm_m[l], state_mlstm_conv[l])
        xp, st_p = layer(xp, pos_p, zero_state(bp), p, True)
        xs, st_s = layer(xs, pos_s, st_s0, p, False)
        prompt_states.append(st_p)
        sample_states.append(st_s)
    y_prompt = rmsnorm(xp, norm_final)[:, N_META:]
    y_sample = rmsnorm(xs, norm_final)
    pn = [jnp.stack([s[i] for s in prompt_states], axis=0) for i in range(7)]
    sn = [jnp.stack([s[i] for s in sample_states], axis=0) for i in range(7)]
    rg_h_p, rg_conv_p, ret_p, c_p, n_p, m_p, mconv_p = pn
    rg_h_s, rg_conv_s, ret_s, c_s, n_s, m_s, mconv_s = sn
    return (y_prompt, y_sample, rg_h_p, rg_conv_p, ret_p, c_p, n_p, m_p, mconv_p,
            rg_h_s, rg_conv_s, ret_s, c_s, n_s, m_s, mconv_s)
```

```python
import functools
import math

import jax
import jax.numpy as jnp
import numpy as np
from jax import lax
from jax.experimental import pallas as pl
from jax.experimental.pallas import tpu as pltpu

F32 = jnp.float32
BF16 = jnp.bfloat16

LANES = 128
SUBLANES = 8
BF16_SUBLANES = 16
VMEM_LIMIT_BYTES = 56 * 1024 * 1024

CHUNK = 128
CONV_W = 4
EPS = 1e-6
RG_C = 8.0
ROPE_BASE = 10000.0
PAST_LEN = 16384
N_GROUPS = 4
EXP_PER_GROUP = 8
N_EXPERTS = N_GROUPS * EXP_PER_GROUP
TOP_K = 2
MOE_ROWS = 128
NEG = -1e30


def _cparams(sem, vmem=VMEM_LIMIT_BYTES):
    return pltpu.CompilerParams(dimension_semantics=sem, vmem_limit_bytes=vmem)


def _pick_tile(n, cap, mult):
    best = None
    for t in range(mult, min(n, cap) + 1, mult):
        if n % t == 0:
            best = t
    assert best is not None, (n, cap, mult)
    return best


def _dot(a, b):
    return jnp.dot(a, b, preferred_element_type=F32)


def _dot_nt(a, b):
    return lax.dot_general(a, b, (((1,), (1,)), ((), ())), preferred_element_type=F32)


def _dot_tn(a, b):
    return lax.dot_general(a, b, (((0,), (0,)), ((), ())), preferred_element_type=F32)


def _dot_hi(a, b):
    return jnp.dot(a, b, preferred_element_type=F32, precision=lax.Precision.HIGHEST)


def _sigmoid(x):
    return 1.0 / (1.0 + jnp.exp(-x))


def _silu(x):
    return x * _sigmoid(x)


def _log_sigmoid(x):
    return jnp.minimum(x, 0.0) - jnp.log(1.0 + jnp.exp(-jnp.abs(x)))


def _softplus(x):
    return jnp.maximum(x, 0.0) + jnp.log(1.0 + jnp.exp(-jnp.abs(x)))


def _head_norm(o):
    mu = jnp.mean(o, axis=-1, keepdims=True)
    d = o - mu
    var = jnp.mean(d * d, axis=-1, keepdims=True)
    return d * lax.rsqrt(var + EPS)


def _blockdiag(x, w_ref, scale=None):
    outs = []
    for j in range(w_ref.shape[0]):
        o = _dot(x[:, j * LANES:(j + 1) * LANES].astype(BF16), w_ref[j])
        outs.append(o if scale is None else o * scale)
    return jnp.concatenate(outs, axis=-1)


def _norm_kernel(*refs, n_add, want_x, router):
    x_ref = refs[0]
    add_refs = refs[1:1 + n_add]
    g_ref = refs[1 + n_add]
    pos = 2 + n_add
    if router:
        wr_ref, br_ref = refs[pos], refs[pos + 1]
        pos += 2
    outs = refs[pos:]
    x = x_ref[...]
    for a in add_refs:
        x = x + a[...]
    k = 0
    if want_x:
        outs[k][...] = x
        k += 1
    ms = jnp.mean(x * x, axis=-1, keepdims=True)
    h = x * lax.rsqrt(ms + EPS) * g_ref[...]
    outs[k][...] = h.astype(outs[k].dtype)
    k += 1
    if router:
        ids_ref, wts_ref = outs[k], outs[k + 1]
        logits = _dot_hi(h, wr_ref[...]) + br_ref[...]
        lane = lax.broadcasted_iota(jnp.int32, logits.shape, 1)
        big = jnp.int32(1 << 20)
        is_g = lane < N_GROUPS
        gl = jnp.where(is_g, logits, NEG)
        gmax = jnp.max(gl, axis=-1, keepdims=True)
        gidx = jnp.min(jnp.where(gl == gmax, lane, big), axis=-1, keepdims=True)
        gsum = jnp.sum(jnp.where(is_g, jnp.exp(gl - gmax), 0.0), axis=-1, keepdims=True)
        gprob = 1.0 / gsum
        lo = N_GROUPS + EXP_PER_GROUP * gidx
        em = jnp.where(lane >= lo, jnp.where(lane < lo + EXP_PER_GROUP, logits, NEG), NEG)
        e1v = jnp.max(em, axis=-1, keepdims=True)
        e1i = jnp.min(jnp.where(em == e1v, lane, big), axis=-1, keepdims=True)
        em2 = jnp.where(lane == e1i, NEG, em)
        e2v = jnp.max(em2, axis=-1, keepdims=True)
        e2i = jnp.min(jnp.where(em2 == e2v, lane, big), axis=-1, keepdims=True)
        t = jnp.exp(e2v - e1v)
        w1 = gprob / (1.0 + t)
        w2 = gprob * t / (1.0 + t)
        ids_ref[...] = jnp.where(lane == 0, e1i - N_GROUPS, jnp.where(lane == 1, e2i - N_GROUPS, 0))
        wts_ref[...] = jnp.where(lane == 0, w1, jnp.where(lane == 1, w2, 0.0))


def _norm_call(x, adds, g, *, want_x, h_dtype, router=None, name):
    r, d = x.shape
    tm = _pick_tile(r, 384, BF16_SUBLANES)
    nb = r // tm
    in_specs = [pl.BlockSpec((tm, d), lambda i: (i, 0))]
    args = [x]
    for arr, off in adds:
        assert off % tm == 0
        in_specs.append(pl.BlockSpec((tm, d), lambda i, o=off // tm: (i + o, 0)))
        args.append(arr)
    in_specs.append(pl.BlockSpec((1, d), lambda i: (0, 0)))
    args.append(g.reshape(1, d))
    out_shape, out_specs = [], []
    if want_x:
        out_shape.append(jax.ShapeDtypeStruct((r, d), F32))
        out_specs.append(pl.BlockSpec((tm, d), lambda i: (i, 0)))
    out_shape.append(jax.ShapeDtypeStruct((r, d), h_dtype))
    out_specs.append(pl.BlockSpec((tm, d), lambda i: (i, 0)))
    if router is not None:
        wr, br = router
        in_specs += [pl.BlockSpec((d, LANES), lambda i: (0, 0)), pl.BlockSpec((1, LANES), lambda i: (0, 0))]
        args += [wr, br]
        out_shape += [jax.ShapeDtypeStruct((r, LANES), jnp.int32), jax.ShapeDtypeStruct((r, LANES), F32)]
        out_specs += [pl.BlockSpec((tm, LANES), lambda i: (i, 0))] * 2
    return pl.pallas_call(
        functools.partial(_norm_kernel, n_add=len(adds), want_x=want_x, router=router is not None),
        grid=(nb,), in_specs=in_specs, out_specs=out_specs, out_shape=out_shape,
        compiler_params=_cparams(("arbitrary",)), name=name)(*args)


def _mm_kernel(*refs, has_res):
    if has_res:
        x_ref, w_ref, r_ref, o_ref, wb_ref = refs
    else:
        x_ref, w_ref, o_ref, wb_ref = refs

    @pl.when(pl.program_id(1) == 0)
    def _():
        wb_ref[...] = w_ref[...].astype(BF16)

    acc = _dot(x_ref[...], wb_ref[...])
    if has_res:
        acc = acc + r_ref[...]
    o_ref[...] = acc.astype(o_ref.dtype)


def _mm_call(x, w, res=None, *, out_dtype=F32, name):
    r, k = x.shape
    n = w.shape[1]
    tm = _pick_tile(r, 1152, BF16_SUBLANES)
    tn = _pick_tile(n, 1024, LANES)
    in_specs = [pl.BlockSpec((tm, k), lambda j, i: (i, 0)), pl.BlockSpec((k, tn), lambda j, i: (0, j))]
    args = [x, w]
    if res is not None:
        in_specs.append(pl.BlockSpec((tm, tn), lambda j, i: (i, j)))
        args.append(res)
    return pl.pallas_call(
        functools.partial(_mm_kernel, has_res=res is not None),
        grid=(n // tn, r // tm), in_specs=in_specs,
        out_specs=pl.BlockSpec((tm, tn), lambda j, i: (i, j)),
        out_shape=jax.ShapeDtypeStruct((r, n), out_dtype),
        scratch_shapes=[pltpu.VMEM((k, tn), BF16)],
        compiler_params=_cparams(("arbitrary", "arbitrary")), name=name)(*args)


def _merge_kernel(ya_ref, yb_ref, yc_ref, ga_ref, gb_ref, gc_ref, w_ref, o_ref, wb_ref):
    @pl.when(pl.program_id(1) == 0)
    def _():
        wb_ref[...] = w_ref[...].astype(BF16)

    acc = _sigmoid(ga_ref[...]) * _dot(ya_ref[...], wb_ref[0])
    acc = acc + _sigmoid(gb_ref[...]) * _dot(yb_ref[...], wb_ref[1])
    acc = acc + _sigmoid(gc_ref[...]) * _dot(yc_ref[...], wb_ref[2])
    o_ref[...] = acc.astype(o_ref.dtype)


def _merge_call(ya, yb, yc, z, w_branch, gate_col0, *, name):
    r, db = ya.shape
    nbr, _, d = w_branch.shape
    tn = _pick_tile(d, 1024, LANES)
    tm = _pick_tile(r, 384, BF16_SUBLANES)
    assert gate_col0 % tn == 0 and d % tn == 0
    g0 = gate_col0 // tn
    per = d // tn
    y_spec = pl.BlockSpec((tm, db), lambda j, i: (i, 0))
    g_specs = [pl.BlockSpec((tm, tn), lambda j, i, n=n: (i, g0 + per * n + j)) for n in range(nbr)]
    return pl.pallas_call(
        _merge_kernel, grid=(d // tn, r // tm),
        in_specs=[y_spec, y_spec, y_spec] + g_specs + [pl.BlockSpec((nbr, db, tn), lambda j, i: (0, 0, j))],
        out_specs=pl.BlockSpec((tm, tn), lambda j, i: (i, j)),
        out_shape=jax.ShapeDtypeStruct((r, d), BF16),
        scratch_shapes=[pltpu.VMEM((nbr, db, tn), BF16)],
        compiler_params=_cparams(("arbitrary", "arbitrary")), name=name)(ya, yb, yc, z, z, z, w_branch)


def _shift_rows(x, prev8, k, row):
    rolled = pltpu.roll(x, k, 0)
    head = jnp.where(row[:SUBLANES] < k, pltpu.roll(prev8, k, 0), rolled[:SUBLANES])
    return jnp.concatenate([head, rolled[SUBLANES:]], axis=0)


def _chunk_conv(x, prev8, w_ref, b_ref, row):
    y = b_ref[...] + w_ref[CONV_W - 1:CONV_W, :] * x
    for k in range(1, CONV_W):
        y = y + w_ref[CONV_W - 1 - k:CONV_W - k, :] * _shift_rows(x, prev8, k, row)
    return y


def _rglru_coeffs(xc, wa_ref, ba_ref, wx_ref, bx_ref, lam_ref):
    r = _sigmoid(_blockdiag(xc, wa_ref) + ba_ref[...])
    i = _sigmoid(_blockdiag(xc, wx_ref) + bx_ref[...])
    log_a = -RG_C * r * _softplus(-lam_ref[...])
    a = jnp.exp(log_a)
    mult = jnp.sqrt(1.0 - a * a)
    return a, mult * i * xc


def _pa_kernel(x_ref, cw_ref, cb_ref, wa_ref, ba_ref, wx_ref, bx_ref, lam_ref,
               y_ref, hl_ref, cv_ref, prev_ref, h_ref, *, pad):
    c = pl.program_id(1)
    last = pl.num_programs(1) - 1
    L = x_ref.shape[0]
    row = lax.broadcasted_iota(jnp.int32, (L, 1), 0)
    valid = jnp.logical_or(c > 0, row >= pad)

    @pl.when(c == 0)
    def _():
        prev_ref[...] = jnp.zeros_like(prev_ref)
        h_ref[...] = jnp.zeros_like(h_ref)

    x = jnp.where(valid, x_ref[...], 0.0)
    xc = _chunk_conv(x, prev_ref[...], cw_ref, cb_ref, row)
    prev_ref[...] = x[L - SUBLANES:]
    a, b = _rglru_coeffs(xc, wa_ref, ba_ref, wx_ref, bx_ref, lam_ref)
    a = jnp.where(valid, a, 1.0)
    b = jnp.where(valid, b, 0.0)
    hs = []
    for j in range(x.shape[1] // LANES):
        sl = slice(j * LANES, (j + 1) * LANES)
        aj, bj = a[:, sl], b[:, sl]
        d = 1
        while d < L:
            keep = row >= d
            a_s = jnp.where(keep, pltpu.roll(aj, d, 0), 1.0)
            b_s = jnp.where(keep, pltpu.roll(bj, d, 0), 0.0)
            bj = aj * b_s + bj
            aj = aj * a_s
            d *= 2
        hs.append(aj * h_ref[0:1, sl] + bj)
    h = jnp.concatenate(hs, axis=-1)
    h_ref[0:1, :] = h[L - 1:L]
    y_ref[...] = h.astype(y_ref.dtype)

    @pl.when(c == last)
    def _():
        hl_ref[0] = h[L - 1:L]
        cv_ref[0] = x[L - (CONV_W - 1):]


def _pa_call(z, nseq, nch, pad, p, *, name):
    rows = nseq * nch * CHUNK
    c = p['rg_conv_w'].shape[1]
    full = lambda a: pl.BlockSpec(a.shape, lambda s, k: (0,) * a.ndim)
    params = [p['rg_conv_w'], p['rg_conv_b'], p['rg_wa'], p['rg_ba'], p['rg_wx'], p['rg_bx'], p['rg_lambda']]
    return pl.pallas_call(
        functools.partial(_pa_kernel, pad=pad), grid=(nseq, nch),
        in_specs=[pl.BlockSpec((CHUNK, c), lambda s, k: (s * nch + k, 0))] + [full(a) for a in params],
        out_specs=[pl.BlockSpec((CHUNK, c), lambda s, k: (s * nch + k, 0)),
                   pl.BlockSpec((1, 1, c), lambda s, k: (s, 0, 0)),
                   pl.BlockSpec((1, CONV_W - 1, c), lambda s, k: (s, 0, 0))],
        out_shape=[jax.ShapeDtypeStruct((rows, c), BF16),
                   jax.ShapeDtypeStruct((nseq, 1, c), F32),
                   jax.ShapeDtypeStruct((nseq, CONV_W - 1, c), F32)],
        scratch_shapes=[pltpu.VMEM((SUBLANES, c), F32), pltpu.VMEM((SUBLANES, c), F32)],
        compiler_params=_cparams(("arbitrary", "arbitrary")), name=name)(z, *params)


def _step_conv(x, cs_ref, w_ref, b_ref):
    y = b_ref[...] + w_ref[CONV_W - 1:CONV_W, :] * x
    for k in range(CONV_W - 1):
        y = y + w_ref[k:k + 1, :] * cs_ref[k]
    return y


def _step_conv_state(x, cs_ref, out_ref):
    for k in range(CONV_W - 2):
        out_ref[k] = cs_ref[k + 1]
    out_ref[CONV_W - 2] = x


def _sa_kernel(x_ref, cs_ref, h0_ref, cw_ref, cb_ref, wa_ref, ba_ref, wx_ref, bx_ref, lam_ref,
               y_ref, hn_ref, cv_ref):
    x = x_ref[...]
    xc = _step_conv(x, cs_ref, cw_ref, cb_ref)
    a, b = _rglru_coeffs(xc, wa_ref, ba_ref, wx_ref, bx_ref, lam_ref)
    h = a * h0_ref[...] + b
    hn_ref[...] = h
    y_ref[...] = h.astype(y_ref.dtype)
    _step_conv_state(x, cs_ref, cv_ref)


def _sa_call(z, row0, bs, cs, h0, p, *, name):
    c = h0.shape[1]
    assert row0 % bs == 0
    full = lambda a: pl.BlockSpec(a.shape, lambda i: (0,) * a.ndim)
    params = [p['rg_conv_w'], p['rg_conv_b'], p['rg_wa'], p['rg_ba'], p['rg_wx'], p['rg_bx'], p['rg_lambda']]
    return pl.pallas_call(
        _sa_kernel, grid=(1,),
        in_specs=[pl.BlockSpec((bs, c), lambda i: (row0 // bs, 0)), full(cs), full(h0)] + [full(a) for a in params],
        out_specs=[pl.BlockSpec((bs, c), lambda i: (0, 0)), pl.BlockSpec((bs, c), lambda i: (0, 0)),
                   pl.BlockSpec(cs.shape, lambda i: (0, 0, 0))],
        out_shape=[jax.ShapeDtypeStruct((bs, c), BF16), jax.ShapeDtypeStruct((bs, c), F32),
                   jax.ShapeDtypeStruct(cs.shape, F32)],
        compiler_params=_cparams(("arbitrary",)), name=name)(z, cs, h0, *params)


def _ret_log_g(h):
    return math.log1p(-(2.0 ** (-5.0 - h)))


def _rope(x, cos2, sin2):
    return x * cos2 + pltpu.roll(x, x.shape[1] // 2, 1) * sin2


def _pb_kernel(q_ref, k_ref, v_ref, g_ref, cos_ref, sin_ref, gn_ref, y_ref, so_ref, s_ref, *, pad, heads):
    c = pl.program_id(1)
    last = pl.num_programs(1) - 1
    L = q_ref.shape[0]
    dk = q_ref.shape[1] // heads
    dv = v_ref.shape[1] // heads
    row = lax.broadcasted_iota(jnp.int32, (L, 1), 0)
    valid = jnp.logical_or(c > 0, row >= pad)

    @pl.when(c == 0)
    def _():
        s_ref[...] = jnp.zeros_like(s_ref)

    t = row.astype(F32)
    rel = t - lax.broadcasted_iota(jnp.int32, (1, L), 1).astype(F32)
    cos2, sin2 = cos_ref[...], sin_ref[...]
    for h in range(heads):
        lg = _ret_log_g(h)
        qh = jnp.where(valid, _rope(q_ref[:, h * dk:(h + 1) * dk], cos2, sin2), 0.0)
        kh = jnp.where(valid, _rope(k_ref[:, h * dk:(h + 1) * dk], cos2, sin2), 0.0) * (dk ** -0.5)
        vh = jnp.where(valid, v_ref[:, h * dv:(h + 1) * dv], 0.0)
        decay = jnp.where(rel >= 0, jnp.exp(jnp.maximum(rel, 0.0) * lg), 0.0)
        qb, vb = qh.astype(BF16), vh.astype(BF16)
        scores = _dot_nt(qb, kh.astype(BF16)) * decay
        s_old = s_ref[h]
        o = _dot(scores.astype(BF16), vb) + _dot(qb, s_old.astype(BF16)) * jnp.exp((t + 1.0) * lg)
        k_dec = kh * jnp.exp((L - 1.0 - t) * lg)
        s_ref[h] = math.exp(L * lg) * s_old + _dot_tn(k_dec.astype(BF16), vb)
        sl = slice(h * dv, (h + 1) * dv)
        y_ref[:, sl] = (_head_norm(o) * gn_ref[:, sl] * _silu(g_ref[:, sl])).astype(y_ref.dtype)

    @pl.when(c == last)
    def _():
        so_ref[0] = s_ref[...]


def _pb_call(z, nseq, nch, pad, heads, dk, dv, cos2, sin2, ret_norm, *, name):
    rows = nseq * nch * CHUNK
    hk, hv = heads * dk, heads * dv
    assert hv % hk == 0
    q0 = hv // hk
    return pl.pallas_call(
        functools.partial(_pb_kernel, pad=pad, heads=heads), grid=(nseq, nch),
        in_specs=[pl.BlockSpec((CHUNK, hk), lambda s, k: (s * nch + k, q0)),
                  pl.BlockSpec((CHUNK, hk), lambda s, k: (s * nch + k, q0 + 1)),
                  pl.BlockSpec((CHUNK, hv), lambda s, k: (s * nch + k, 2)),
                  pl.BlockSpec((CHUNK, hv), lambda s, k: (s * nch + k, 3)),
                  pl.BlockSpec((CHUNK, dk), lambda s, k: (k, 0)),
                  pl.BlockSpec((CHUNK, dk), lambda s, k: (k, 0)),
                  pl.BlockSpec((1, hv), lambda s, k: (0, 0))],
        out_specs=[pl.BlockSpec((CHUNK, hv), lambda s, k: (s * nch + k, 0)),
                   pl.BlockSpec((1, heads, dk, dv), lambda s, k: (s, 0, 0, 0))],
        out_shape=[jax.ShapeDtypeStruct((rows, hv), BF16),
                   jax.ShapeDtypeStruct((nseq, heads, dk, dv), F32)],
        scratch_shapes=[pltpu.VMEM((heads, dk, dv), F32)],
        compiler_params=_cparams(("arbitrary", "arbitrary")), name=name)(z, z, z, z, cos2, sin2, ret_norm)


def _sb_kernel(q_ref, k_ref, v_ref, g_ref, s_ref, cos_ref, sin_ref, gn_ref, y_ref, so_ref, *, heads):
    nb = q_ref.shape[0]
    dk = q_ref.shape[1] // heads
    dv = v_ref.shape[1] // heads
    row = lax.broadcasted_iota(jnp.int32, (nb, 1), 0)
    cos2, sin2 = cos_ref[...], sin_ref[...]
    for h in range(heads):
        g = math.exp(_ret_log_g(h))
        qh = _rope(q_ref[:, h * dk:(h + 1) * dk], cos2, sin2)
        kh = _rope(k_ref[:, h * dk:(h + 1) * dk], cos2, sin2) * (dk ** -0.5)
        vh = v_ref[:, h * dv:(h + 1) * dv]
        qb, vb = qh.astype(BF16), vh.astype(BF16)
        cross = jnp.zeros((nb, dv), F32)
        for i in range(nb):
            s_old = s_ref[i, h]
            cross = jnp.where(row == i, _dot(qb, s_old.astype(BF16)), cross)
            so_ref[i, h] = g * s_old + _dot_tn(jnp.where(row == i, kh, 0.0), vh)
        qk = jnp.sum(qb.astype(F32) * kh.astype(BF16).astype(F32), axis=-1, keepdims=True)
        o = qk * vb.astype(F32) + g * cross
        sl = slice(h * dv, (h + 1) * dv)
        y_ref[:, sl] = (_head_norm(o) * gn_ref[:, sl] * _silu(g_ref[:, sl])).astype(y_ref.dtype)


def _sb_call(z, row0, bs, heads, dk, dv, state, cos2, sin2, ret_norm, *, name):
    hk, hv = heads * dk, heads * dv
    nb = SUBLANES
    assert bs % nb == 0 and row0 % nb == 0
    r0 = row0 // nb
    q0 = hv // hk
    return pl.pallas_call(
        functools.partial(_sb_kernel, heads=heads), grid=(bs // nb,),
        in_specs=[pl.BlockSpec((nb, hk), lambda i: (r0 + i, q0)),
                  pl.BlockSpec((nb, hk), lambda i: (r0 + i, q0 + 1)),
                  pl.BlockSpec((nb, hv), lambda i: (r0 + i, 2)),
                  pl.BlockSpec((nb, hv), lambda i: (r0 + i, 3)),
                  pl.BlockSpec((nb, heads, dk, dv), lambda i: (i, 0, 0, 0)),
                  pl.BlockSpec((1, dk), lambda i: (0, 0)),
                  pl.BlockSpec((1, dk), lambda i: (0, 0)),
                  pl.BlockSpec((1, hv), lambda i: (0, 0))],
        out_specs=[pl.BlockSpec((nb, hv), lambda i: (i, 0)),
                   pl.BlockSpec((nb, heads, dk, dv), lambda i: (i, 0, 0, 0))],
        out_shape=[jax.ShapeDtypeStruct((bs, hv), BF16), jax.ShapeDtypeStruct(state.shape, F32)],
        compiler_params=_cparams(("arbitrary",)), name=name)(z, z, z, z, state, cos2, sin2, ret_norm)


def _mlstm_qkv_gates(x, xc, wq_ref, wk_ref, wv_ref, wif_ref, bif_ref, dh):
    q = _blockdiag(xc, wq_ref)
    k = _blockdiag(xc, wk_ref, scale=dh ** -0.5)
    v = _blockdiag(x, wv_ref)
    c = x.shape[1]
    gates = (_dot_hi(q, wif_ref[0:c, :]) + _dot_hi(k, wif_ref[c:2 * c, :])
             + _dot_hi(v, wif_ref[2 * c:3 * c, :]) + bif_ref[...])
    return q, k, v, gates


def _pc_kernel(x_ref, o_ref, cw_ref, cb_ref, wq_ref, wk_ref, wv_ref, wif_ref, bif_ref, gn_ref, sk_ref,
               y_ref, co_ref, no_ref, mo_ref, cv_ref, prev_ref, c_ref, n_ref, m_ref, *, pad, heads):
    c = pl.program_id(1)
    last = pl.num_programs(1) - 1
    L = x_ref.shape[0]
    dh = x_ref.shape[1] // heads
    row = lax.broadcasted_iota(jnp.int32, (L, 1), 0)
    valid = jnp.logical_or(c > 0, row >= pad)

    @pl.when(c == 0)
    def _():
        prev_ref[...] = jnp.zeros_like(prev_ref)
        c_ref[...] = jnp.zeros_like(c_ref)
        n_ref[...] = jnp.zeros_like(n_ref)
        m_ref[...] = jnp.zeros_like(m_ref)

    x = jnp.where(valid, x_ref[...], 0.0)
    xc = _silu(_chunk_conv(x, prev_ref[...], cw_ref, cb_ref, row))
    prev_ref[...] = x[L - SUBLANES:]
    q, k, v, gates = _mlstm_qkv_gates(x, xc, wq_ref, wk_ref, wv_ref, wif_ref, bif_ref, dh)
    lane = lax.broadcasted_iota(jnp.int32, gates.shape, 1)
    is_i = lane < heads
    ig_c = jnp.where(jnp.logical_and(valid, is_i), gates, NEG)
    lf_c = jnp.where(valid, _log_sigmoid(gates), 0.0)
    ig_r = ig_c.T
    lf_r = lf_c.T
    ti = lax.broadcasted_iota(jnp.int32, (L, L), 0)
    si = lax.broadcasted_iota(jnp.int32, (L, L), 1)
    causal = ti >= si
    tril = causal.astype(F32)
    b_c = _dot_hi(tril, lf_c)
    b_r = _dot_hi(lf_r, (si >= ti).astype(F32))
    m_all = m_ref[...]
    m_lane = lax.broadcasted_iota(jnp.int32, m_all.shape, 1)
    m_new = jnp.zeros_like(m_all)
    for h in range(heads):
        sl = slice(h * dh, (h + 1) * dh)
        bc = b_c[:, heads + h:heads + h + 1]
        br = b_r[heads + h:heads + h + 1, :]
        igr = ig_r[h:h + 1, :]
        igc = ig_c[:, h:h + 1]
        m_prev = m_all[0:1, h:h + 1]
        log_d = jnp.where(causal, bc - br + igr, NEG)
        log_inter = bc + m_prev
        m_t = jnp.maximum(log_inter, jnp.max(log_d, axis=-1, keepdims=True))
        d_m = jnp.exp(log_d - m_t)
        w_inter = jnp.exp(log_inter - m_t)
        qh, kh, vh = q[:, sl], k[:, sl], v[:, sl]
        qb, kb, vb = qh.astype(BF16), kh.astype(BF16), vh.astype(BF16)
        scores = _dot_nt(qb, kb) * d_m
        c_old = c_ref[h]
        n_old = n_ref[h:h + 1, :]
        num = _dot(scores.astype(BF16), vb) + w_inter * _dot(qb, c_old.astype(BF16))
        den = jnp.sum(scores, axis=-1, keepdims=True) + w_inter * jnp.sum(qh * n_old, axis=-1, keepdims=True)
        hh = num / jnp.maximum(jnp.abs(den), jnp.exp(-m_t))
        m_end = m_t[L - 1:L]
        b_last = bc[L - 1:L]
        w_c = jnp.exp(b_last - bc + igc - m_end)
        decay_c = jnp.exp(b_last + m_prev - m_end)
        kw = kh * w_c
        c_ref[h] = decay_c * c_old + _dot_tn(kw.astype(BF16), vb)
        n_ref[h:h + 1, :] = decay_c * n_old + jnp.sum(kw, axis=0, keepdims=True)
        m_new = jnp.where(m_lane == h, m_end, m_new)
        hm = _sigmoid(o_ref[:, sl]) * hh
        y_ref[:, sl] = (_head_norm(hm) * gn_ref[:, sl] + sk_ref[:, sl] * xc[:, sl]).astype(y_ref.dtype)
    m_ref[...] = m_new

    @pl.when(c == last)
    def _():
        co_ref[0] = c_ref[...]
        no_ref[0] = n_ref[...]
        mo_ref[0] = m_ref[...]
        cv_ref[0] = x[L - (CONV_W - 1):]


def _pc_call(z, nseq, nch, pad, heads, xcol, p, *, name):
    rows = nseq * nch * CHUNK
    c = p['m_conv_w'].shape[1]
    dh = c // heads
    full = lambda a: pl.BlockSpec(a.shape, lambda s, k: (0,) * a.ndim)
    params = [p['m_conv_w'], p['m_conv_b'], p['m_wq'], p['m_wk'], p['m_wv'], p['m_w_if'], p['m_b_if'],
              p['m_norm'], p['m_skip']]
    return pl.pallas_call(
        functools.partial(_pc_kernel, pad=pad, heads=heads), grid=(nseq, nch),
        in_specs=[pl.BlockSpec((CHUNK, c), lambda s, k: (s * nch + k, xcol)),
                  pl.BlockSpec((CHUNK, c), lambda s, k: (s * nch + k, xcol + 1))] + [full(a) for a in params],
        out_specs=[pl.BlockSpec((CHUNK, c), lambda s, k: (s * nch + k, 0)),
                   pl.BlockSpec((1, heads, dh, dh), lambda s, k: (s, 0, 0, 0)),
                   pl.BlockSpec((1, SUBLANES, dh), lambda s, k: (s, 0, 0)),
                   pl.BlockSpec((1, SUBLANES, LANES), lambda s, k: (s, 0, 0)),
                   pl.BlockSpec((1, CONV_W - 1, c), lambda s, k: (s, 0, 0))],
        out_shape=[jax.ShapeDtypeStruct((rows, c), BF16),
                   jax.ShapeDtypeStruct((nseq, heads, dh, dh), F32),
                   jax.ShapeDtypeStruct((nseq, SUBLANES, dh), F32),
                   jax.ShapeDtypeStruct((nseq, SUBLANES, LANES), F32),
                   jax.ShapeDtypeStruct((nseq, CONV_W - 1, c), F32)],
        scratch_shapes=[pltpu.VMEM((SUBLANES, c), F32), pltpu.VMEM((heads, dh, dh), F32),
                        pltpu.VMEM((SUBLANES, dh), F32), pltpu.VMEM((SUBLANES, LANES), F32)],
        compiler_params=_cparams(("arbitrary", "arbitrary")), name=name)(z, z, *params)


def _sc_kernel(x_ref, o_ref, cs_ref, c_ref, n_ref, m_ref, cw_ref, cb_ref, wq_ref, wk_ref, wv_ref, wif_ref, bif_ref,
               gn_ref, sk_ref, y_ref, co_ref, no_ref, mo_ref, cv_ref, *, heads):
    nb = x_ref.shape[0]
    dh = x_ref.shape[1] // heads
    row = lax.broadcasted_iota(jnp.int32, (nb, 1), 0)
    x = x_ref[...]
    xc = _silu(_step_conv(x, cs_ref, cw_ref, cb_ref))
    q, k, v, gates = _mlstm_qkv_gates(x, xc, wq_ref, wk_ref, wv_ref, wif_ref, bif_ref, dh)
    lf = _log_sigmoid(gates)
    m_old = m_ref[...]
    lane = lax.broadcasted_iota(jnp.int32, m_old.shape, 1)
    m_new = jnp.zeros_like(m_old)
    for h in range(heads):
        sl = slice(h * dh, (h + 1) * dh)
        ig = gates[:, h:h + 1]
        log_inter = lf[:, heads + h:heads + h + 1] + m_old[:, h:h + 1]
        m_t = jnp.maximum(log_inter, ig)
        d_m = jnp.exp(ig - m_t)
        w_inter = jnp.exp(log_inter - m_t)
        qh, kh, vh = q[:, sl], k[:, sl], v[:, sl]
        qb, kb, vb = qh.astype(BF16), kh.astype(BF16), vh.astype(BF16)
        sc = jnp.sum(qb.astype(F32) * kb.astype(F32), axis=-1, keepdims=True) * d_m
        n_old = n_ref[:, sl]
        kw = kh * d_m
        qc = jnp.zeros((nb, dh), F32)
        for i in range(nb):
            c_old = c_ref[i, h]
            qc = jnp.where(row == i, _dot(qb, c_old.astype(BF16)), qc)
            co_ref[i, h] = (w_inter[i:i + 1] * c_old
                            + _dot_tn(jnp.where(row == i, kw, 0.0), vh))
        num = sc * vb.astype(F32) + w_inter * qc
        den = sc + w_inter * jnp.sum(qh * n_old, axis=-1, keepdims=True)
        hh = num / jnp.maximum(jnp.abs(den), jnp.exp(-m_t))
        no_ref[:, sl] = w_inter * n_old + kw
        m_new = jnp.where(lane == h, m_t, m_new)
        hm = _sigmoid(o_ref[:, sl]) * hh
        y_ref[:, sl] = (_head_norm(hm) * gn_ref[:, sl] + sk_ref[:, sl] * xc[:, sl]).astype(y_ref.dtype)
    mo_ref[...] = m_new
    _step_conv_state(x, cs_ref, cv_ref)


def _sc_call(z, row0, bs, heads, xcol, cs, c_state, n_state, m_state, p, *, name):
    c = p['m_conv_w'].shape[1]
    dh = c // heads
    nb = SUBLANES
    assert bs % nb == 0 and row0 % nb == 0
    r0 = row0 // nb
    full = lambda a: pl.BlockSpec(a.shape, lambda i: (0,) * a.ndim)
    params = [p['m_conv_w'], p['m_conv_b'], p['m_wq'], p['m_wk'], p['m_wv'], p['m_w_if'], p['m_b_if'],
              p['m_norm'], p['m_skip']]
    cs_spec = pl.BlockSpec((CONV_W - 1, nb, c), lambda i: (0, i, 0))
    return pl.pallas_call(
        functools.partial(_sc_kernel, heads=heads), grid=(bs // nb,),
        in_specs=[pl.BlockSpec((nb, c), lambda i: (r0 + i, xcol)),
                  pl.BlockSpec((nb, c), lambda i: (r0 + i, xcol + 1)),
                  cs_spec,
                  pl.BlockSpec((nb, heads, dh, dh), lambda i: (i, 0, 0, 0)),
                  pl.BlockSpec((nb, c), lambda i: (i, 0)),
                  pl.BlockSpec((nb, LANES), lambda i: (i, 0))] + [full(a) for a in params],
        out_specs=[pl.BlockSpec((nb, c), lambda i: (i, 0)),
                   pl.BlockSpec((nb, heads, dh, dh), lambda i: (i, 0, 0, 0)),
                   pl.BlockSpec((nb, c), lambda i: (i, 0)),
                   pl.BlockSpec((nb, LANES), lambda i: (i, 0)),
                   cs_spec],
        out_shape=[jax.ShapeDtypeStruct((bs, c), BF16), jax.ShapeDtypeStruct(c_state.shape, F32),
                   jax.ShapeDtypeStruct((bs, c), F32), jax.ShapeDtypeStruct((bs, LANES), F32),
                   jax.ShapeDtypeStruct(cs.shape, F32)],
        compiler_params=_cparams(("arbitrary",)), name=name)(z, z, cs, c_state, n_state, m_state, *params)


def _ffn_kernel(be_ref, cnt_ref, src_ref, dst_ref, h_hbm, w_ref, wg_ref, wu_ref, wd_ref, o_hbm,
                xbuf, ybuf, gsem, ssem):
    i = pl.program_id(0)
    cnt = cnt_ref[i]
    base = i * MOE_ROWS

    @pl.when(i == 0)
    def _():
        xbuf[...] = jnp.zeros_like(xbuf)

    def gather(r):
        return pltpu.make_async_copy(h_hbm.at[pl.ds(src_ref[base + r], 1), :], xbuf.at[pl.ds(r, 1), :], gsem)

    def scatter(r):
        return pltpu.make_async_copy(ybuf.at[pl.ds(r, 1), :], o_hbm.at[pl.ds(dst_ref[base + r], 1), :], ssem)

    def start(mk):
        def body(r, carry):
            mk(r).start()
            return carry
        return body

    def wait(mk):
        def body(r, carry):
            mk(r).wait()
            return carry
        return body

    @pl.when(cnt > 0)
    def _():
        lax.fori_loop(0, cnt, start(gather), 0)
        lax.fori_loop(0, cnt, wait(gather), 0)
        x = xbuf[...].astype(BF16)
        hmid = (_silu(_dot(x, wg_ref[...])) * _dot(x, wu_ref[...])).astype(BF16)
        ybuf[...] = _dot(hmid, wd_ref[...]) * w_ref[...]
        lax.fori_loop(0, cnt, start(scatter), 0)
        lax.fori_loop(0, cnt, wait(scatter), 0)


def _ffn_call(h2, block_e, block_cnt, slot_src, slot_dst, slot_w, wg, wu, wd, out_rows, *, name):
    d = h2.shape[1]
    f = wg.shape[2]
    nb = block_e.shape[0]
    grid_spec = pltpu.PrefetchScalarGridSpec(
        num_scalar_prefetch=4, grid=(nb,),
        in_specs=[pl.BlockSpec(memory_space=pl.ANY),
                  pl.BlockSpec((MOE_ROWS, 1), lambda i, be, cn, sr, ds: (i, 0)),
                  pl.BlockSpec((None, d, f), lambda i, be, cn, sr, ds: (be[i], 0, 0)),
                  pl.BlockSpec((None, d, f), lambda i, be, cn, sr, ds: (be[i], 0, 0)),
                  pl.BlockSpec((None, f, d), lambda i, be, cn, sr, ds: (be[i], 0, 0))],
        out_specs=pl.BlockSpec(memory_space=pl.ANY),
        scratch_shapes=[pltpu.VMEM((MOE_ROWS, d), F32), pltpu.VMEM((MOE_ROWS, d), F32),
                        pltpu.SemaphoreType.DMA(()), pltpu.SemaphoreType.DMA(())])
    return pl.pallas_call(
        _ffn_kernel, grid_spec=grid_spec,
        out_shape=jax.ShapeDtypeStruct((out_rows, d), F32),
        compiler_params=_cparams(("arbitrary",)), name=name)(
            block_e, block_cnt, slot_src, slot_dst, h2, slot_w, wg, wu, wd)


def _dispatch(ids, wts, r):
    a = r * TOP_K
    flat_e = ids[:, :TOP_K].reshape(a)
    flat_w = wts[:, :TOP_K].reshape(a)
    order = jnp.argsort(flat_e)
    e_sorted = flat_e[order]
    counts = jnp.bincount(flat_e, length=N_EXPERTS)
    padded = (counts + MOE_ROWS - 1) // MOE_ROWS * MOE_ROWS
    pad_end = jnp.cumsum(padded)
    pad_start = pad_end - padded
    start = jnp.cumsum(counts) - counts
    dest = (pad_start[e_sorted] + jnp.arange(a) - start[e_sorted]).astype(jnp.int32)
    nb = a // MOE_ROWS + N_EXPERTS
    ns = nb * MOE_ROWS
    tok = (order // TOP_K).astype(jnp.int32)
    kk = (order % TOP_K).astype(jnp.int32)
    slot_src = jnp.zeros((ns,), jnp.int32).at[dest].set(tok)
    slot_dst = jnp.zeros((ns,), jnp.int32).at[dest].set(kk * r + tok)
    slot_w = jnp.zeros((ns,), F32).at[dest].set(flat_w[order]).reshape(ns, 1)
    blk0 = jnp.arange(nb, dtype=jnp.int32) * MOE_ROWS
    block_e = jnp.minimum(jnp.searchsorted(pad_end, blk0, side='right'), N_EXPERTS - 1).astype(jnp.int32)
    seg_end = (pad_start + counts)[block_e]
    block_cnt = jnp.clip(seg_end - blk0, 0, MOE_ROWS).astype(jnp.int32)
    return block_e, block_cnt, slot_src, slot_dst, slot_w


def _expand_blockdiag(w):
    nb, bs, _ = w.shape
    if bs == LANES:
        return w.astype(BF16)
    per = LANES // bs
    eye = jnp.eye(per, dtype=w.dtype)
    wt = w.reshape(nb // per, per, bs, bs)
    out = jnp.einsum('tpbc,pq->tpbqc', wt, eye)
    return out.reshape(nb // per, LANES, LANES).astype(BF16)


def _rope_tables(pos, half):
    freq = ROPE_BASE ** (-jnp.arange(half, dtype=F32) / half)
    ang = pos[:, None] * freq[None, :]
    cos, sin = jnp.cos(ang), jnp.sin(ang)
    return jnp.concatenate([cos, cos], axis=-1), jnp.concatenate([-sin, sin], axis=-1)


def kernel(x_prompt, x_sample, state_rglru_h, state_rglru_conv, state_ret, state_mlstm_C, state_mlstm_n,
           state_mlstm_m, state_mlstm_conv, meta_tokens, norm_mix, norm_ffn, norm_final, w_in, rg_conv_w,
           rg_conv_b, rg_wa, rg_ba, rg_wx, rg_bx, rg_lambda, ret_norm, m_conv_w, m_conv_b, m_wq, m_wk, m_wv,
           m_w_if, m_b_if, m_norm, m_skip, w_branch, w_out, moe_w_group, moe_b_group, moe_w_expert,
           moe_b_expert, moe_w_gate, moe_w_up, moe_w_down):
    bp, seq, d = x_prompt.shape
    bs = x_sample.shape[0]
    n_meta = meta_tokens.shape[0]
    depth = w_in.shape[0]
    d_rnn = state_rglru_h.shape[2]
    _, _, r_heads, r_dk, r_dv = state_ret.shape
    m_heads, m_dh = state_mlstm_C.shape[2], state_mlstm_C.shape[3]
    d_m = m_heads * m_dh
    assert seq % CHUNK == 0 and n_meta <= CHUNK and x_sample.shape[1] == 1
    pad = CHUNK - n_meta
    nch = 1 + seq // CHUNK
    rp = bp * nch * CHUNK
    r = rp + bs
    assert d_rnn == r_heads * r_dv == d_m
    xm_col = (d_rnn + 2 * r_heads * r_dk + 2 * r_heads * r_dv) // d_m
    gate_col0 = d_rnn + 2 * r_heads * r_dk + 2 * r_heads * r_dv + 2 * d_m

    meta = jnp.broadcast_to(meta_tokens[None], (bp, n_meta, d))
    xp = jnp.concatenate([jnp.zeros((bp, pad, d), F32), meta, x_prompt], axis=1).reshape(rp, d)
    x = jnp.concatenate([xp, x_sample.reshape(bs, d)], axis=0)

    pos_p = jnp.arange(nch * CHUNK, dtype=F32) - pad
    cos_p, sin_p = _rope_tables(pos_p, r_dk // 2)
    pos_s = jnp.full((1,), float(PAST_LEN), F32)
    cos_s, sin_s = _rope_tables(pos_s, r_dk // 2)

    prompt_states, sample_states = [], []
    adds = []
    for l in range(depth):
        p = {'rg_conv_w': rg_conv_w[l], 'rg_conv_b': rg_conv_b[l].reshape(1, -1),
             'rg_wa': _expand_blockdiag(rg_wa[l]), 'rg_ba': rg_ba[l].reshape(1, -1),
             'rg_wx': _expand_blockdiag(rg_wx[l]), 'rg_bx': rg_bx[l].reshape(1, -1),
             'rg_lambda': rg_lambda[l].reshape(1, -1),
             'm_conv_w': m_conv_w[l], 'm_conv_b': m_conv_b[l].reshape(1, -1),
             'm_wq': _expand_blockdiag(m_wq[l]), 'm_wk': _expand_blockdiag(m_wk[l]),
             'm_wv': _expand_blockdiag(m_wv[l]),
             'm_w_if': jnp.pad(m_w_if[l], ((0, 0), (0, LANES - 2 * m_heads))),
             'm_b_if': jnp.pad(m_b_if[l], (0, LANES - 2 * m_heads)).reshape(1, LANES),
             'm_norm': m_norm[l].reshape(1, -1), 'm_skip': m_skip[l].reshape(1, -1)}
        gn_ret = ret_norm[l].reshape(1, -1)

        x, h = _norm_call(x, adds, norm_mix[l], want_x=True, h_dtype=BF16, name=f'norm_mix{l}')
        z = _mm_call(h, w_in[l], name=f'in_proj{l}')

        ya_p, rgh_p, rgc_p = _pa_call(z, bp, nch, pad, p, name=f'rglru_p{l}')
        yb_p, ret_p = _pb_call(z, bp, nch, pad, r_heads, r_dk, r_dv, cos_p, sin_p, gn_ret, name=f'ret_p{l}')
        yc_p, c_p, n_p, m_p, mc_p = _pc_call(z, bp, nch, pad, m_heads, xm_col, p, name=f'mlstm_p{l}')

        rg_cs = jnp.transpose(state_rglru_conv[l], (1, 0, 2))
        ya_s, rgh_s, rgc_s = _sa_call(z, rp, bs, rg_cs, state_rglru_h[l], p, name=f'rglru_s{l}')
        yb_s, ret_s = _sb_call(z, rp, bs, r_heads, r_dk, r_dv, state_ret[l], cos_s, sin_s, gn_ret, name=f'ret_s{l}')
        m_cs = jnp.transpose(state_mlstm_conv[l], (1, 0, 2))
        m_in = jnp.pad(state_mlstm_m[l], ((0, 0), (0, LANES - m_heads)))
        yc_s, c_s, n_s, m_s, mc_s = _sc_call(z, rp, bs, m_heads, xm_col, m_cs, state_mlstm_C[l],
                                             state_mlstm_n[l].reshape(bs, d_m), m_in, p, name=f'mlstm_s{l}')

        prompt_states.append((rgh_p[:, 0], rgc_p, ret_p, c_p, n_p[:, :m_heads], m_p[:, 0, :m_heads], mc_p))
        sample_states.append((rgh_s, jnp.transpose(rgc_s, (1, 0, 2)), ret_s, c_s,
                              n_s.reshape(bs, m_heads, m_dh), m_s[:, :m_heads], jnp.transpose(mc_s, (1, 0, 2))))

        ya = jnp.concatenate([ya_p, ya_s], axis=0)
        yb = jnp.concatenate([yb_p, yb_s], axis=0)
        yc = jnp.concatenate([yc_p, yc_s], axis=0)
        merged = _merge_call(ya, yb, yc, z, w_branch[l], gate_col0, name=f'merge{l}')
        x = _mm_call(merged, w_out[l], res=x, name=f'out_proj{l}')

        wr = jnp.pad(jnp.concatenate([moe_w_group[l], moe_w_expert[l]], axis=1),
                     ((0, 0), (0, LANES - N_GROUPS - N_EXPERTS)))
        br = jnp.pad(jnp.concatenate([moe_b_group[l], moe_b_expert[l]]), (0, LANES - N_GROUPS - N_EXPERTS))
        h2, ids, wts = _norm_call(x, [], norm_ffn[l], want_x=False, h_dtype=F32,
                                  router=(wr, br.reshape(1, LANES)), name=f'norm_router{l}')
        block_e, block_cnt, slot_src, slot_dst, slot_w = _dispatch(ids, wts, r)
        moe = _ffn_call(h2, block_e, block_cnt, slot_src, slot_dst, slot_w, moe_w_gate[l].astype(BF16),
                        moe_w_up[l].astype(BF16), moe_w_down[l].astype(BF16), TOP_K * r, name=f'moe_ffn{l}')
        adds = [(moe, 0), (moe, r)]

    (y,) = _norm_call(x, adds, norm_final, want_x=False, h_dtype=F32, name='norm_final')
    y_prompt = y[:rp].reshape(bp, nch * CHUNK, d)[:, CHUNK:]
    y_sample = y[rp:].reshape(bs, 1, d)
    pn = [jnp.stack([s[i] for s in prompt_states], axis=0) for i in range(7)]
    sn = [jnp.stack([s[i] for s in sample_states], axis=0) for i in range(7)]
    return (y_prompt, y_sample, *pn, *sn)
```

```python
import functools
import math

import jax
import jax.numpy as jnp
import numpy as np
from jax import lax
from jax.experimental import pallas as pl
from jax.experimental.pallas import tpu as pltpu

F32 = jnp.float32
BF16 = jnp.bfloat16

LANES = 128
SUBLANES = 8
BF16_SUBLANES = 16
VMEM_LIMIT_BYTES = 56 * 1024 * 1024

CHUNK = 128
CONV_W = 4
EPS = 1e-6
RG_C = 8.0
ROPE_BASE = 10000.0
PAST_LEN = 16384
N_GROUPS = 4
EXP_PER_GROUP = 8
N_EXPERTS = N_GROUPS * EXP_PER_GROUP
TOP_K = 2
MOE_ROWS = 128
NEG = -1e30


def _cparams(sem, vmem=VMEM_LIMIT_BYTES):
    return pltpu.CompilerParams(dimension_semantics=sem, vmem_limit_bytes=vmem)


def _pick_tile(n, cap, mult):
    best = None
    for t in range(mult, min(n, cap) + 1, mult):
        if n % t == 0:
            best = t
    assert best is not None, (n, cap, mult)
    return best


def _skip_ref(fn, idx):
    if idx is None:
        return fn

    def wrapped(*refs):
        return fn(*refs[:idx], *refs[idx + 1:])
    return wrapped


def _prev_specs(prev):
    return [] if prev is None else [pl.BlockSpec(memory_space=pl.ANY)]


def _dot(a, b):
    return jnp.dot(a, b, preferred_element_type=F32)


def _dot_nt(a, b):
    return lax.dot_general(a, b, (((1,), (1,)), ((), ())), preferred_element_type=F32)


def _dot_tn(a, b):
    return lax.dot_general(a, b, (((0,), (0,)), ((), ())), preferred_element_type=F32)


def _dot_hi(a, b):
    return jnp.dot(a, b, preferred_element_type=F32, precision=lax.Precision.HIGHEST)


def _sigmoid(x):
    return 1.0 / (1.0 + jnp.exp(-x))


def _silu(x):
    return x * _sigmoid(x)


def _log_sigmoid(x):
    return jnp.minimum(x, 0.0) - jnp.log(1.0 + jnp.exp(-jnp.abs(x)))


def _softplus(x):
    return jnp.maximum(x, 0.0) + jnp.log(1.0 + jnp.exp(-jnp.abs(x)))


def _head_norm(o):
    mu = jnp.mean(o, axis=-1, keepdims=True)
    d = o - mu
    var = jnp.mean(d * d, axis=-1, keepdims=True)
    return d * lax.rsqrt(var + EPS)


def _blockdiag(x, w_ref, scale=None):
    outs = []
    for j in range(w_ref.shape[0]):
        o = _dot(x[:, j * LANES:(j + 1) * LANES].astype(BF16), w_ref[j])
        outs.append(o if scale is None else o * scale)
    return jnp.concatenate(outs, axis=-1)


def _norm_kernel(*refs, combine, want_x, router):
    refs = list(refs)
    if combine:
        dest_ref, wts_in_ref, yb_hbm = refs[:3]
        refs = refs[3:]
    x_ref, g_ref = refs[:2]
    refs = refs[2:]
    if router:
        wr_ref, br_ref = refs[:2]
        refs = refs[2:]
    n_out = int(want_x) + 1 + (3 if router else 0)
    outs, scratch = refs[:n_out], refs[n_out:]
    x = x_ref[...]
    tm = x.shape[0]
    if combine:
        gbuf, gsem = scratch

        def row_copy(r, k):
            return pltpu.make_async_copy(yb_hbm.at[pl.ds(dest_ref[0, TOP_K * r + k], 1), :],
                                         gbuf.at[k, pl.ds(r, 1), :], gsem)

        def start(r, carry):
            for k in range(TOP_K):
                row_copy(r, k).start()
            return carry

        def wait(r, carry):
            for k in range(TOP_K):
                row_copy(r, k).wait()
            return carry

        lax.fori_loop(0, tm, start, 0)
        lax.fori_loop(0, tm, wait, 0)
        for k in range(TOP_K):
            x = x + wts_in_ref[:, k:k + 1] * gbuf[k]
    k = 0
    if want_x:
        outs[k][...] = x
        k += 1
    ms = jnp.mean(x * x, axis=-1, keepdims=True)
    h = x * lax.rsqrt(ms + EPS) * g_ref[...]
    outs[k][...] = h.astype(outs[k].dtype)
    k += 1
    if router:
        ids_ref, wts_ref, cnt_ref = outs[k], outs[k + 1], outs[k + 2]
        logits = _dot_hi(h, wr_ref[...]) + br_ref[...]
        lane = lax.broadcasted_iota(jnp.int32, logits.shape, 1)
        big = jnp.int32(1 << 20)
        is_g = lane < N_GROUPS
        gl = jnp.where(is_g, logits, NEG)
        gmax = jnp.max(gl, axis=-1, keepdims=True)
        gidx = jnp.min(jnp.where(gl == gmax, lane, big), axis=-1, keepdims=True)
        gsum = jnp.sum(jnp.where(is_g, jnp.exp(gl - gmax), 0.0), axis=-1, keepdims=True)
        gprob = 1.0 / gsum
        lo = N_GROUPS + EXP_PER_GROUP * gidx
        em = jnp.where(lane >= lo, jnp.where(lane < lo + EXP_PER_GROUP, logits, NEG), NEG)
        e1v = jnp.max(em, axis=-1, keepdims=True)
        e1i = jnp.min(jnp.where(em == e1v, lane, big), axis=-1, keepdims=True)
        em2 = jnp.where(lane == e1i, NEG, em)
        e2v = jnp.max(em2, axis=-1, keepdims=True)
        e2i = jnp.min(jnp.where(em2 == e2v, lane, big), axis=-1, keepdims=True)
        t = jnp.exp(e2v - e1v)
        w1 = gprob / (1.0 + t)
        w2 = gprob * t / (1.0 + t)
        e1, e2 = e1i - N_GROUPS, e2i - N_GROUPS
        chosen = jnp.where(lane == e1, 1.0, jnp.where(lane == e2, 1.0, 0.0))
        ti = lax.broadcasted_iota(jnp.int32, (tm, tm), 0)
        si = lax.broadcasted_iota(jnp.int32, (tm, tm), 1)
        before = _dot(jnp.where(ti > si, 1.0, 0.0).astype(BF16), chosen.astype(BF16))
        rank1 = jnp.sum(jnp.where(lane == e1, before, 0.0), axis=-1, keepdims=True).astype(jnp.int32)
        rank2 = jnp.sum(jnp.where(lane == e2, before, 0.0), axis=-1, keepdims=True).astype(jnp.int32)
        ids_ref[...] = jnp.where(lane == 0, e1, jnp.where(lane == 1, e2,
                                 jnp.where(lane == 2, rank1, jnp.where(lane == 3, rank2, 0))))
        wts_ref[...] = jnp.where(lane == 0, w1, jnp.where(lane == 1, w2, 0.0))
        cnt_ref[...] = jnp.sum(chosen, axis=0, keepdims=True)


def _norm_call(x, g, *, tm, grid, row_tile, want_x, h_shape, h_dtype, h_spec, combine=None, router=None, name):
    r, d = x.shape
    x_spec = pl.BlockSpec((tm, d), lambda *gi: (row_tile(*gi), 0))
    in_specs, args, scratch = [], [], []
    if combine is not None:
        dest, wts, yb = combine
        in_specs += [pl.BlockSpec((None, 1, TOP_K * tm), lambda *gi: (row_tile(*gi), 0, 0), memory_space=pltpu.SMEM),
                     pl.BlockSpec((tm, LANES), lambda *gi: (row_tile(*gi), 0)),
                     pl.BlockSpec(memory_space=pl.ANY)]
        args += [dest.reshape(r // tm, 1, TOP_K * tm), wts, yb]
        scratch = [pltpu.VMEM((TOP_K, tm, d), F32), pltpu.SemaphoreType.DMA(())]
    in_specs += [x_spec, pl.BlockSpec((1, d), lambda *gi: (0, 0))]
    args += [x, g.reshape(1, d)]
    if router is not None:
        in_specs += [pl.BlockSpec((d, LANES), lambda *gi: (0, 0)), pl.BlockSpec((1, LANES), lambda *gi: (0, 0))]
        args += list(router)
    out_shape, out_specs = [], []
    if want_x:
        out_shape.append(jax.ShapeDtypeStruct((r, d), F32))
        out_specs.append(x_spec)
    out_shape.append(jax.ShapeDtypeStruct(h_shape, h_dtype))
    out_specs.append(h_spec)
    if router is not None:
        out_shape += [jax.ShapeDtypeStruct((r, LANES), jnp.int32), jax.ShapeDtypeStruct((r, LANES), F32),
                      jax.ShapeDtypeStruct((r // tm, 1, LANES), F32)]
        out_specs += [pl.BlockSpec((tm, LANES), lambda *gi: (row_tile(*gi), 0))] * 2
        out_specs += [pl.BlockSpec((None, 1, LANES), lambda *gi: (row_tile(*gi), 0, 0))]
    return pl.pallas_call(
        functools.partial(_norm_kernel, combine=combine is not None, want_x=want_x, router=router is not None),
        grid=grid, in_specs=in_specs, out_specs=out_specs, out_shape=out_shape, scratch_shapes=scratch,
        compiler_params=_cparams(("arbitrary",) * len(grid)), name=name)(*args)


def _dispatch_kernel(dest_ref, h_ref, xs_in, xs_out, sem):
    del xs_in
    tm = h_ref.shape[0]

    def row_copy(r, k):
        return pltpu.make_async_copy(h_ref.at[pl.ds(r, 1), :],
                                     xs_out.at[pl.ds(dest_ref[0, TOP_K * r + k], 1), :], sem)

    def start(r, carry):
        for k in range(TOP_K):
            row_copy(r, k).start()
        return carry

    def wait(r, carry):
        for k in range(TOP_K):
            row_copy(r, k).wait()
        return carry

    lax.fori_loop(0, tm, start, 0)
    lax.fori_loop(0, tm, wait, 0)


def _dispatch_call(h2, dest, n_slots, *, tm, name):
    r, d = h2.shape
    xs0 = jnp.zeros((n_slots, d), h2.dtype)
    return pl.pallas_call(
        _dispatch_kernel, grid=(r // tm,),
        in_specs=[pl.BlockSpec((None, 1, TOP_K * tm), lambda i: (i, 0, 0), memory_space=pltpu.SMEM),
                  pl.BlockSpec((tm, d), lambda i: (i, 0)),
                  pl.BlockSpec(memory_space=pl.ANY)],
        out_specs=pl.BlockSpec(memory_space=pl.ANY),
        out_shape=jax.ShapeDtypeStruct((n_slots, d), h2.dtype),
        scratch_shapes=[pltpu.SemaphoreType.DMA(())],
        input_output_aliases={2: 0},
        compiler_params=_cparams(("arbitrary",)), name=name)(dest.reshape(r // tm, 1, TOP_K * tm), h2, xs0)


def _mm_kernel(*refs, has_res):
    if has_res:
        x_ref, w_ref, r_ref, o_ref, wb_ref = refs
    else:
        x_ref, w_ref, o_ref, wb_ref = refs

    @pl.when(pl.program_id(1) == 0)
    def _():
        wb_ref[...] = w_ref[...].astype(BF16)

    acc = _dot(x_ref[...], wb_ref[...])
    if has_res:
        acc = acc + r_ref[...]
    o_ref[...] = acc.astype(o_ref.dtype)


def _mm_call(x, w, l, res=None, *, out_dtype=F32, name):
    r, k = x.shape
    n = w.shape[2]
    tm = _pick_tile(r, 1152, BF16_SUBLANES)
    tn = _pick_tile(n, 1024, LANES)
    in_specs = [pl.BlockSpec((tm, k), lambda j, i: (i, 0)), pl.BlockSpec((None, k, tn), lambda j, i: (l, 0, j))]
    args = [x, w]
    if res is not None:
        in_specs.append(pl.BlockSpec((tm, tn), lambda j, i: (i, j)))
        args.append(res)
    return pl.pallas_call(
        functools.partial(_mm_kernel, has_res=res is not None),
        grid=(n // tn, r // tm), in_specs=in_specs,
        out_specs=pl.BlockSpec((tm, tn), lambda j, i: (i, j)),
        out_shape=jax.ShapeDtypeStruct((r, n), out_dtype),
        scratch_shapes=[pltpu.VMEM((k, tn), BF16)],
        compiler_params=_cparams(("arbitrary", "arbitrary")), name=name)(*args)


def _merge_kernel(ya_ref, yb_ref, yc_ref, ga_ref, gb_ref, gc_ref, w_ref, o_ref, wb_ref):
    @pl.when(pl.program_id(1) == 0)
    def _():
        wb_ref[...] = w_ref[...].astype(BF16)

    acc = _sigmoid(ga_ref[...]) * _dot(ya_ref[...], wb_ref[0])
    acc = acc + _sigmoid(gb_ref[...]) * _dot(yb_ref[...], wb_ref[1])
    acc = acc + _sigmoid(gc_ref[...]) * _dot(yc_ref[...], wb_ref[2])
    o_ref[...] = acc.astype(o_ref.dtype)


def _merge_call(ya, yb, yc, z, w_branch, l, gate_col0, *, name):
    r, db = ya.shape
    _, nbr, _, d = w_branch.shape
    tn = _pick_tile(d, 1024, LANES)
    tm = _pick_tile(r, 384, BF16_SUBLANES)
    assert gate_col0 % tn == 0 and d % tn == 0
    g0 = gate_col0 // tn
    per = d // tn
    y_spec = pl.BlockSpec((tm, db), lambda j, i: (i, 0))
    g_specs = [pl.BlockSpec((tm, tn), lambda j, i, n=n: (i, g0 + per * n + j)) for n in range(nbr)]
    return pl.pallas_call(
        _merge_kernel, grid=(d // tn, r // tm),
        in_specs=[y_spec, y_spec, y_spec] + g_specs + [pl.BlockSpec((None, nbr, db, tn), lambda j, i: (l, 0, 0, j))],
        out_specs=pl.BlockSpec((tm, tn), lambda j, i: (i, j)),
        out_shape=jax.ShapeDtypeStruct((r, d), BF16),
        scratch_shapes=[pltpu.VMEM((nbr, db, tn), BF16)],
        compiler_params=_cparams(("arbitrary", "arbitrary")), name=name)(ya, yb, yc, z, z, z, w_branch)


def _shift_rows(x, prev8, k, row):
    rolled = pltpu.roll(x, k, 0)
    head = jnp.where(row[:SUBLANES] < k, pltpu.roll(prev8, k, 0), rolled[:SUBLANES])
    return jnp.concatenate([head, rolled[SUBLANES:]], axis=0)


def _chunk_conv(x, prev8, w_ref, b_ref, row):
    y = b_ref[...] + w_ref[CONV_W - 1:CONV_W, :] * x
    for k in range(1, CONV_W):
        y = y + w_ref[CONV_W - 1 - k:CONV_W - k, :] * _shift_rows(x, prev8, k, row)
    return y


def _rglru_coeffs(xc, wa_ref, ba_ref, wx_ref, bx_ref, lam_ref):
    r = _sigmoid(_blockdiag(xc, wa_ref) + ba_ref[...])
    i = _sigmoid(_blockdiag(xc, wx_ref) + bx_ref[...])
    log_a = -RG_C * r * _softplus(-lam_ref[...])
    a = jnp.exp(log_a)
    mult = jnp.sqrt(1.0 - a * a)
    return a, mult * i * xc


def _pa_kernel(x_ref, cw_ref, cb_ref, wa_ref, ba_ref, wx_ref, bx_ref, lam_ref,
               y_ref, hl_ref, cv_ref, prev_ref, h_ref, *, pad):
    c = pl.program_id(1)
    last = pl.num_programs(1) - 1
    L = x_ref.shape[0]
    row = lax.broadcasted_iota(jnp.int32, (L, 1), 0)
    valid = jnp.logical_or(c > 0, row >= pad)

    @pl.when(c == 0)
    def _():
        prev_ref[...] = jnp.zeros_like(prev_ref)
        h_ref[...] = jnp.zeros_like(h_ref)

    x = jnp.where(valid, x_ref[...], 0.0)
    xc = _chunk_conv(x, prev_ref[...], cw_ref, cb_ref, row)
    prev_ref[...] = x[L - SUBLANES:]
    a, b = _rglru_coeffs(xc, wa_ref, ba_ref, wx_ref, bx_ref, lam_ref)
    a = jnp.where(valid, a, 1.0)
    b = jnp.where(valid, b, 0.0)
    hs = []
    for j in range(x.shape[1] // LANES):
        sl = slice(j * LANES, (j + 1) * LANES)
        aj, bj = a[:, sl], b[:, sl]
        d = 1
        while d < L:
            keep = row >= d
            a_s = jnp.where(keep, pltpu.roll(aj, d, 0), 1.0)
            b_s = jnp.where(keep, pltpu.roll(bj, d, 0), 0.0)
            bj = aj * b_s + bj
            aj = aj * a_s
            d *= 2
        hs.append(aj * h_ref[0:1, sl] + bj)
    h = jnp.concatenate(hs, axis=-1)
    h_ref[0:1, :] = h[L - 1:L]
    y_ref[...] = h.astype(y_ref.dtype)

    @pl.when(c == last)
    def _():
        hl_ref[0] = h[L - 1:L]
        cv_ref[0] = x[L - (CONV_W - 1):]


def _pa_call(z, nseq, nch, pad, p, *, name):
    rows = nseq * nch * CHUNK
    c = p['rg_conv_w'].shape[1]
    full = lambda a: pl.BlockSpec(a.shape, lambda s, k: (0,) * a.ndim)
    params = [p['rg_conv_w'], p['rg_conv_b'], p['rg_wa'], p['rg_ba'], p['rg_wx'], p['rg_bx'], p['rg_lambda']]
    return pl.pallas_call(
        functools.partial(_pa_kernel, pad=pad), grid=(nseq, nch),
        in_specs=[pl.BlockSpec((CHUNK, c), lambda s, k: (s * nch + k, 0))] + [full(a) for a in params],
        out_specs=[pl.BlockSpec((CHUNK, c), lambda s, k: (s * nch + k, 0)),
                   pl.BlockSpec((1, 1, c), lambda s, k: (s, 0, 0)),
                   pl.BlockSpec((1, CONV_W - 1, c), lambda s, k: (s, 0, 0))],
        out_shape=[jax.ShapeDtypeStruct((rows, c), BF16),
                   jax.ShapeDtypeStruct((nseq, 1, c), F32),
                   jax.ShapeDtypeStruct((nseq, CONV_W - 1, c), F32)],
        scratch_shapes=[pltpu.VMEM((SUBLANES, c), F32), pltpu.VMEM((SUBLANES, c), F32)],
        compiler_params=_cparams(("arbitrary", "arbitrary")), name=name)(z, *params)


def _step_conv(x, cs_ref, w_ref, b_ref):
    y = b_ref[...] + w_ref[CONV_W - 1:CONV_W, :] * x
    for k in range(CONV_W - 1):
        y = y + w_ref[k:k + 1, :] * cs_ref[k]
    return y


def _step_conv_state(x, cs_ref, out_ref):
    for k in range(CONV_W - 2):
        out_ref[k] = cs_ref[k + 1]
    out_ref[CONV_W - 2] = x


def _sa_kernel(x_ref, cs_ref, h0_ref, cw_ref, cb_ref, wa_ref, ba_ref, wx_ref, bx_ref, lam_ref,
               y_ref, hn_ref, cv_ref):
    x = x_ref[...]
    xc = _step_conv(x, cs_ref, cw_ref, cb_ref)
    a, b = _rglru_coeffs(xc, wa_ref, ba_ref, wx_ref, bx_ref, lam_ref)
    h = a * h0_ref[...] + b
    hn_ref[...] = h
    y_ref[...] = h.astype(y_ref.dtype)
    _step_conv_state(x, cs_ref, cv_ref)


def _sa_call(z, row0, bs, cs, h0, p, *, name):
    c = h0.shape[1]
    assert row0 % bs == 0
    full = lambda a: pl.BlockSpec(a.shape, lambda i: (0,) * a.ndim)
    params = [p['rg_conv_w'], p['rg_conv_b'], p['rg_wa'], p['rg_ba'], p['rg_wx'], p['rg_bx'], p['rg_lambda']]
    return pl.pallas_call(
        _sa_kernel, grid=(1,),
        in_specs=[pl.BlockSpec((bs, c), lambda i: (row0 // bs, 0)), full(cs), full(h0)] + [full(a) for a in params],
        out_specs=[pl.BlockSpec((bs, c), lambda i: (0, 0)), pl.BlockSpec((bs, c), lambda i: (0, 0)),
                   pl.BlockSpec(cs.shape, lambda i: (0, 0, 0))],
        out_shape=[jax.ShapeDtypeStruct((bs, c), BF16), jax.ShapeDtypeStruct((bs, c), F32),
                   jax.ShapeDtypeStruct(cs.shape, F32)],
        compiler_params=_cparams(("arbitrary",)), name=name)(z, cs, h0, *params)


def _ret_log_g(h):
    return math.log1p(-(2.0 ** (-5.0 - h)))


def _rope(x, cos2, sin2):
    return x * cos2 + pltpu.roll(x, x.shape[1] // 2, 1) * sin2


def _pb_kernel(q_ref, k_ref, v_ref, g_ref, cos_ref, sin_ref, gn_ref, y_ref, so_ref, s_ref, *, pad, heads):
    c = pl.program_id(1)
    last = pl.num_programs(1) - 1
    L = q_ref.shape[0]
    dk = q_ref.shape[1] // heads
    dv = v_ref.shape[1] // heads
    row = lax.broadcasted_iota(jnp.int32, (L, 1), 0)
    valid = jnp.logical_or(c > 0, row >= pad)

    @pl.when(c == 0)
    def _():
        s_ref[...] = jnp.zeros_like(s_ref)

    t = row.astype(F32)
    rel = t - lax.broadcasted_iota(jnp.int32, (1, L), 1).astype(F32)
    cos2, sin2 = cos_ref[...], sin_ref[...]
    for h in range(heads):
        lg = _ret_log_g(h)
        qh = jnp.where(valid, _rope(q_ref[:, h * dk:(h + 1) * dk], cos2, sin2), 0.0)
        kh = jnp.where(valid, _rope(k_ref[:, h * dk:(h + 1) * dk], cos2, sin2), 0.0) * (dk ** -0.5)
        vh = jnp.where(valid, v_ref[:, h * dv:(h + 1) * dv], 0.0)
        decay = jnp.where(rel >= 0, jnp.exp(jnp.maximum(rel, 0.0) * lg), 0.0)
        qb, vb = qh.astype(BF16), vh.astype(BF16)
        scores = _dot_nt(qb, kh.astype(BF16)) * decay
        s_old = s_ref[h]
        o = _dot(scores.astype(BF16), vb) + _dot(qb, s_old.astype(BF16)) * jnp.exp((t + 1.0) * lg)
        k_dec = kh * jnp.exp((L - 1.0 - t) * lg)
        s_ref[h] = math.exp(L * lg) * s_old + _dot_tn(k_dec.astype(BF16), vb)
        sl = slice(h * dv, (h + 1) * dv)
        y_ref[:, sl] = (_head_norm(o) * gn_ref[:, sl] * _silu(g_ref[:, sl])).astype(y_ref.dtype)

    @pl.when(c == last)
    def _():
        so_ref[0] = s_ref[...]


def _pb_call(z, nseq, nch, pad, heads, dk, dv, cos2, sin2, ret_norm, *, name):
    rows = nseq * nch * CHUNK
    hk, hv = heads * dk, heads * dv
    assert hv % hk == 0
    q0 = hv // hk
    return pl.pallas_call(
        functools.partial(_pb_kernel, pad=pad, heads=heads), grid=(nseq, nch),
        in_specs=[pl.BlockSpec((CHUNK, hk), lambda s, k: (s * nch + k, q0)),
                  pl.BlockSpec((CHUNK, hk), lambda s, k: (s * nch + k, q0 + 1)),
                  pl.BlockSpec((CHUNK, hv), lambda s, k: (s * nch + k, 2)),
                  pl.BlockSpec((CHUNK, hv), lambda s, k: (s * nch + k, 3)),
                  pl.BlockSpec((CHUNK, dk), lambda s, k: (k, 0)),
                  pl.BlockSpec((CHUNK, dk), lambda s, k: (k, 0)),
                  pl.BlockSpec((1, hv), lambda s, k: (0, 0))],
        out_specs=[pl.BlockSpec((CHUNK, hv), lambda s, k: (s * nch + k, 0)),
                   pl.BlockSpec((1, heads, dk, dv), lambda s, k: (s, 0, 0, 0))],
        out_shape=[jax.ShapeDtypeStruct((rows, hv), BF16),
                   jax.ShapeDtypeStruct((nseq, heads, dk, dv), F32)],
        scratch_shapes=[pltpu.VMEM((heads, dk, dv), F32)],
        compiler_params=_cparams(("arbitrary", "arbitrary")), name=name)(z, z, z, z, cos2, sin2, ret_norm)


def _first_pass_only(body, state_out_ref):
    @pl.when(pl.program_id(0) == 0)
    def _():
        body()

    @pl.when(pl.program_id(0) > 0)
    def _():
        state_out_ref[...] = jnp.zeros_like(state_out_ref)


def _pass_maps(l, prev, nblk, depth):
    if prev is not None:
        return 1, (lambda g, i: i), (lambda g, i: l)
    assert l == 0
    return depth, (lambda g, i: jnp.where(g == 0, i, nblk - 1)), (lambda g, i: g)


def _sb_kernel(*refs, heads):
    _first_pass_only(functools.partial(_sb_body, *refs, heads=heads), refs[-1])


def _sb_body(q_ref, k_ref, v_ref, g_ref, s_ref, cos_ref, sin_ref, gn_ref, y_ref, so_ref, *, heads):
    nb = q_ref.shape[0]
    dk = q_ref.shape[1] // heads
    dv = v_ref.shape[1] // heads
    row = lax.broadcasted_iota(jnp.int32, (nb, 1), 0)
    cos2, sin2 = cos_ref[...], sin_ref[...]
    for h in range(heads):
        g = math.exp(_ret_log_g(h))
        qh = _rope(q_ref[:, h * dk:(h + 1) * dk], cos2, sin2)
        kh = _rope(k_ref[:, h * dk:(h + 1) * dk], cos2, sin2) * (dk ** -0.5)
        vh = v_ref[:, h * dv:(h + 1) * dv]
        qb, vb = qh.astype(BF16), vh.astype(BF16)
        cross = jnp.zeros((nb, dv), F32)
        for i in range(nb):
            s_old = s_ref[i, h]
            cross = jnp.where(row == i, _dot(qb, s_old.astype(BF16)), cross)
            so_ref[i, h] = g * s_old + _dot_tn(jnp.where(row == i, kh, 0.0), vh)
        qk = jnp.sum(qb.astype(F32) * kh.astype(BF16).astype(F32), axis=-1, keepdims=True)
        o = qk * vb.astype(F32) + g * cross
        sl = slice(h * dv, (h + 1) * dv)
        y_ref[:, sl] = (_head_norm(o) * gn_ref[:, sl] * _silu(g_ref[:, sl])).astype(y_ref.dtype)


def _sb_call(z, row0, bs, heads, dk, dv, state, l, prev, cos2, sin2, ret_norm, *, name):
    hk, hv = heads * dk, heads * dv
    nb = SUBLANES
    assert bs % nb == 0 and row0 % nb == 0
    r0 = row0 // nb
    q0 = hv // hk
    passes, blk, lay = _pass_maps(l, prev, bs // nb, state.shape[0])
    return pl.pallas_call(
        _skip_ref(functools.partial(_sb_kernel, heads=heads), 8 if prev is not None else None),
        grid=(passes, bs // nb),
        in_specs=[pl.BlockSpec((nb, hk), lambda g, i: (r0 + blk(g, i), q0)),
                  pl.BlockSpec((nb, hk), lambda g, i: (r0 + blk(g, i), q0 + 1)),
                  pl.BlockSpec((nb, hv), lambda g, i: (r0 + blk(g, i), 2)),
                  pl.BlockSpec((nb, hv), lambda g, i: (r0 + blk(g, i), 3)),
                  pl.BlockSpec((None, nb, heads, dk, dv), lambda g, i: (l, blk(g, i), 0, 0, 0)),
                  pl.BlockSpec((1, dk), lambda g, i: (0, 0)),
                  pl.BlockSpec((1, dk), lambda g, i: (0, 0)),
                  pl.BlockSpec((1, hv), lambda g, i: (0, 0))] + _prev_specs(prev),
        out_specs=[pl.BlockSpec((nb, hv), lambda g, i: (blk(g, i), 0)),
                   pl.BlockSpec((None, nb, heads, dk, dv), lambda g, i: (lay(g, i), i, 0, 0, 0))],
        out_shape=[jax.ShapeDtypeStruct((bs, hv), BF16), jax.ShapeDtypeStruct(state.shape, F32)],
        input_output_aliases={8: 1} if prev is not None else {},
        compiler_params=_cparams(("arbitrary", "arbitrary")), name=name)(
            z, z, z, z, state, cos2, sin2, ret_norm, *([] if prev is None else [prev]))


def _mlstm_qkv_gates(x, xc, wq_ref, wk_ref, wv_ref, wif_ref, bif_ref, dh):
    q = _blockdiag(xc, wq_ref)
    k = _blockdiag(xc, wk_ref, scale=dh ** -0.5)
    v = _blockdiag(x, wv_ref)
    c = x.shape[1]
    gates = (_dot_hi(q, wif_ref[0:c, :]) + _dot_hi(k, wif_ref[c:2 * c, :])
             + _dot_hi(v, wif_ref[2 * c:3 * c, :]) + bif_ref[...])
    return q, k, v, gates


def _pc_kernel(x_ref, o_ref, cw_ref, cb_ref, wq_ref, wk_ref, wv_ref, wif_ref, bif_ref, gn_ref, sk_ref,
               y_ref, co_ref, no_ref, mo_ref, cv_ref, prev_ref, c_ref, n_ref, m_ref, *, pad, heads):
    c = pl.program_id(1)
    last = pl.num_programs(1) - 1
    L = x_ref.shape[0]
    dh = x_ref.shape[1] // heads
    row = lax.broadcasted_iota(jnp.int32, (L, 1), 0)
    valid = jnp.logical_or(c > 0, row >= pad)

    @pl.when(c == 0)
    def _():
        prev_ref[...] = jnp.zeros_like(prev_ref)
        c_ref[...] = jnp.zeros_like(c_ref)
        n_ref[...] = jnp.zeros_like(n_ref)
        m_ref[...] = jnp.zeros_like(m_ref)

    x = jnp.where(valid, x_ref[...], 0.0)
    xc = _silu(_chunk_conv(x, prev_ref[...], cw_ref, cb_ref, row))
    prev_ref[...] = x[L - SUBLANES:]
    q, k, v, gates = _mlstm_qkv_gates(x, xc, wq_ref, wk_ref, wv_ref, wif_ref, bif_ref, dh)
    lane = lax.broadcasted_iota(jnp.int32, gates.shape, 1)
    is_i = lane < heads
    ig_c = jnp.where(jnp.logical_and(valid, is_i), gates, NEG)
    lf_c = jnp.where(valid, _log_sigmoid(gates), 0.0)
    ig_r = ig_c.T
    lf_r = lf_c.T
    ti = lax.broadcasted_iota(jnp.int32, (L, L), 0)
    si = lax.broadcasted_iota(jnp.int32, (L, L), 1)
    causal = ti >= si
    tril = causal.astype(F32)
    b_c = _dot_hi(tril, lf_c)
    b_r = _dot_hi(lf_r, (si >= ti).astype(F32))
    m_all = m_ref[...]
    m_lane = lax.broadcasted_iota(jnp.int32, m_all.shape, 1)
    m_new = jnp.zeros_like(m_all)
    for h in range(heads):
        sl = slice(h * dh, (h + 1) * dh)
        bc = b_c[:, heads + h:heads + h + 1]
        br = b_r[heads + h:heads + h + 1, :]
        igr = ig_r[h:h + 1, :]
        igc = ig_c[:, h:h + 1]
        m_prev = m_all[0:1, h:h + 1]
        log_d = jnp.where(causal, bc - br + igr, NEG)
        log_inter = bc + m_prev
        m_t = jnp.maximum(log_inter, jnp.max(log_d, axis=-1, keepdims=True))
        d_m = jnp.exp(log_d - m_t)
        w_inter = jnp.exp(log_inter - m_t)
        qh, kh, vh = q[:, sl], k[:, sl], v[:, sl]
        qb, kb, vb = qh.astype(BF16), kh.astype(BF16), vh.astype(BF16)
        scores = _dot_nt(qb, kb) * d_m
        c_old = c_ref[h]
        n_old = n_ref[h:h + 1, :]
        num = _dot(scores.astype(BF16), vb) + w_inter * _dot(qb, c_old.astype(BF16))
        den = jnp.sum(scores, axis=-1, keepdims=True) + w_inter * jnp.sum(qh * n_old, axis=-1, keepdims=True)
        hh = num / jnp.maximum(jnp.abs(den), jnp.exp(-m_t))
        m_end = m_t[L - 1:L]
        b_last = bc[L - 1:L]
        w_c = jnp.exp(b_last - bc + igc - m_end)
        decay_c = jnp.exp(b_last + m_prev - m_end)
        kw = kh * w_c
        c_ref[h] = decay_c * c_old + _dot_tn(kw.astype(BF16), vb)
        n_ref[h:h + 1, :] = decay_c * n_old + jnp.sum(kw, axis=0, keepdims=True)
        m_new = jnp.where(m_lane == h, m_end, m_new)
        hm = _sigmoid(o_ref[:, sl]) * hh
        y_ref[:, sl] = (_head_norm(hm) * gn_ref[:, sl] + sk_ref[:, sl] * xc[:, sl]).astype(y_ref.dtype)
    m_ref[...] = m_new

    @pl.when(c == last)
    def _():
        co_ref[0] = c_ref[...]
        no_ref[0] = n_ref[...]
        mo_ref[0] = m_ref[...]
        cv_ref[0] = x[L - (CONV_W - 1):]


def _pc_call(z, nseq, nch, pad, heads, xcol, p, *, name):
    rows = nseq * nch * CHUNK
    c = p['m_conv_w'].shape[1]
    dh = c // heads
    full = lambda a: pl.BlockSpec(a.shape, lambda s, k: (0,) * a.ndim)
    params = [p['m_conv_w'], p['m_conv_b'], p['m_wq'], p['m_wk'], p['m_wv'], p['m_w_if'], p['m_b_if'],
              p['m_norm'], p['m_skip']]
    return pl.pallas_call(
        functools.partial(_pc_kernel, pad=pad, heads=heads), grid=(nseq, nch),
        in_specs=[pl.BlockSpec((CHUNK, c), lambda s, k: (s * nch + k, xcol)),
                  pl.BlockSpec((CHUNK, c), lambda s, k: (s * nch + k, xcol + 1))] + [full(a) for a in params],
        out_specs=[pl.BlockSpec((CHUNK, c), lambda s, k: (s * nch + k, 0)),
                   pl.BlockSpec((1, heads, dh, dh), lambda s, k: (s, 0, 0, 0)),
                   pl.BlockSpec((1, SUBLANES, dh), lambda s, k: (s, 0, 0)),
                   pl.BlockSpec((1, SUBLANES, LANES), lambda s, k: (s, 0, 0)),
                   pl.BlockSpec((1, CONV_W - 1, c), lambda s, k: (s, 0, 0))],
        out_shape=[jax.ShapeDtypeStruct((rows, c), BF16),
                   jax.ShapeDtypeStruct((nseq, heads, dh, dh), F32),
                   jax.ShapeDtypeStruct((nseq, SUBLANES, dh), F32),
                   jax.ShapeDtypeStruct((nseq, SUBLANES, LANES), F32),
                   jax.ShapeDtypeStruct((nseq, CONV_W - 1, c), F32)],
        scratch_shapes=[pltpu.VMEM((SUBLANES, c), F32), pltpu.VMEM((heads, dh, dh), F32),
                        pltpu.VMEM((SUBLANES, dh), F32), pltpu.VMEM((SUBLANES, LANES), F32)],
        compiler_params=_cparams(("arbitrary", "arbitrary")), name=name)(z, z, *params)


def _sc_kernel(*refs, heads):
    _first_pass_only(functools.partial(_sc_body, *refs, heads=heads), refs[-4])


def _sc_body(x_ref, o_ref, cs_ref, c_ref, n_ref, m_ref, cw_ref, cb_ref, wq_ref, wk_ref, wv_ref, wif_ref, bif_ref,
             gn_ref, sk_ref, y_ref, co_ref, no_ref, mo_ref, cv_ref, *, heads):
    nb = x_ref.shape[0]
    dh = x_ref.shape[1] // heads
    row = lax.broadcasted_iota(jnp.int32, (nb, 1), 0)
    x = x_ref[...]
    xc = _silu(_step_conv(x, cs_ref, cw_ref, cb_ref))
    q, k, v, gates = _mlstm_qkv_gates(x, xc, wq_ref, wk_ref, wv_ref, wif_ref, bif_ref, dh)
    lf = _log_sigmoid(gates)
    m_old = m_ref[...]
    lane = lax.broadcasted_iota(jnp.int32, m_old.shape, 1)
    m_new = jnp.zeros_like(m_old)
    for h in range(heads):
        sl = slice(h * dh, (h + 1) * dh)
        ig = gates[:, h:h + 1]
        log_inter = lf[:, heads + h:heads + h + 1] + m_old[:, h:h + 1]
        m_t = jnp.maximum(log_inter, ig)
        d_m = jnp.exp(ig - m_t)
        w_inter = jnp.exp(log_inter - m_t)
        qh, kh, vh = q[:, sl], k[:, sl], v[:, sl]
        qb, kb, vb = qh.astype(BF16), kh.astype(BF16), vh.astype(BF16)
        sc = jnp.sum(qb.astype(F32) * kb.astype(F32), axis=-1, keepdims=True) * d_m
        n_old = n_ref[:, sl]
        kw = kh * d_m
        qc = jnp.zeros((nb, dh), F32)
        for i in range(nb):
            c_old = c_ref[i, h]
            qc = jnp.where(row == i, _dot(qb, c_old.astype(BF16)), qc)
            co_ref[i, h] = (w_inter[i:i + 1] * c_old
                            + _dot_tn(jnp.where(row == i, kw, 0.0), vh))
        num = sc * vb.astype(F32) + w_inter * qc
        den = sc + w_inter * jnp.sum(qh * n_old, axis=-1, keepdims=True)
        hh = num / jnp.maximum(jnp.abs(den), jnp.exp(-m_t))
        no_ref[:, sl] = w_inter * n_old + kw
        m_new = jnp.where(lane == h, m_t, m_new)
        hm = _sigmoid(o_ref[:, sl]) * hh
        y_ref[:, sl] = (_head_norm(hm) * gn_ref[:, sl] + sk_ref[:, sl] * xc[:, sl]).astype(y_ref.dtype)
    mo_ref[...] = m_new
    _step_conv_state(x, cs_ref, cv_ref)


def _sc_call(z, row0, bs, heads, xcol, cs, c_state, l, prev, n_state, m_state, p, *, name):
    c = p['m_conv_w'].shape[1]
    dh = c // heads
    nb = SUBLANES
    assert bs % nb == 0 and row0 % nb == 0
    r0 = row0 // nb
    full = lambda a: pl.BlockSpec(a.shape, lambda g, i: (0,) * a.ndim)
    params = [p['m_conv_w'], p['m_conv_b'], p['m_wq'], p['m_wk'], p['m_wv'], p['m_w_if'], p['m_b_if'],
              p['m_norm'], p['m_skip']]
    passes, blk, lay = _pass_maps(l, prev, bs // nb, c_state.shape[0])
    cs_spec = pl.BlockSpec((CONV_W - 1, nb, c), lambda g, i: (0, blk(g, i), 0))
    row_spec = lambda w: pl.BlockSpec((nb, w), lambda g, i: (blk(g, i), 0))
    return pl.pallas_call(
        _skip_ref(functools.partial(_sc_kernel, heads=heads), 6 + len(params) if prev is not None else None),
        grid=(passes, bs // nb),
        in_specs=[pl.BlockSpec((nb, c), lambda g, i: (r0 + blk(g, i), xcol)),
                  pl.BlockSpec((nb, c), lambda g, i: (r0 + blk(g, i), xcol + 1)),
                  cs_spec,
                  pl.BlockSpec((None, nb, heads, dh, dh), lambda g, i: (l, blk(g, i), 0, 0, 0)),
                  row_spec(c), row_spec(LANES)] + [full(a) for a in params] + _prev_specs(prev),
        out_specs=[row_spec(c),
                   pl.BlockSpec((None, nb, heads, dh, dh), lambda g, i: (lay(g, i), i, 0, 0, 0)),
                   row_spec(c), row_spec(LANES), cs_spec],
        out_shape=[jax.ShapeDtypeStruct((bs, c), BF16), jax.ShapeDtypeStruct(c_state.shape, F32),
                   jax.ShapeDtypeStruct((bs, c), F32), jax.ShapeDtypeStruct((bs, LANES), F32),
                   jax.ShapeDtypeStruct(cs.shape, F32)],
        input_output_aliases={6 + len(params): 1} if prev is not None else {},
        compiler_params=_cparams(("arbitrary", "arbitrary")), name=name)(
            z, z, cs, c_state, n_state, m_state, *params, *([] if prev is None else [prev]))


def _ffn_kernel(be_ref, nv_ref, x_ref, wg_ref, wu_ref, wd_ref, o_ref):
    del be_ref

    @pl.when(pl.program_id(0) < nv_ref[0])
    def _():
        x = x_ref[...].astype(BF16)
        hmid = (_silu(_dot(x, wg_ref[...])) * _dot(x, wu_ref[...])).astype(BF16)
        o_ref[...] = _dot(hmid, wd_ref[...])

    @pl.when(pl.program_id(0) >= nv_ref[0])
    def _():
        o_ref[...] = jnp.zeros_like(o_ref)


def _ffn_call(xs, block_e, n_valid, wg, wu, wd, l, *, name):
    ns, d = xs.shape
    f = wg.shape[3]
    nb = ns // MOE_ROWS
    row = lambda i, be, nv: (jnp.minimum(i, nv[0] - 1), 0)
    grid_spec = pltpu.PrefetchScalarGridSpec(
        num_scalar_prefetch=2, grid=(nb,),
        in_specs=[pl.BlockSpec((MOE_ROWS, d), row),
                  pl.BlockSpec((None, None, d, f), lambda i, be, nv: (l, be[i], 0, 0)),
                  pl.BlockSpec((None, None, d, f), lambda i, be, nv: (l, be[i], 0, 0)),
                  pl.BlockSpec((None, None, f, d), lambda i, be, nv: (l, be[i], 0, 0))],
        out_specs=pl.BlockSpec((MOE_ROWS, d), lambda i, be, nv: (i, 0)))
    return pl.pallas_call(
        _ffn_kernel, grid_spec=grid_spec, out_shape=jax.ShapeDtypeStruct((ns, d), F32),
        compiler_params=_cparams(("arbitrary",)), name=name)(block_e, n_valid, xs, wg, wu, wd)


def _route_tables(ids, tile_cnt, r, tm):
    nt = r // tm
    eidx = jnp.arange(N_EXPERTS, dtype=jnp.int32)
    cnt_t = tile_cnt[:, 0, :N_EXPERTS].astype(jnp.int32)
    counts = jnp.sum(cnt_t, axis=0)
    tile_off = jnp.cumsum(cnt_t, axis=0) - cnt_t
    padded = (counts + MOE_ROWS - 1) // MOE_ROWS * MOE_ROWS
    pad_end = jnp.cumsum(padded)
    pad_start = pad_end - padded
    base = jnp.repeat(pad_start[None, :] + tile_off, tm, axis=0)
    e = ids[:, :TOP_K]
    rank = ids[:, TOP_K:2 * TOP_K]
    dest = rank + jnp.sum(jnp.where(e[:, :, None] == eidx, base[:, None, :], 0), axis=-1)
    nb = r * TOP_K // MOE_ROWS + N_EXPERTS
    blk0 = jnp.arange(nb, dtype=jnp.int32) * MOE_ROWS
    block_e = jnp.minimum(jnp.sum(pad_end[None, :] <= blk0[:, None], axis=1), N_EXPERTS - 1).astype(jnp.int32)
    n_valid = (pad_end[-1:] // MOE_ROWS).astype(jnp.int32)
    return dest.astype(jnp.int32), block_e, n_valid, nb * MOE_ROWS


def _expand_blockdiag(w):
    nb, bs, _ = w.shape
    if bs == LANES:
        return w.astype(BF16)
    per = LANES // bs
    eye = jnp.eye(per, dtype=w.dtype)
    wt = w.reshape(nb // per, per, bs, bs)
    out = jnp.einsum('tpbc,pq->tpbqc', wt, eye)
    return out.reshape(nb // per, LANES, LANES).astype(BF16)


def _rope_tables(pos, half):
    freq = ROPE_BASE ** (-jnp.arange(half, dtype=F32) / half)
    ang = pos[:, None] * freq[None, :]
    cos, sin = jnp.cos(ang), jnp.sin(ang)
    return jnp.concatenate([cos, cos], axis=-1), jnp.concatenate([-sin, sin], axis=-1)


def kernel(x_prompt, x_sample, state_rglru_h, state_rglru_conv, state_ret, state_mlstm_C, state_mlstm_n,
           state_mlstm_m, state_mlstm_conv, meta_tokens, norm_mix, norm_ffn, norm_final, w_in, rg_conv_w,
           rg_conv_b, rg_wa, rg_ba, rg_wx, rg_bx, rg_lambda, ret_norm, m_conv_w, m_conv_b, m_wq, m_wk, m_wv,
           m_w_if, m_b_if, m_norm, m_skip, w_branch, w_out, moe_w_group, moe_b_group, moe_w_expert,
           moe_b_expert, moe_w_gate, moe_w_up, moe_w_down):
    bp, seq, d = x_prompt.shape
    bs = x_sample.shape[0]
    n_meta = meta_tokens.shape[0]
    depth = w_in.shape[0]
    d_rnn = state_rglru_h.shape[2]
    _, _, r_heads, r_dk, r_dv = state_ret.shape
    m_heads, m_dh = state_mlstm_C.shape[2], state_mlstm_C.shape[3]
    d_m = m_heads * m_dh
    assert seq % CHUNK == 0 and n_meta <= CHUNK and x_sample.shape[1] == 1
    pad = CHUNK - n_meta
    nch = 1 + seq // CHUNK
    rp = bp * nch * CHUNK
    r = rp + bs
    assert d_rnn == r_heads * r_dv == d_m
    xm_col = (d_rnn + 2 * r_heads * r_dk + 2 * r_heads * r_dv) // d_m
    gate_col0 = d_rnn + 2 * r_heads * r_dk + 2 * r_heads * r_dv + 2 * d_m

    meta = jnp.broadcast_to(meta_tokens[None], (bp, n_meta, d))
    xp = jnp.concatenate([jnp.zeros((bp, pad, d), F32), meta, x_prompt], axis=1).reshape(rp, d)
    x = jnp.concatenate([xp, x_sample.reshape(bs, d)], axis=0)

    pos_p = jnp.arange(nch * CHUNK, dtype=F32) - pad
    cos_p, sin_p = _rope_tables(pos_p, r_dk // 2)
    pos_s = jnp.full((1,), float(PAST_LEN), F32)
    cos_s, sin_s = _rope_tables(pos_s, r_dk // 2)

    tm = _pick_tile(r, 384, BF16_SUBLANES)
    assert rp % CHUNK == 0 and bs % BF16_SUBLANES == 0
    wg_b, wu_b, wd_b = moe_w_gate.astype(BF16), moe_w_up.astype(BF16), moe_w_down.astype(BF16)
    prompt_states, sample_states = [], []
    combine = None
    ret_s = c_s = None
    for l in range(depth):
        p = {'rg_conv_w': rg_conv_w[l], 'rg_conv_b': rg_conv_b[l].reshape(1, -1),
             'rg_wa': _expand_blockdiag(rg_wa[l]), 'rg_ba': rg_ba[l].reshape(1, -1),
             'rg_wx': _expand_blockdiag(rg_wx[l]), 'rg_bx': rg_bx[l].reshape(1, -1),
             'rg_lambda': rg_lambda[l].reshape(1, -1),
             'm_conv_w': m_conv_w[l], 'm_conv_b': m_conv_b[l].reshape(1, -1),
             'm_wq': _expand_blockdiag(m_wq[l]), 'm_wk': _expand_blockdiag(m_wk[l]),
             'm_wv': _expand_blockdiag(m_wv[l]),
             'm_w_if': jnp.pad(m_w_if[l], ((0, 0), (0, LANES - 2 * m_heads))),
             'm_b_if': jnp.pad(m_b_if[l], (0, LANES - 2 * m_heads)).reshape(1, LANES),
             'm_norm': m_norm[l].reshape(1, -1), 'm_skip': m_skip[l].reshape(1, -1)}
        gn_ret = ret_norm[l].reshape(1, -1)

        x, h = _norm_call(x, norm_mix[l], tm=tm, grid=(r // tm,), row_tile=lambda i: i, want_x=True,
                          h_shape=(r, d), h_dtype=BF16, h_spec=pl.BlockSpec((tm, d), lambda i: (i, 0)),
                          combine=combine, name=f'norm_mix{l}')
        z = _mm_call(h, w_in, l, name=f'in_proj{l}')

        ya_p, rgh_p, rgc_p = _pa_call(z, bp, nch, pad, p, name=f'rglru_p{l}')
        yb_p, ret_p = _pb_call(z, bp, nch, pad, r_heads, r_dk, r_dv, cos_p, sin_p, gn_ret, name=f'ret_p{l}')
        yc_p, c_p, n_p, m_p, mc_p = _pc_call(z, bp, nch, pad, m_heads, xm_col, p, name=f'mlstm_p{l}')

        rg_cs = jnp.transpose(state_rglru_conv[l], (1, 0, 2))
        ya_s, rgh_s, rgc_s = _sa_call(z, rp, bs, rg_cs, state_rglru_h[l], p, name=f'rglru_s{l}')
        yb_s, ret_s = _sb_call(z, rp, bs, r_heads, r_dk, r_dv, state_ret, l, ret_s, cos_s, sin_s, gn_ret,
                               name=f'ret_s{l}')
        m_cs = jnp.transpose(state_mlstm_conv[l], (1, 0, 2))
        m_in = jnp.pad(state_mlstm_m[l], ((0, 0), (0, LANES - m_heads)))
        yc_s, c_s, n_s, m_s, mc_s = _sc_call(z, rp, bs, m_heads, xm_col, m_cs, state_mlstm_C, l, c_s,
                                             state_mlstm_n[l].reshape(bs, d_m), m_in, p, name=f'mlstm_s{l}')

        prompt_states.append((rgh_p[:, 0], rgc_p, ret_p, c_p, n_p[:, :m_heads], m_p[:, 0, :m_heads], mc_p))
        sample_states.append((rgh_s, jnp.transpose(rgc_s, (1, 0, 2)), None, None,
                              n_s.reshape(bs, m_heads, m_dh), m_s[:, :m_heads], jnp.transpose(mc_s, (1, 0, 2))))

        ya = jnp.concatenate([ya_p, ya_s], axis=0)
        yb = jnp.concatenate([yb_p, yb_s], axis=0)
        yc = jnp.concatenate([yc_p, yc_s], axis=0)
        merged = _merge_call(ya, yb, yc, z, w_branch, l, gate_col0, name=f'merge{l}')
        x = _mm_call(merged, w_out, l, res=x, name=f'out_proj{l}')

        wr = jnp.pad(jnp.concatenate([moe_w_group[l], moe_w_expert[l]], axis=1),
                     ((0, 0), (0, LANES - N_GROUPS - N_EXPERTS)))
        br = jnp.pad(jnp.concatenate([moe_b_group[l], moe_b_expert[l]]), (0, LANES - N_GROUPS - N_EXPERTS))
        h2, ids, wts, tile_cnt = _norm_call(
            x, norm_ffn[l], tm=tm, grid=(r // tm,), row_tile=lambda i: i, want_x=False, h_shape=(r, d),
            h_dtype=F32, h_spec=pl.BlockSpec((tm, d), lambda i: (i, 0)), router=(wr, br.reshape(1, LANES)),
            name=f'norm_router{l}')
        dest, block_e, n_valid, n_slots = _route_tables(ids, tile_cnt, r, tm)
        xs = _dispatch_call(h2, dest, n_slots, tm=tm, name=f'moe_dispatch{l}')
        ys = _ffn_call(xs, block_e, n_valid, wg_b, wu_b, wd_b, l, name=f'moe_ffn{l}')
        combine = (dest, wts, ys)

    (y_prompt,) = _norm_call(
        x, norm_final, tm=CHUNK, grid=(bp, seq // CHUNK), row_tile=lambda b, c: b * nch + 1 + c, want_x=False,
        h_shape=(bp, seq, d), h_dtype=F32, h_spec=pl.BlockSpec((None, CHUNK, d), lambda b, c: (b, c, 0)),
        combine=combine, name='norm_final_p')
    ts = _pick_tile(bs, CHUNK, SUBLANES)
    assert rp % ts == 0
    (y_sample,) = _norm_call(
        x, norm_final, tm=ts, grid=(bs // ts,), row_tile=lambda i: rp // ts + i, want_x=False,
        h_shape=(bs, d), h_dtype=F32, h_spec=pl.BlockSpec((ts, d), lambda i: (i, 0)),
        combine=combine, name='norm_final_s')
    pn = [jnp.stack([s[i] for s in prompt_states], axis=0) for i in range(7)]
    sn = [None if sample_states[0][i] is None else jnp.stack([s[i] for s in sample_states], axis=0)
          for i in range(7)]
    sn[2], sn[3] = ret_s, c_s
    return (y_prompt, y_sample.reshape(bs, 1, d), *pn, *sn)
```

```python
import functools
import math

import jax
import jax.numpy as jnp
import numpy as np
from jax import lax
from jax.experimental import pallas as pl
from jax.experimental.pallas import tpu as pltpu

F32 = jnp.float32
BF16 = jnp.bfloat16

LANES = 128
SUBLANES = 8
BF16_SUBLANES = 16
VMEM_LIMIT_BYTES = 56 * 1024 * 1024

CHUNK = 128
CONV_W = 4
EPS = 1e-6
RG_C = 8.0
ROPE_BASE = 10000.0
PAST_LEN = 16384
N_GROUPS = 4
EXP_PER_GROUP = 8
N_EXPERTS = N_GROUPS * EXP_PER_GROUP
TOP_K = 2
MOE_ROWS = 128
MOE_SB_ROWS = 1024
FFN_CHUNK = 256
NEG = -1e30


def _cparams(sem, vmem=VMEM_LIMIT_BYTES):
    return pltpu.CompilerParams(dimension_semantics=sem, vmem_limit_bytes=vmem)


def _pick_tile(n, cap, mult):
    best = None
    for t in range(mult, min(n, cap) + 1, mult):
        if n % t == 0:
            best = t
    assert best is not None, (n, cap, mult)
    return best


def _skip_ref(fn, idx):
    if idx is None:
        return fn

    def wrapped(*refs):
        return fn(*refs[:idx], *refs[idx + 1:])
    return wrapped


def _prev_specs(prev):
    return [] if prev is None else [pl.BlockSpec(memory_space=pl.ANY)]


def _dot(a, b):
    return jnp.dot(a, b, preferred_element_type=F32)


def _dot_nt(a, b):
    return lax.dot_general(a, b, (((1,), (1,)), ((), ())), preferred_element_type=F32)


def _dot_tn(a, b):
    return lax.dot_general(a, b, (((0,), (0,)), ((), ())), preferred_element_type=F32)


def _split3(x):
    hi = x.astype(BF16)
    r1 = x - hi.astype(F32)
    mid = r1.astype(BF16)
    lo = (r1 - mid.astype(F32)).astype(BF16)
    return hi, mid, lo


def _dot_hi(a, b):
    return jnp.dot(a, b, preferred_element_type=F32, precision=lax.Precision.HIGHEST)


def _sigmoid(x):
    return 1.0 / (1.0 + jnp.exp(-x))


def _silu(x):
    return x * _sigmoid(x)


def _log_sigmoid(x):
    return jnp.minimum(x, 0.0) - jnp.log(1.0 + jnp.exp(-jnp.abs(x)))


def _softplus(x):
    return jnp.maximum(x, 0.0) + jnp.log(1.0 + jnp.exp(-jnp.abs(x)))


def _head_norm(o):
    mu = jnp.mean(o, axis=-1, keepdims=True)
    d = o - mu
    var = jnp.mean(d * d, axis=-1, keepdims=True)
    return d * lax.rsqrt(var + EPS)


def _blockdiag(x, w_ref, scale=None):
    outs = []
    for j in range(w_ref.shape[0]):
        o = _dot(x[:, j * LANES:(j + 1) * LANES].astype(BF16), w_ref[j])
        outs.append(o if scale is None else o * scale)
    return jnp.concatenate(outs, axis=-1)


ROW_DMA_UNROLL = 8


def _row_copies(n_rows, row_copy):
    def start(r, carry):
        for k in range(TOP_K):
            row_copy(r, k).start(priority=k % 2)
        return carry

    def wait(r, carry):
        for k in range(TOP_K):
            row_copy(r, k).wait()
        return carry

    lax.fori_loop(0, n_rows, start, 0, unroll=ROW_DMA_UNROLL)
    lax.fori_loop(0, n_rows, wait, 0, unroll=ROW_DMA_UNROLL)


def _norm_kernel(*refs, combine, want_x, router):
    refs = list(refs)
    if combine:
        dest_ref, wts_in_ref, yb_hbm = refs[:3]
        refs = refs[3:]
    x_ref, g_ref = refs[:2]
    refs = refs[2:]
    if router:
        wr_ref, br_ref = refs[:2]
        refs = refs[2:]
    n_out = int(want_x) + 1 + (3 if router else 0)
    outs, scratch = refs[:n_out], refs[n_out:]
    x = x_ref[...]
    tm = x.shape[0]
    if combine:
        gbuf, gsem = scratch
        _row_copies(tm, lambda r, k: pltpu.make_async_copy(
            yb_hbm.at[pl.ds(dest_ref[0, TOP_K * r + k], 1), :], gbuf.at[k, pl.ds(r, 1), :], gsem))
        for k in range(TOP_K):
            x = x + wts_in_ref[:, k:k + 1] * gbuf[k]
    k = 0
    if want_x:
        outs[k][...] = x
        k += 1
    ms = jnp.mean(x * x, axis=-1, keepdims=True)
    h = x * lax.rsqrt(ms + EPS) * g_ref[...]
    outs[k][...] = h.astype(outs[k].dtype)
    k += 1
    if router:
        ids_ref, wts_ref, cnt_ref = outs[k], outs[k + 1], outs[k + 2]
        logits = _dot_hi(h, wr_ref[...]) + br_ref[...]
        lane = lax.broadcasted_iota(jnp.int32, logits.shape, 1)
        big = jnp.int32(1 << 20)
        is_g = lane < N_GROUPS
        gl = jnp.where(is_g, logits, NEG)
        gmax = jnp.max(gl, axis=-1, keepdims=True)
        gidx = jnp.min(jnp.where(gl == gmax, lane, big), axis=-1, keepdims=True)
        gsum = jnp.sum(jnp.where(is_g, jnp.exp(gl - gmax), 0.0), axis=-1, keepdims=True)
        gprob = 1.0 / gsum
        lo = N_GROUPS + EXP_PER_GROUP * gidx
        em = jnp.where(lane >= lo, jnp.where(lane < lo + EXP_PER_GROUP, logits, NEG), NEG)
        e1v = jnp.max(em, axis=-1, keepdims=True)
        e1i = jnp.min(jnp.where(em == e1v, lane, big), axis=-1, keepdims=True)
        em2 = jnp.where(lane == e1i, NEG, em)
        e2v = jnp.max(em2, axis=-1, keepdims=True)
        e2i = jnp.min(jnp.where(em2 == e2v, lane, big), axis=-1, keepdims=True)
        t = jnp.exp(e2v - e1v)
        w1 = gprob / (1.0 + t)
        w2 = gprob * t / (1.0 + t)
        e1, e2 = e1i - N_GROUPS, e2i - N_GROUPS
        chosen = jnp.where(lane == e1, 1.0, jnp.where(lane == e2, 1.0, 0.0))
        ti = lax.broadcasted_iota(jnp.int32, (tm, tm), 0)
        si = lax.broadcasted_iota(jnp.int32, (tm, tm), 1)
        before = _dot(jnp.where(ti > si, 1.0, 0.0).astype(BF16), chosen.astype(BF16))
        rank1 = jnp.sum(jnp.where(lane == e1, before, 0.0), axis=-1, keepdims=True).astype(jnp.int32)
        rank2 = jnp.sum(jnp.where(lane == e2, before, 0.0), axis=-1, keepdims=True).astype(jnp.int32)
        ids_ref[...] = jnp.where(lane == 0, e1, jnp.where(lane == 1, e2,
                                 jnp.where(lane == 2, rank1, jnp.where(lane == 3, rank2, 0))))
        wts_ref[...] = jnp.where(lane == 0, w1, jnp.where(lane == 1, w2, 0.0))
        cnt_ref[...] = jnp.sum(chosen, axis=0, keepdims=True)


def _norm_call(x, g, *, tm, grid, row_tile, want_x, h_shape, h_dtype, h_spec, combine=None, router=None, name):
    r, d = x.shape
    x_spec = pl.BlockSpec((tm, d), lambda *gi: (row_tile(*gi), 0))
    in_specs, args, scratch = [], [], []
    if combine is not None:
        dest, wts, yb = combine
        in_specs += [pl.BlockSpec((None, 1, TOP_K * tm), lambda *gi: (row_tile(*gi), 0, 0), memory_space=pltpu.SMEM),
                     pl.BlockSpec((tm, LANES), lambda *gi: (row_tile(*gi), 0)),
                     pl.BlockSpec(memory_space=pl.ANY)]
        args += [dest.reshape(r // tm, 1, TOP_K * tm), wts, yb]
        scratch = [pltpu.VMEM((TOP_K, tm, d), F32), pltpu.SemaphoreType.DMA(())]
    in_specs += [x_spec, pl.BlockSpec((1, d), lambda *gi: (0, 0))]
    args += [x, g.reshape(1, d)]
    if router is not None:
        in_specs += [pl.BlockSpec((d, LANES), lambda *gi: (0, 0)), pl.BlockSpec((1, LANES), lambda *gi: (0, 0))]
        args += list(router)
    out_shape, out_specs = [], []
    if want_x:
        out_shape.append(jax.ShapeDtypeStruct((r, d), F32))
        out_specs.append(x_spec)
    out_shape.append(jax.ShapeDtypeStruct(h_shape, h_dtype))
    out_specs.append(h_spec)
    if router is not None:
        out_shape += [jax.ShapeDtypeStruct((r, LANES), jnp.int32), jax.ShapeDtypeStruct((r, LANES), F32),
                      jax.ShapeDtypeStruct((r // tm, 1, LANES), F32)]
        out_specs += [pl.BlockSpec((tm, LANES), lambda *gi: (row_tile(*gi), 0))] * 2
        out_specs += [pl.BlockSpec((None, 1, LANES), lambda *gi: (row_tile(*gi), 0, 0))]
    return pl.pallas_call(
        functools.partial(_norm_kernel, combine=combine is not None, want_x=want_x, router=router is not None),
        grid=grid, in_specs=in_specs, out_specs=out_specs, out_shape=out_shape, scratch_shapes=scratch,
        compiler_params=_cparams(("arbitrary",) * len(grid)), name=name)(*args)


def _zero_blocks(zero_ref, out_hbm, first, last, sem):
    def blk_copy(b):
        return pltpu.make_async_copy(zero_ref, out_hbm.at[pl.ds(pl.multiple_of(b * MOE_ROWS, MOE_ROWS), MOE_ROWS), :], sem)

    def start(b, carry):
        blk_copy(b).start()
        return carry

    def wait(b, carry):
        blk_copy(b).wait()
        return carry

    lax.fori_loop(first, last, start, 0)
    lax.fori_loop(first, last, wait, 0)


def _dispatch_kernel(pe_ref, dest_ref, h_ref, xs_out, zero_ref, sem, zsem):
    tm = h_ref.shape[0]
    n_blocks = xs_out.shape[0] // MOE_ROWS

    @pl.when(pl.program_id(0) == 0)
    def _():
        zero_ref[...] = jnp.zeros_like(zero_ref)

        def last_block(e):
            b = jnp.maximum(pe_ref[e] // MOE_ROWS - 1, 0)
            return pltpu.make_async_copy(
                zero_ref, xs_out.at[pl.ds(pl.multiple_of(b * MOE_ROWS, MOE_ROWS), MOE_ROWS), :], zsem)

        for e in range(N_EXPERTS):
            last_block(e).start()
        for e in range(N_EXPERTS):
            last_block(e).wait()
        _zero_blocks(zero_ref, xs_out, pe_ref[N_EXPERTS - 1] // MOE_ROWS, n_blocks, zsem)

    _row_copies(tm, lambda r, k: pltpu.make_async_copy(
        h_ref.at[pl.ds(r, 1), :], xs_out.at[pl.ds(dest_ref[0, TOP_K * r + k], 1), :], sem))


def _dispatch_call(h2, dest, pad_end, n_slots, *, tm, name):
    r, d = h2.shape
    grid_spec = pltpu.PrefetchScalarGridSpec(
        num_scalar_prefetch=1, grid=(r // tm,),
        in_specs=[pl.BlockSpec((None, 1, TOP_K * tm), lambda i, pe: (i, 0, 0), memory_space=pltpu.SMEM),
                  pl.BlockSpec((tm, d), lambda i, pe: (i, 0))],
        out_specs=pl.BlockSpec(memory_space=pl.ANY),
        scratch_shapes=[pltpu.VMEM((MOE_ROWS, d), h2.dtype), pltpu.SemaphoreType.DMA(()),
                        pltpu.SemaphoreType.DMA(())])
    return pl.pallas_call(
        _dispatch_kernel, grid_spec=grid_spec, out_shape=jax.ShapeDtypeStruct((n_slots, d), h2.dtype),
        compiler_params=_cparams(("arbitrary",)), name=name)(pad_end, dest.reshape(r // tm, 1, TOP_K * tm), h2)


def _mm_kernel(*refs, has_res):
    if has_res:
        x_ref, w_ref, r_ref, o_ref, wb_ref = refs
    else:
        x_ref, w_ref, o_ref, wb_ref = refs

    @pl.when(pl.program_id(1) == 0)
    def _():
        wb_ref[...] = w_ref[...].astype(BF16)

    acc = _dot(x_ref[...], wb_ref[...])
    if has_res:
        acc = acc + r_ref[...]
    o_ref[...] = acc.astype(o_ref.dtype)


def _mm_call(x, w, l, res=None, *, out_dtype=F32, name):
    r, k = x.shape
    n = w.shape[2]
    tm = _pick_tile(r, 1152, BF16_SUBLANES)
    tn = _pick_tile(n, 1024, LANES)
    in_specs = [pl.BlockSpec((tm, k), lambda j, i: (i, 0)), pl.BlockSpec((None, k, tn), lambda j, i: (l, 0, j))]
    args = [x, w]
    if res is not None:
        in_specs.append(pl.BlockSpec((tm, tn), lambda j, i: (i, j)))
        args.append(res)
    return pl.pallas_call(
        functools.partial(_mm_kernel, has_res=res is not None),
        grid=(n // tn, r // tm), in_specs=in_specs,
        out_specs=pl.BlockSpec((tm, tn), lambda j, i: (i, j)),
        out_shape=jax.ShapeDtypeStruct((r, n), out_dtype),
        scratch_shapes=[pltpu.VMEM((k, tn), BF16)],
        compiler_params=_cparams(("arbitrary", "arbitrary")), name=name)(*args)


def _merge_kernel(ya_ref, yb_ref, yc_ref, ga_ref, gb_ref, gc_ref, w_ref, o_ref, wb_ref):
    @pl.when(pl.program_id(1) == 0)
    def _():
        wb_ref[...] = w_ref[...].astype(BF16)

    acc = _sigmoid(ga_ref[...]) * _dot(ya_ref[...], wb_ref[0])
    acc = acc + _sigmoid(gb_ref[...]) * _dot(yb_ref[...], wb_ref[1])
    acc = acc + _sigmoid(gc_ref[...]) * _dot(yc_ref[...], wb_ref[2])
    o_ref[...] = acc.astype(o_ref.dtype)


def _merge_call(ya, yb, yc, z, w_branch, l, gate_col0, *, name):
    r, db = ya.shape
    _, nbr, _, d = w_branch.shape
    tn = _pick_tile(d, 1024, LANES)
    tm = _pick_tile(r, 384, BF16_SUBLANES)
    assert gate_col0 % tn == 0 and d % tn == 0
    g0 = gate_col0 // tn
    per = d // tn
    y_spec = pl.BlockSpec((tm, db), lambda j, i: (i, 0))
    g_specs = [pl.BlockSpec((tm, tn), lambda j, i, n=n: (i, g0 + per * n + j)) for n in range(nbr)]
    return pl.pallas_call(
        _merge_kernel, grid=(d // tn, r // tm),
        in_specs=[y_spec, y_spec, y_spec] + g_specs + [pl.BlockSpec((None, nbr, db, tn), lambda j, i: (l, 0, 0, j))],
        out_specs=pl.BlockSpec((tm, tn), lambda j, i: (i, j)),
        out_shape=jax.ShapeDtypeStruct((r, d), BF16),
        scratch_shapes=[pltpu.VMEM((nbr, db, tn), BF16)],
        compiler_params=_cparams(("arbitrary", "arbitrary")), name=name)(ya, yb, yc, z, z, z, w_branch)


def _shift_rows(x, prev8, k, row):
    rolled = pltpu.roll(x, k, 0)
    head = jnp.where(row[:SUBLANES] < k, pltpu.roll(prev8, k, 0), rolled[:SUBLANES])
    return jnp.concatenate([head, rolled[SUBLANES:]], axis=0)


def _chunk_conv(x, prev8, w_ref, b_ref, row):
    y = b_ref[...] + w_ref[CONV_W - 1:CONV_W, :] * x
    for k in range(1, CONV_W):
        y = y + w_ref[CONV_W - 1 - k:CONV_W - k, :] * _shift_rows(x, prev8, k, row)
    return y


def _rglru_coeffs(xc, wa_ref, ba_ref, wx_ref, bx_ref, lam_ref):
    r = _sigmoid(_blockdiag(xc, wa_ref) + ba_ref[...])
    i = _sigmoid(_blockdiag(xc, wx_ref) + bx_ref[...])
    log_a = -RG_C * r * _softplus(-lam_ref[...])
    a = jnp.exp(log_a)
    mult = jnp.sqrt(1.0 - a * a)
    return a, mult * i * xc


def _pa_kernel(x_ref, cw_ref, cb_ref, wa_ref, ba_ref, wx_ref, bx_ref, lam_ref,
               y_ref, hl_ref, cv_ref, prev_ref, h_ref, *, pad):
    c = pl.program_id(1)
    last = pl.num_programs(1) - 1
    L = x_ref.shape[0]
    row = lax.broadcasted_iota(jnp.int32, (L, 1), 0)
    valid = jnp.logical_or(c > 0, row >= pad)

    @pl.when(c == 0)
    def _():
        prev_ref[...] = jnp.zeros_like(prev_ref)
        h_ref[...] = jnp.zeros_like(h_ref)

    x = jnp.where(valid, x_ref[...], 0.0)
    xc = _chunk_conv(x, prev_ref[...], cw_ref, cb_ref, row)
    prev_ref[...] = x[L - SUBLANES:]
    a, b = _rglru_coeffs(xc, wa_ref, ba_ref, wx_ref, bx_ref, lam_ref)
    a = jnp.where(valid, a, 1.0)
    b = jnp.where(valid, b, 0.0)
    hs = []
    for j in range(x.shape[1] // LANES):
        sl = slice(j * LANES, (j + 1) * LANES)
        aj, bj = a[:, sl], b[:, sl]
        d = 1
        while d < L:
            keep = row >= d
            a_s = jnp.where(keep, pltpu.roll(aj, d, 0), 1.0)
            b_s = jnp.where(keep, pltpu.roll(bj, d, 0), 0.0)
            bj = aj * b_s + bj
            aj = aj * a_s
            d *= 2
        hs.append(aj * h_ref[0:1, sl] + bj)
    h = jnp.concatenate(hs, axis=-1)
    h_ref[0:1, :] = h[L - 1:L]
    y_ref[...] = h.astype(y_ref.dtype)

    @pl.when(c == last)
    def _():
        hl_ref[0] = h[L - 1:L]
        cv_ref[0] = x[L - (CONV_W - 1):]


def _pa_call(z, nseq, nch, pad, p, *, name):
    rows = nseq * nch * CHUNK
    c = p['rg_conv_w'].shape[1]
    full = lambda a: pl.BlockSpec(a.shape, lambda s, k: (0,) * a.ndim)
    params = [p['rg_conv_w'], p['rg_conv_b'], p['rg_wa'], p['rg_ba'], p['rg_wx'], p['rg_bx'], p['rg_lambda']]
    return pl.pallas_call(
        functools.partial(_pa_kernel, pad=pad), grid=(nseq, nch),
        in_specs=[pl.BlockSpec((CHUNK, c), lambda s, k: (s * nch + k, 0))] + [full(a) for a in params],
        out_specs=[pl.BlockSpec((CHUNK, c), lambda s, k: (s * nch + k, 0)),
                   pl.BlockSpec((1, 1, c), lambda s, k: (s, 0, 0)),
                   pl.BlockSpec((1, CONV_W - 1, c), lambda s, k: (s, 0, 0))],
        out_shape=[jax.ShapeDtypeStruct((rows, c), BF16),
                   jax.ShapeDtypeStruct((nseq, 1, c), F32),
                   jax.ShapeDtypeStruct((nseq, CONV_W - 1, c), F32)],
        scratch_shapes=[pltpu.VMEM((SUBLANES, c), F32), pltpu.VMEM((SUBLANES, c), F32)],
        compiler_params=_cparams(("arbitrary", "arbitrary")), name=name)(z, *params)


def _step_conv(x, cs_ref, w_ref, b_ref):
    y = b_ref[...] + w_ref[CONV_W - 1:CONV_W, :] * x
    for k in range(CONV_W - 1):
        y = y + w_ref[k:k + 1, :] * cs_ref[k]
    return y


def _step_conv_state(x, cs_ref, out_ref):
    for k in range(CONV_W - 2):
        out_ref[k] = cs_ref[k + 1]
    out_ref[CONV_W - 2] = x


def _sa_kernel(x_ref, cs_ref, h0_ref, cw_ref, cb_ref, wa_ref, ba_ref, wx_ref, bx_ref, lam_ref,
               y_ref, hn_ref, cv_ref):
    x = x_ref[...]
    xc = _step_conv(x, cs_ref, cw_ref, cb_ref)
    a, b = _rglru_coeffs(xc, wa_ref, ba_ref, wx_ref, bx_ref, lam_ref)
    h = a * h0_ref[...] + b
    hn_ref[...] = h
    y_ref[...] = h.astype(y_ref.dtype)
    _step_conv_state(x, cs_ref, cv_ref)


def _sa_call(z, row0, bs, cs, h0, p, *, name):
    c = h0.shape[1]
    assert row0 % bs == 0
    full = lambda a: pl.BlockSpec(a.shape, lambda i: (0,) * a.ndim)
    params = [p['rg_conv_w'], p['rg_conv_b'], p['rg_wa'], p['rg_ba'], p['rg_wx'], p['rg_bx'], p['rg_lambda']]
    return pl.pallas_call(
        _sa_kernel, grid=(1,),
        in_specs=[pl.BlockSpec((bs, c), lambda i: (row0 // bs, 0)), full(cs), full(h0)] + [full(a) for a in params],
        out_specs=[pl.BlockSpec((bs, c), lambda i: (0, 0)), pl.BlockSpec((bs, c), lambda i: (0, 0)),
                   pl.BlockSpec(cs.shape, lambda i: (0, 0, 0))],
        out_shape=[jax.ShapeDtypeStruct((bs, c), BF16), jax.ShapeDtypeStruct((bs, c), F32),
                   jax.ShapeDtypeStruct(cs.shape, F32)],
        compiler_params=_cparams(("arbitrary",)), name=name)(z, cs, h0, *params)


def _ret_log_g(h):
    return math.log1p(-(2.0 ** (-5.0 - h)))


def _rope(x, cos2, sin2):
    return x * cos2 + pltpu.roll(x, x.shape[1] // 2, 1) * sin2


def _pb_kernel(q_ref, k_ref, v_ref, g_ref, cos_ref, sin_ref, gn_ref, y_ref, so_ref, s_ref, *, pad, heads):
    c = pl.program_id(1)
    last = pl.num_programs(1) - 1
    L = q_ref.shape[0]
    dk = q_ref.shape[1] // heads
    dv = v_ref.shape[1] // heads
    row = lax.broadcasted_iota(jnp.int32, (L, 1), 0)
    valid = jnp.logical_or(c > 0, row >= pad)

    @pl.when(c == 0)
    def _():
        s_ref[...] = jnp.zeros_like(s_ref)

    t = row.astype(F32)
    rel = t - lax.broadcasted_iota(jnp.int32, (1, L), 1).astype(F32)
    cos2, sin2 = cos_ref[...], sin_ref[...]
    for h in range(heads):
        lg = _ret_log_g(h)
        qh = jnp.where(valid, _rope(q_ref[:, h * dk:(h + 1) * dk], cos2, sin2), 0.0)
        kh = jnp.where(valid, _rope(k_ref[:, h * dk:(h + 1) * dk], cos2, sin2), 0.0) * (dk ** -0.5)
        vh = jnp.where(valid, v_ref[:, h * dv:(h + 1) * dv], 0.0)
        decay = jnp.where(rel >= 0, jnp.exp(jnp.maximum(rel, 0.0) * lg), 0.0)
        qb, vb = qh.astype(BF16), vh.astype(BF16)
        scores = _dot_nt(qb, kh.astype(BF16)) * decay
        s_old = s_ref[h]
        o = _dot(scores.astype(BF16), vb) + _dot(qb, s_old.astype(BF16)) * jnp.exp((t + 1.0) * lg)
        k_dec = kh * jnp.exp((L - 1.0 - t) * lg)
        s_ref[h] = math.exp(L * lg) * s_old + _dot_tn(k_dec.astype(BF16), vb)
        sl = slice(h * dv, (h + 1) * dv)
        y_ref[:, sl] = (_head_norm(o) * gn_ref[:, sl] * _silu(g_ref[:, sl])).astype(y_ref.dtype)

    @pl.when(c == last)
    def _():
        so_ref[0] = s_ref[...]


def _pb_call(z, nseq, nch, pad, heads, dk, dv, cos2, sin2, ret_norm, *, name):
    rows = nseq * nch * CHUNK
    hk, hv = heads * dk, heads * dv
    assert hv % hk == 0
    q0 = hv // hk
    return pl.pallas_call(
        functools.partial(_pb_kernel, pad=pad, heads=heads), grid=(nseq, nch),
        in_specs=[pl.BlockSpec((CHUNK, hk), lambda s, k: (s * nch + k, q0)),
                  pl.BlockSpec((CHUNK, hk), lambda s, k: (s * nch + k, q0 + 1)),
                  pl.BlockSpec((CHUNK, hv), lambda s, k: (s * nch + k, 2)),
                  pl.BlockSpec((CHUNK, hv), lambda s, k: (s * nch + k, 3)),
                  pl.BlockSpec((CHUNK, dk), lambda s, k: (k, 0)),
                  pl.BlockSpec((CHUNK, dk), lambda s, k: (k, 0)),
                  pl.BlockSpec((1, hv), lambda s, k: (0, 0))],
        out_specs=[pl.BlockSpec((CHUNK, hv), lambda s, k: (s * nch + k, 0)),
                   pl.BlockSpec((1, heads, dk, dv), lambda s, k: (s, 0, 0, 0))],
        out_shape=[jax.ShapeDtypeStruct((rows, hv), BF16),
                   jax.ShapeDtypeStruct((nseq, heads, dk, dv), F32)],
        scratch_shapes=[pltpu.VMEM((heads, dk, dv), F32)],
        compiler_params=_cparams(("arbitrary", "arbitrary")), name=name)(z, z, z, z, cos2, sin2, ret_norm)


def _first_pass_only(body, state_out_ref):
    @pl.when(pl.program_id(0) == 0)
    def _():
        body()

    @pl.when(pl.program_id(0) > 0)
    def _():
        state_out_ref[...] = jnp.zeros_like(state_out_ref)


def _pass_maps(l, prev, nblk, depth):
    if prev is not None:
        return 1, (lambda g, i: i), (lambda g, i: l)
    assert l == 0
    return depth, (lambda g, i: jnp.where(g == 0, i, nblk - 1)), (lambda g, i: g)


def _sb_kernel(*refs, heads):
    _first_pass_only(functools.partial(_sb_body, *refs, heads=heads), refs[-1])


def _sb_body(q_ref, k_ref, v_ref, g_ref, s_ref, cos_ref, sin_ref, gn_ref, y_ref, so_ref, *, heads):
    nb = q_ref.shape[0]
    dk = q_ref.shape[1] // heads
    dv = v_ref.shape[1] // heads
    row = lax.broadcasted_iota(jnp.int32, (nb, 1), 0)
    cos2, sin2 = cos_ref[...], sin_ref[...]
    for h in range(heads):
        g = math.exp(_ret_log_g(h))
        qh = _rope(q_ref[:, h * dk:(h + 1) * dk], cos2, sin2)
        kh = _rope(k_ref[:, h * dk:(h + 1) * dk], cos2, sin2) * (dk ** -0.5)
        vh = v_ref[:, h * dv:(h + 1) * dv]
        qb, vb = qh.astype(BF16), vh.astype(BF16)
        cross = jnp.zeros((nb, dv), F32)
        for i in range(nb):
            s_old = s_ref[i, h]
            cross = jnp.where(row == i, _dot(qb, s_old.astype(BF16)), cross)
            so_ref[i, h] = g * s_old + _dot_tn(jnp.where(row == i, kh, 0.0), vh)
        qk = jnp.sum(qb.astype(F32) * kh.astype(BF16).astype(F32), axis=-1, keepdims=True)
        o = qk * vb.astype(F32) + g * cross
        sl = slice(h * dv, (h + 1) * dv)
        y_ref[:, sl] = (_head_norm(o) * gn_ref[:, sl] * _silu(g_ref[:, sl])).astype(y_ref.dtype)


def _sb_call(z, row0, bs, heads, dk, dv, state, l, prev, cos2, sin2, ret_norm, *, name):
    hk, hv = heads * dk, heads * dv
    nb = SUBLANES
    assert bs % nb == 0 and row0 % nb == 0
    r0 = row0 // nb
    q0 = hv // hk
    passes, blk, lay = _pass_maps(l, prev, bs // nb, state.shape[0])
    return pl.pallas_call(
        _skip_ref(functools.partial(_sb_kernel, heads=heads), 8 if prev is not None else None),
        grid=(passes, bs // nb),
        in_specs=[pl.BlockSpec((nb, hk), lambda g, i: (r0 + blk(g, i), q0)),
                  pl.BlockSpec((nb, hk), lambda g, i: (r0 + blk(g, i), q0 + 1)),
                  pl.BlockSpec((nb, hv), lambda g, i: (r0 + blk(g, i), 2)),
                  pl.BlockSpec((nb, hv), lambda g, i: (r0 + blk(g, i), 3)),
                  pl.BlockSpec((None, nb, heads, dk, dv), lambda g, i: (l, blk(g, i), 0, 0, 0)),
                  pl.BlockSpec((1, dk), lambda g, i: (0, 0)),
                  pl.BlockSpec((1, dk), lambda g, i: (0, 0)),
                  pl.BlockSpec((1, hv), lambda g, i: (0, 0))] + _prev_specs(prev),
        out_specs=[pl.BlockSpec((nb, hv), lambda g, i: (blk(g, i), 0)),
                   pl.BlockSpec((None, nb, heads, dk, dv), lambda g, i: (lay(g, i), i, 0, 0, 0))],
        out_shape=[jax.ShapeDtypeStruct((bs, hv), BF16), jax.ShapeDtypeStruct(state.shape, F32)],
        input_output_aliases={8: 1} if prev is not None else {},
        compiler_params=_cparams(("arbitrary", "arbitrary")), name=name)(
            z, z, z, z, state, cos2, sin2, ret_norm, *([] if prev is None else [prev]))


def _mlstm_qkv_gates(x, xc, wq_ref, wk_ref, wv_ref, wif_ref, bif_ref, dh):
    q = _blockdiag(xc, wq_ref)
    k = _blockdiag(xc, wk_ref, scale=dh ** -0.5)
    v = _blockdiag(x, wv_ref)
    c = x.shape[1]
    gates = (_dot(q.astype(BF16), wif_ref[0:c, :]) + _dot(k.astype(BF16), wif_ref[c:2 * c, :])
             + _dot(v.astype(BF16), wif_ref[2 * c:3 * c, :]) + bif_ref[...])
    return q, k, v, gates


def _pc_kernel(x_ref, o_ref, cw_ref, cb_ref, wq_ref, wk_ref, wv_ref, wif_ref, bif_ref, gn_ref, sk_ref,
               y_ref, co_ref, no_ref, mo_ref, cv_ref, prev_ref, c_ref, n_ref, m_ref, *, pad, heads):
    c = pl.program_id(1)
    last = pl.num_programs(1) - 1
    L = x_ref.shape[0]
    dh = x_ref.shape[1] // heads
    row = lax.broadcasted_iota(jnp.int32, (L, 1), 0)
    valid = jnp.logical_or(c > 0, row >= pad)

    @pl.when(c == 0)
    def _():
        prev_ref[...] = jnp.zeros_like(prev_ref)
        c_ref[...] = jnp.zeros_like(c_ref)
        n_ref[...] = jnp.zeros_like(n_ref)
        m_ref[...] = jnp.zeros_like(m_ref)

    x = jnp.where(valid, x_ref[...], 0.0)
    xc = _silu(_chunk_conv(x, prev_ref[...], cw_ref, cb_ref, row))
    prev_ref[...] = x[L - SUBLANES:]
    q, k, v, gates = _mlstm_qkv_gates(x, xc, wq_ref, wk_ref, wv_ref, wif_ref, bif_ref, dh)
    lane = lax.broadcasted_iota(jnp.int32, gates.shape, 1)
    is_i = lane < heads
    ig_c = jnp.where(jnp.logical_and(valid, is_i), gates, NEG)
    lf_c = jnp.where(valid, _log_sigmoid(gates), 0.0)
    ig_r = ig_c.T
    lf_r = lf_c.T
    ti = lax.broadcasted_iota(jnp.int32, (L, L), 0)
    si = lax.broadcasted_iota(jnp.int32, (L, L), 1)
    causal = ti >= si
    ones_lower = jnp.where(causal, 1.0, 0.0).astype(BF16)
    ones_upper = jnp.where(si >= ti, 1.0, 0.0).astype(BF16)
    b_c = sum(_dot(ones_lower, part) for part in _split3(lf_c))
    b_r = sum(_dot(part, ones_upper) for part in _split3(lf_r))
    m_all = m_ref[...]
    m_lane = lax.broadcasted_iota(jnp.int32, m_all.shape, 1)
    m_new = jnp.zeros_like(m_all)
    for h in range(heads):
        sl = slice(h * dh, (h + 1) * dh)
        bc = b_c[:, heads + h:heads + h + 1]
        br = b_r[heads + h:heads + h + 1, :]
        igr = ig_r[h:h + 1, :]
        igc = ig_c[:, h:h + 1]
        m_prev = m_all[0:1, h:h + 1]
        log_d = jnp.where(causal, bc - br + igr, NEG)
        log_inter = bc + m_prev
        m_t = jnp.maximum(log_inter, jnp.max(log_d, axis=-1, keepdims=True))
        d_m = jnp.exp(log_d - m_t)
        w_inter = jnp.exp(log_inter - m_t)
        qh, kh, vh = q[:, sl], k[:, sl], v[:, sl]
        qb, kb, vb = qh.astype(BF16), kh.astype(BF16), vh.astype(BF16)
        scores = _dot_nt(qb, kb) * d_m
        c_old = c_ref[h]
        n_old = n_ref[h:h + 1, :]
        num = _dot(scores.astype(BF16), vb) + w_inter * _dot(qb, c_old.astype(BF16))
        den = jnp.sum(scores, axis=-1, keepdims=True) + w_inter * jnp.sum(qh * n_old, axis=-1, keepdims=True)
        hh = num / jnp.maximum(jnp.abs(den), jnp.exp(-m_t))
        m_end = m_t[L - 1:L]
        b_last = bc[L - 1:L]
        w_c = jnp.exp(b_last - bc + igc - m_end)
        decay_c = jnp.exp(b_last + m_prev - m_end)
        kw = kh * w_c
        c_ref[h] = decay_c * c_old + _dot_tn(kw.astype(BF16), vb)
        n_ref[h:h + 1, :] = decay_c * n_old + jnp.sum(kw, axis=0, keepdims=True)
        m_new = jnp.where(m_lane == h, m_end, m_new)
        hm = _sigmoid(o_ref[:, sl]) * hh
        y_ref[:, sl] = (_head_norm(hm) * gn_ref[:, sl] + sk_ref[:, sl] * xc[:, sl]).astype(y_ref.dtype)
    m_ref[...] = m_new

    @pl.when(c == last)
    def _():
        co_ref[0] = c_ref[...]
        no_ref[0] = n_ref[...]
        mo_ref[0] = m_ref[...]
        cv_ref[0] = x[L - (CONV_W - 1):]


def _pc_call(z, nseq, nch, pad, heads, xcol, p, *, name):
    rows = nseq * nch * CHUNK
    c = p['m_conv_w'].shape[1]
    dh = c // heads
    full = lambda a: pl.BlockSpec(a.shape, lambda s, k: (0,) * a.ndim)
    params = [p['m_conv_w'], p['m_conv_b'], p['m_wq'], p['m_wk'], p['m_wv'], p['m_w_if'], p['m_b_if'],
              p['m_norm'], p['m_skip']]
    return pl.pallas_call(
        functools.partial(_pc_kernel, pad=pad, heads=heads), grid=(nseq, nch),
        in_specs=[pl.BlockSpec((CHUNK, c), lambda s, k: (s * nch + k, xcol)),
                  pl.BlockSpec((CHUNK, c), lambda s, k: (s * nch + k, xcol + 1))] + [full(a) for a in params],
        out_specs=[pl.BlockSpec((CHUNK, c), lambda s, k: (s * nch + k, 0)),
                   pl.BlockSpec((1, heads, dh, dh), lambda s, k: (s, 0, 0, 0)),
                   pl.BlockSpec((1, SUBLANES, dh), lambda s, k: (s, 0, 0)),
                   pl.BlockSpec((1, SUBLANES, LANES), lambda s, k: (s, 0, 0)),
                   pl.BlockSpec((1, CONV_W - 1, c), lambda s, k: (s, 0, 0))],
        out_shape=[jax.ShapeDtypeStruct((rows, c), BF16),
                   jax.ShapeDtypeStruct((nseq, heads, dh, dh), F32),
                   jax.ShapeDtypeStruct((nseq, SUBLANES, dh), F32),
                   jax.ShapeDtypeStruct((nseq, SUBLANES, LANES), F32),
                   jax.ShapeDtypeStruct((nseq, CONV_W - 1, c), F32)],
        scratch_shapes=[pltpu.VMEM((SUBLANES, c), F32), pltpu.VMEM((heads, dh, dh), F32),
                        pltpu.VMEM((SUBLANES, dh), F32), pltpu.VMEM((SUBLANES, LANES), F32)],
        compiler_params=_cparams(("arbitrary", "arbitrary")), name=name)(z, z, *params)


def _sc_kernel(*refs, heads):
    _first_pass_only(functools.partial(_sc_body, *refs, heads=heads), refs[-4])


def _sc_body(x_ref, o_ref, cs_ref, c_ref, n_ref, m_ref, cw_ref, cb_ref, wq_ref, wk_ref, wv_ref, wif_ref, bif_ref,
             gn_ref, sk_ref, y_ref, co_ref, no_ref, mo_ref, cv_ref, *, heads):
    nb = x_ref.shape[0]
    dh = x_ref.shape[1] // heads
    row = lax.broadcasted_iota(jnp.int32, (nb, 1), 0)
    x = x_ref[...]
    xc = _silu(_step_conv(x, cs_ref, cw_ref, cb_ref))
    q, k, v, gates = _mlstm_qkv_gates(x, xc, wq_ref, wk_ref, wv_ref, wif_ref, bif_ref, dh)
    lf = _log_sigmoid(gates)
    m_old = m_ref[...]
    lane = lax.broadcasted_iota(jnp.int32, m_old.shape, 1)
    m_new = jnp.zeros_like(m_old)
    for h in range(heads):
        sl = slice(h * dh, (h + 1) * dh)
        ig = gates[:, h:h + 1]
        log_inter = lf[:, heads + h:heads + h + 1] + m_old[:, h:h + 1]
        m_t = jnp.maximum(log_inter, ig)
        d_m = jnp.exp(ig - m_t)
        w_inter = jnp.exp(log_inter - m_t)
        qh, kh, vh = q[:, sl], k[:, sl], v[:, sl]
        qb, kb, vb = qh.astype(BF16), kh.astype(BF16), vh.astype(BF16)
        sc = jnp.sum(qb.astype(F32) * kb.astype(F32), axis=-1, keepdims=True) * d_m
        n_old = n_ref[:, sl]
        kw = kh * d_m
        qc = jnp.zeros((nb, dh), F32)
        for i in range(nb):
            c_old = c_ref[i, h]
            qc = jnp.where(row == i, _dot(qb, c_old.astype(BF16)), qc)
            co_ref[i, h] = (w_inter[i:i + 1] * c_old
                            + _dot_tn(jnp.where(row == i, kw, 0.0), vh))
        num = sc * vb.astype(F32) + w_inter * qc
        den = sc + w_inter * jnp.sum(qh * n_old, axis=-1, keepdims=True)
        hh = num / jnp.maximum(jnp.abs(den), jnp.exp(-m_t))
        no_ref[:, sl] = w_inter * n_old + kw
        m_new = jnp.where(lane == h, m_t, m_new)
        hm = _sigmoid(o_ref[:, sl]) * hh
        y_ref[:, sl] = (_head_norm(hm) * gn_ref[:, sl] + sk_ref[:, sl] * xc[:, sl]).astype(y_ref.dtype)
    mo_ref[...] = m_new
    _step_conv_state(x, cs_ref, cv_ref)


def _sc_call(z, row0, bs, heads, xcol, cs, c_state, l, prev, n_state, m_state, p, *, name):
    c = p['m_conv_w'].shape[1]
    dh = c // heads
    nb = SUBLANES
    assert bs % nb == 0 and row0 % nb == 0
    r0 = row0 // nb
    full = lambda a: pl.BlockSpec(a.shape, lambda g, i: (0,) * a.ndim)
    params = [p['m_conv_w'], p['m_conv_b'], p['m_wq'], p['m_wk'], p['m_wv'], p['m_w_if'], p['m_b_if'],
              p['m_norm'], p['m_skip']]
    passes, blk, lay = _pass_maps(l, prev, bs // nb, c_state.shape[0])
    cs_spec = pl.BlockSpec((CONV_W - 1, nb, c), lambda g, i: (0, blk(g, i), 0))
    row_spec = lambda w: pl.BlockSpec((nb, w), lambda g, i: (blk(g, i), 0))
    return pl.pallas_call(
        _skip_ref(functools.partial(_sc_kernel, heads=heads), 6 + len(params) if prev is not None else None),
        grid=(passes, bs // nb),
        in_specs=[pl.BlockSpec((nb, c), lambda g, i: (r0 + blk(g, i), xcol)),
                  pl.BlockSpec((nb, c), lambda g, i: (r0 + blk(g, i), xcol + 1)),
                  cs_spec,
                  pl.BlockSpec((None, nb, heads, dh, dh), lambda g, i: (l, blk(g, i), 0, 0, 0)),
                  row_spec(c), row_spec(LANES)] + [full(a) for a in params] + _prev_specs(prev),
        out_specs=[row_spec(c),
                   pl.BlockSpec((None, nb, heads, dh, dh), lambda g, i: (lay(g, i), i, 0, 0, 0)),
                   row_spec(c), row_spec(LANES), cs_spec],
        out_shape=[jax.ShapeDtypeStruct((bs, c), BF16), jax.ShapeDtypeStruct(c_state.shape, F32),
                   jax.ShapeDtypeStruct((bs, c), F32), jax.ShapeDtypeStruct((bs, LANES), F32),
                   jax.ShapeDtypeStruct(cs.shape, F32)],
        input_output_aliases={6 + len(params): 1} if prev is not None else {},
        compiler_params=_cparams(("arbitrary", "arbitrary")), name=name)(
            z, z, cs, c_state, n_state, m_state, *params, *([] if prev is None else [prev]))


def _ffn_kernel(sbe_ref, sbs_ref, sbn_ref, nv_ref, xs_hbm, wg_ref, wu_ref, wd_ref, ys_hbm,
                xf_ref, xb_ref, y_ref, wgb_ref, wub_ref, wdb_ref, zero_ref, isem, osem):
    del sbe_ref
    j, f = pl.program_id(0), pl.program_id(1)
    nblk = sbn_ref[j]
    blk0 = sbs_ref[j]
    max_blk = xf_ref.shape[0] // MOE_ROWS

    @pl.when(jnp.logical_and(j == 0, f == 0))
    def _():
        xf_ref[...] = jnp.zeros_like(xf_ref)
        zero_ref[...] = jnp.zeros_like(zero_ref)

    def hbm_rows(ref, p):
        return ref.at[pl.ds(pl.multiple_of((blk0 + p) * MOE_ROWS, MOE_ROWS), MOE_ROWS), :]

    def vmem_rows(ref, p):
        return ref.at[pl.ds(p * MOE_ROWS, MOE_ROWS), :]

    def copy_blocks(make, n):
        for p in range(max_blk):
            @pl.when(p < n)
            def _():
                make(p).start()
        for p in range(max_blk):
            @pl.when(p < n)
            def _():
                make(p).wait()

    @pl.when(nblk > 0)
    def _():
        @pl.when(f == 0)
        def _():
            copy_blocks(lambda p: pltpu.make_async_copy(hbm_rows(xs_hbm, p), vmem_rows(xf_ref, p), isem), nblk)
            xb_ref[...] = xf_ref[...].astype(BF16)

        wgb_ref[...] = wg_ref[...].astype(BF16)
        wub_ref[...] = wu_ref[...].astype(BF16)
        wdb_ref[...] = wd_ref[...].astype(BF16)

        def body(b, carry):
            rows = pl.ds(pl.multiple_of(b * MOE_ROWS, MOE_ROWS), MOE_ROWS)
            x = xb_ref[rows, :]
            hmid = (_silu(_dot(x, wgb_ref[...])) * _dot(x, wub_ref[...])).astype(BF16)
            contrib = _dot(hmid, wdb_ref[...])

            @pl.when(f == 0)
            def _():
                y_ref[rows, :] = contrib

            @pl.when(f > 0)
            def _():
                y_ref[rows, :] = y_ref[rows, :] + contrib
            return carry

        lax.fori_loop(0, nblk, body, 0)

        @pl.when(f == pl.num_programs(1) - 1)
        def _():
            copy_blocks(lambda p: pltpu.make_async_copy(vmem_rows(y_ref, p), hbm_rows(ys_hbm, p), osem), nblk)

    @pl.when(jnp.logical_and(j == pl.num_programs(0) - 1, f == pl.num_programs(1) - 1))
    def _():
        _zero_blocks(zero_ref, ys_hbm, nv_ref[0], ys_hbm.shape[0] // MOE_ROWS, osem)


def _ffn_call(xs, tables, wg, wu, wd, l, *, name):
    sb_e, sb_blk0, sb_nblk, n_valid = tables
    ns, d = xs.shape
    nsb = sb_e.shape[0]
    nf = wg.shape[3] // FFN_CHUNK

    def wmap(transposed):
        def index(j, f, sbe, sbs, sbn, nv):
            fi = jnp.where(j < nv[1], f, nf - 1)
            return (l, sbe[j], fi, 0) if transposed else (l, sbe[j], 0, fi)
        return index

    grid_spec = pltpu.PrefetchScalarGridSpec(
        num_scalar_prefetch=4, grid=(nsb, nf),
        in_specs=[pl.BlockSpec(memory_space=pl.ANY),
                  pl.BlockSpec((None, None, d, FFN_CHUNK), wmap(False)),
                  pl.BlockSpec((None, None, d, FFN_CHUNK), wmap(False)),
                  pl.BlockSpec((None, None, FFN_CHUNK, d), wmap(True))],
        out_specs=pl.BlockSpec(memory_space=pl.ANY),
        scratch_shapes=[pltpu.VMEM((MOE_SB_ROWS, d), F32), pltpu.VMEM((MOE_SB_ROWS, d), BF16),
                        pltpu.VMEM((MOE_SB_ROWS, d), F32),
                        pltpu.VMEM((d, FFN_CHUNK), BF16), pltpu.VMEM((d, FFN_CHUNK), BF16),
                        pltpu.VMEM((FFN_CHUNK, d), BF16), pltpu.VMEM((MOE_ROWS, d), F32),
                        pltpu.SemaphoreType.DMA(()), pltpu.SemaphoreType.DMA(())])
    return pl.pallas_call(
        _ffn_kernel, grid_spec=grid_spec, out_shape=jax.ShapeDtypeStruct((ns, d), F32),
        compiler_params=_cparams(("arbitrary", "arbitrary")), name=name)(
            sb_e, sb_blk0, sb_nblk, n_valid, xs, wg, wu, wd)


def _take(table, idx):
    return jnp.sum(jnp.where(idx[:, None] == jnp.arange(table.shape[0])[None, :], table[None, :], 0), axis=1)


def _route_tables(ids, tile_cnt, r, tm):
    eidx = jnp.arange(N_EXPERTS, dtype=jnp.int32)
    cnt_t = tile_cnt[:, 0, :N_EXPERTS].astype(jnp.int32)
    counts = jnp.sum(cnt_t, axis=0)
    tile_off = jnp.cumsum(cnt_t, axis=0) - cnt_t
    padded = (counts + MOE_ROWS - 1) // MOE_ROWS * MOE_ROWS
    pad_end = jnp.cumsum(padded)
    pad_start = pad_end - padded
    base = jnp.repeat(pad_start[None, :] + tile_off, tm, axis=0)
    e = ids[:, :TOP_K]
    rank = ids[:, TOP_K:2 * TOP_K]
    dest = rank + jnp.sum(jnp.where(e[:, :, None] == eidx, base[:, None, :], 0), axis=-1)
    n_slots = (r * TOP_K // MOE_ROWS + N_EXPERTS) * MOE_ROWS
    nsb_e = (padded + MOE_SB_ROWS - 1) // MOE_SB_ROWS
    sb_end = jnp.cumsum(nsb_e)
    n_sb = sb_end[-1]
    nsb = n_slots // MOE_SB_ROWS + N_EXPERTS
    j = jnp.arange(nsb, dtype=jnp.int32)
    valid = j < n_sb
    e_j = jnp.minimum(jnp.sum(sb_end[None, :] <= j[:, None], axis=1), N_EXPERTS - 1)
    t = j - _take(sb_end - nsb_e, e_j)
    sb_blk0 = jnp.where(valid, (_take(pad_start, e_j) + t * MOE_SB_ROWS) // MOE_ROWS, 0)
    sb_nblk = jnp.where(valid, jnp.clip((_take(padded, e_j) - t * MOE_SB_ROWS) // MOE_ROWS,
                                        0, MOE_SB_ROWS // MOE_ROWS), 0)
    e_last = jnp.sum(jnp.where(j == n_sb - 1, e_j, 0))
    sb_e = jnp.where(valid, e_j, e_last)
    n_valid = jnp.stack([pad_end[-1] // MOE_ROWS, n_sb])
    i32 = lambda a: a.astype(jnp.int32)
    return i32(dest), i32(pad_end), (i32(sb_e), i32(sb_blk0), i32(sb_nblk), i32(n_valid)), n_slots


def _expand_blockdiag(w):
    nb, bs, _ = w.shape
    if bs == LANES:
        return w.astype(BF16)
    per = LANES // bs
    eye = jnp.eye(per, dtype=w.dtype)
    wt = w.reshape(nb // per, per, bs, bs)
    out = jnp.einsum('tpbc,pq->tpbqc', wt, eye)
    return out.reshape(nb // per, LANES, LANES).astype(BF16)


def _rope_tables(pos, half):
    freq = ROPE_BASE ** (-jnp.arange(half, dtype=F32) / half)
    ang = pos[:, None] * freq[None, :]
    cos, sin = jnp.cos(ang), jnp.sin(ang)
    return jnp.concatenate([cos, cos], axis=-1), jnp.concatenate([-sin, sin], axis=-1)


def kernel(x_prompt, x_sample, state_rglru_h, state_rglru_conv, state_ret, state_mlstm_C, state_mlstm_n,
           state_mlstm_m, state_mlstm_conv, meta_tokens, norm_mix, norm_ffn, norm_final, w_in, rg_conv_w,
           rg_conv_b, rg_wa, rg_ba, rg_wx, rg_bx, rg_lambda, ret_norm, m_conv_w, m_conv_b, m_wq, m_wk, m_wv,
           m_w_if, m_b_if, m_norm, m_skip, w_branch, w_out, moe_w_group, moe_b_group, moe_w_expert,
           moe_b_expert, moe_w_gate, moe_w_up, moe_w_down):
    bp, seq, d = x_prompt.shape
    bs = x_sample.shape[0]
    n_meta = meta_tokens.shape[0]
    depth = w_in.shape[0]
    d_rnn = state_rglru_h.shape[2]
    _, _, r_heads, r_dk, r_dv = state_ret.shape
    m_heads, m_dh = state_mlstm_C.shape[2], state_mlstm_C.shape[3]
    d_m = m_heads * m_dh
    assert seq % CHUNK == 0 and n_meta <= CHUNK and x_sample.shape[1] == 1
    pad = CHUNK - n_meta
    nch = 1 + seq // CHUNK
    rp = bp * nch * CHUNK
    r = rp + bs
    assert d_rnn == r_heads * r_dv == d_m
    xm_col = (d_rnn + 2 * r_heads * r_dk + 2 * r_heads * r_dv) // d_m
    gate_col0 = d_rnn + 2 * r_heads * r_dk + 2 * r_heads * r_dv + 2 * d_m

    meta = jnp.broadcast_to(meta_tokens[None], (bp, n_meta, d))
    xp = jnp.concatenate([jnp.zeros((bp, pad, d), F32), meta, x_prompt], axis=1).reshape(rp, d)
    x = jnp.concatenate([xp, x_sample.reshape(bs, d)], axis=0)

    pos_p = jnp.arange(nch * CHUNK, dtype=F32) - pad
    cos_p, sin_p = _rope_tables(pos_p, r_dk // 2)
    pos_s = jnp.full((1,), float(PAST_LEN), F32)
    cos_s, sin_s = _rope_tables(pos_s, r_dk // 2)

    tm = _pick_tile(r, 384, BF16_SUBLANES)
    assert rp % CHUNK == 0 and bs % BF16_SUBLANES == 0
    prompt_states, sample_states = [], []
    combine = None
    ret_s = c_s = None
    for l in range(depth):
        p = {'rg_conv_w': rg_conv_w[l], 'rg_conv_b': rg_conv_b[l].reshape(1, -1),
             'rg_wa': _expand_blockdiag(rg_wa[l]), 'rg_ba': rg_ba[l].reshape(1, -1),
             'rg_wx': _expand_blockdiag(rg_wx[l]), 'rg_bx': rg_bx[l].reshape(1, -1),
             'rg_lambda': rg_lambda[l].reshape(1, -1),
             'm_conv_w': m_conv_w[l], 'm_conv_b': m_conv_b[l].reshape(1, -1),
             'm_wq': _expand_blockdiag(m_wq[l]), 'm_wk': _expand_blockdiag(m_wk[l]),
             'm_wv': _expand_blockdiag(m_wv[l]),
             'm_w_if': jnp.pad(m_w_if[l], ((0, 0), (0, LANES - 2 * m_heads))).astype(BF16),
             'm_b_if': jnp.pad(m_b_if[l], (0, LANES - 2 * m_heads)).reshape(1, LANES),
             'm_norm': m_norm[l].reshape(1, -1), 'm_skip': m_skip[l].reshape(1, -1)}
        gn_ret = ret_norm[l].reshape(1, -1)

        x, h = _norm_call(x, norm_mix[l], tm=tm, grid=(r // tm,), row_tile=lambda i: i, want_x=True,
                          h_shape=(r, d), h_dtype=BF16, h_spec=pl.BlockSpec((tm, d), lambda i: (i, 0)),
                          combine=combine, name=f'norm_mix{l}')
        z = _mm_call(h, w_in, l, name=f'in_proj{l}')

        ya_p, rgh_p, rgc_p = _pa_call(z, bp, nch, pad, p, name=f'rglru_p{l}')
        yb_p, ret_p = _pb_call(z, bp, nch, pad, r_heads, r_dk, r_dv, cos_p, sin_p, gn_ret, name=f'ret_p{l}')
        yc_p, c_p, n_p, m_p, mc_p = _pc_call(z, bp, nch, pad, m_heads, xm_col, p, name=f'mlstm_p{l}')

        rg_cs = jnp.transpose(state_rglru_conv[l], (1, 0, 2))
        ya_s, rgh_s, rgc_s = _sa_call(z, rp, bs, rg_cs, state_rglru_h[l], p, name=f'rglru_s{l}')
        yb_s, ret_s = _sb_call(z, rp, bs, r_heads, r_dk, r_dv, state_ret, l, ret_s, cos_s, sin_s, gn_ret,
                               name=f'ret_s{l}')
        m_cs = jnp.transpose(state_mlstm_conv[l], (1, 0, 2))
        m_in = jnp.pad(state_mlstm_m[l], ((0, 0), (0, LANES - m_heads)))
        yc_s, c_s, n_s, m_s, mc_s = _sc_call(z, rp, bs, m_heads, xm_col, m_cs, state_mlstm_C, l, c_s,
                                             state_mlstm_n[l].reshape(bs, d_m), m_in, p, name=f'mlstm_s{l}')

        prompt_states.append((rgh_p[:, 0], rgc_p, ret_p, c_p, n_p[:, :m_heads], m_p[:, 0, :m_heads], mc_p))
        sample_states.append((rgh_s, jnp.transpose(rgc_s, (1, 0, 2)), None, None,
                              n_s.reshape(bs, m_heads, m_dh), m_s[:, :m_heads], jnp.transpose(mc_s, (1, 0, 2))))

        ya = jnp.concatenate([ya_p, ya_s], axis=0)
        yb = jnp.concatenate([yb_p, yb_s], axis=0)
        yc = jnp.concatenate([yc_p, yc_s], axis=0)
        merged = _merge_call(ya, yb, yc, z, w_branch, l, gate_col0, name=f'merge{l}')
        x = _mm_call(merged, w_out, l, res=x, name=f'out_proj{l}')

        wr = jnp.pad(jnp.concatenate([moe_w_group[l], moe_w_expert[l]], axis=1),
                     ((0, 0), (0, LANES - N_GROUPS - N_EXPERTS)))
        br = jnp.pad(jnp.concatenate([moe_b_group[l], moe_b_expert[l]]), (0, LANES - N_GROUPS - N_EXPERTS))
        h2, ids, wts, tile_cnt = _norm_call(
            x, norm_ffn[l], tm=tm, grid=(r // tm,), row_tile=lambda i: i, want_x=False, h_shape=(r, d),
            h_dtype=F32, h_spec=pl.BlockSpec((tm, d), lambda i: (i, 0)), router=(wr, br.reshape(1, LANES)),
            name=f'norm_router{l}')
        dest, pad_end, sb_tables, n_slots = _route_tables(ids, tile_cnt, r, tm)
        xs = _dispatch_call(h2, dest, pad_end, n_slots, tm=tm, name=f'moe_dispatch{l}')
        ys = _ffn_call(xs, sb_tables, moe_w_gate, moe_w_up, moe_w_down, l, name=f'moe_ffn{l}')
        combine = (dest, wts, ys)

    (y_prompt,) = _norm_call(
        x, norm_final, tm=CHUNK, grid=(bp, seq // CHUNK), row_tile=lambda b, c: b * nch + 1 + c, want_x=False,
        h_shape=(bp, seq, d), h_dtype=F32, h_spec=pl.BlockSpec((None, CHUNK, d), lambda b, c: (b, c, 0)),
        combine=combine, name='norm_final_p')
    ts = _pick_tile(bs, CHUNK, SUBLANES)
    assert rp % ts == 0
    (y_sample,) = _norm_call(
        x, norm_final, tm=ts, grid=(bs // ts,), row_tile=lambda i: rp // ts + i, want_x=False,
        h_shape=(bs, d), h_dtype=F32, h_spec=pl.BlockSpec((ts, d), lambda i: (i, 0)),
        combine=combine, name='norm_final_s')
    pn = [jnp.stack([s[i] for s in prompt_states], axis=0) for i in range(7)]
    sn = [None if sample_states[0][i] is None else jnp.stack([s[i] for s in sample_states], axis=0)
          for i in range(7)]
    sn[2], sn[3] = ret_s, c_s
    return (y_prompt, y_sample.reshape(bs, 1, d), *pn, *sn)
```

```python
import functools
import math

import jax
import jax.numpy as jnp
import numpy as np
from jax import lax
from jax.experimental import pallas as pl
from jax.experimental.pallas import tpu as pltpu

F32 = jnp.float32
BF16 = jnp.bfloat16

LANES = 128
SUBLANES = 8
BF16_SUBLANES = 16
VMEM_LIMIT_BYTES = 56 * 1024 * 1024

CHUNK = 128
CONV_W = 4
EPS = 1e-6
RG_C = 8.0
ROPE_BASE = 10000.0
PAST_LEN = 16384
N_GROUPS = 4
EXP_PER_GROUP = 8
N_EXPERTS = N_GROUPS * EXP_PER_GROUP
TOP_K = 2
MOE_ROWS = 128
MOE_SB_ROWS = 1024
FFN_GU_CHUNK = 1024
FFN_DN_CHUNK = 512
NEG = -1e30


def _cparams(sem, vmem=VMEM_LIMIT_BYTES):
    return pltpu.CompilerParams(dimension_semantics=sem, vmem_limit_bytes=vmem)


def _pick_tile(n, cap, mult):
    best = None
    for t in range(mult, min(n, cap) + 1, mult):
        if n % t == 0:
            best = t
    assert best is not None, (n, cap, mult)
    return best


def _skip_ref(fn, idx):
    if idx is None:
        return fn

    def wrapped(*refs):
        return fn(*refs[:idx], *refs[idx + 1:])
    return wrapped


def _prev_specs(prev):
    return [] if prev is None else [pl.BlockSpec(memory_space=pl.ANY)]


def _dot(a, b):
    return jnp.dot(a, b, preferred_element_type=F32)


def _dot_nt(a, b):
    return lax.dot_general(a, b, (((1,), (1,)), ((), ())), preferred_element_type=F32)


def _dot_tn(a, b):
    return lax.dot_general(a, b, (((0,), (0,)), ((), ())), preferred_element_type=F32)


def _split3(x):
    hi = x.astype(BF16)
    r1 = x - hi.astype(F32)
    mid = r1.astype(BF16)
    lo = (r1 - mid.astype(F32)).astype(BF16)
    return hi, mid, lo


def _dot_hi(a, b):
    return jnp.dot(a, b, preferred_element_type=F32, precision=lax.Precision.HIGHEST)


def _sigmoid(x):
    return 1.0 / (1.0 + jnp.exp(-x))


def _silu(x):
    return x * _sigmoid(x)


def _log_sigmoid(x):
    return jnp.minimum(x, 0.0) - jnp.log(1.0 + jnp.exp(-jnp.abs(x)))


def _softplus(x):
    return jnp.maximum(x, 0.0) + jnp.log(1.0 + jnp.exp(-jnp.abs(x)))


def _head_norm(o):
    mu = jnp.mean(o, axis=-1, keepdims=True)
    d = o - mu
    var = jnp.mean(d * d, axis=-1, keepdims=True)
    return d * lax.rsqrt(var + EPS)


def _blockdiag(x, w_ref, scale=None):
    outs = []
    for j in range(w_ref.shape[0]):
        o = _dot(x[:, j * LANES:(j + 1) * LANES].astype(BF16), w_ref[j])
        outs.append(o if scale is None else o * scale)
    return jnp.concatenate(outs, axis=-1)


ROW_DMA_UNROLL = 8


def _row_copies(n_rows, row_copy):
    def start(r, carry):
        for k in range(TOP_K):
            row_copy(r, k).start(priority=k % 2)
        return carry

    def wait(r, carry):
        for k in range(TOP_K):
            row_copy(r, k).wait()
        return carry

    lax.fori_loop(0, n_rows, start, 0, unroll=ROW_DMA_UNROLL)
    lax.fori_loop(0, n_rows, wait, 0, unroll=ROW_DMA_UNROLL)


def _norm_kernel(*refs, combine, want_x, router):
    refs = list(refs)
    if combine:
        dest_ref, wts_in_ref, yb_hbm = refs[:3]
        refs = refs[3:]
    x_ref, g_ref = refs[:2]
    refs = refs[2:]
    if router:
        wr_ref, br_ref = refs[:2]
        refs = refs[2:]
    n_out = int(want_x) + 1 + (3 if router else 0)
    outs, scratch = refs[:n_out], refs[n_out:]
    x = x_ref[...]
    tm = x.shape[0]
    if combine:
        gbuf, gsem = scratch
        _row_copies(tm, lambda r, k: pltpu.make_async_copy(
            yb_hbm.at[pl.ds(dest_ref[0, TOP_K * r + k], 1), :], gbuf.at[k, pl.ds(r, 1), :], gsem))
        for k in range(TOP_K):
            x = x + wts_in_ref[:, k:k + 1] * gbuf[k]
    k = 0
    if want_x:
        outs[k][...] = x
        k += 1
    ms = jnp.mean(x * x, axis=-1, keepdims=True)
    h = x * lax.rsqrt(ms + EPS) * g_ref[...]
    outs[k][...] = h.astype(outs[k].dtype)
    k += 1
    if router:
        ids_ref, wts_ref, cnt_ref = outs[k], outs[k + 1], outs[k + 2]
        logits = _dot_hi(h, wr_ref[...]) + br_ref[...]
        lane = lax.broadcasted_iota(jnp.int32, logits.shape, 1)
        big = jnp.int32(1 << 20)
        is_g = lane < N_GROUPS
        gl = jnp.where(is_g, logits, NEG)
        gmax = jnp.max(gl, axis=-1, keepdims=True)
        gidx = jnp.min(jnp.where(gl == gmax, lane, big), axis=-1, keepdims=True)
        gsum = jnp.sum(jnp.where(is_g, jnp.exp(gl - gmax), 0.0), axis=-1, keepdims=True)
        gprob = 1.0 / gsum
        lo = N_GROUPS + EXP_PER_GROUP * gidx
        em = jnp.where(lane >= lo, jnp.where(lane < lo + EXP_PER_GROUP, logits, NEG), NEG)
        e1v = jnp.max(em, axis=-1, keepdims=True)
        e1i = jnp.min(jnp.where(em == e1v, lane, big), axis=-1, keepdims=True)
        em2 = jnp.where(lane == e1i, NEG, em)
        e2v = jnp.max(em2, axis=-1, keepdims=True)
        e2i = jnp.min(jnp.where(em2 == e2v, lane, big), axis=-1, keepdims=True)
        t = jnp.exp(e2v - e1v)
        w1 = gprob / (1.0 + t)
        w2 = gprob * t / (1.0 + t)
        e1, e2 = e1i - N_GROUPS, e2i - N_GROUPS
        chosen = jnp.where(lane == e1, 1.0, jnp.where(lane == e2, 1.0, 0.0))
        ti = lax.broadcasted_iota(jnp.int32, (tm, tm), 0)
        si = lax.broadcasted_iota(jnp.int32, (tm, tm), 1)
        before = _dot(jnp.where(ti > si, 1.0, 0.0).astype(BF16), chosen.astype(BF16))
        rank1 = jnp.sum(jnp.where(lane == e1, before, 0.0), axis=-1, keepdims=True).astype(jnp.int32)
        rank2 = jnp.sum(jnp.where(lane == e2, before, 0.0), axis=-1, keepdims=True).astype(jnp.int32)
        ids_ref[...] = jnp.where(lane == 0, e1, jnp.where(lane == 1, e2,
                                 jnp.where(lane == 2, rank1, jnp.where(lane == 3, rank2, 0))))
        wts_ref[...] = jnp.where(lane == 0, w1, jnp.where(lane == 1, w2, 0.0))
        cnt_ref[...] = jnp.sum(chosen, axis=0, keepdims=True)


def _norm_call(x, g, *, tm, grid, row_tile, want_x, h_shape, h_dtype, h_spec, combine=None, router=None, name):
    r, d = x.shape
    x_spec = pl.BlockSpec((tm, d), lambda *gi: (row_tile(*gi), 0))
    in_specs, args, scratch = [], [], []
    if combine is not None:
        dest, wts, yb = combine
        in_specs += [pl.BlockSpec((None, 1, TOP_K * tm), lambda *gi: (row_tile(*gi), 0, 0), memory_space=pltpu.SMEM),
                     pl.BlockSpec((tm, LANES), lambda *gi: (row_tile(*gi), 0)),
                     pl.BlockSpec(memory_space=pl.ANY)]
        args += [dest.reshape(r // tm, 1, TOP_K * tm), wts, yb]
        scratch = [pltpu.VMEM((TOP_K, tm, d), F32), pltpu.SemaphoreType.DMA(())]
    in_specs += [x_spec, pl.BlockSpec((1, d), lambda *gi: (0, 0))]
    args += [x, g.reshape(1, d)]
    if router is not None:
        in_specs += [pl.BlockSpec((d, LANES), lambda *gi: (0, 0)), pl.BlockSpec((1, LANES), lambda *gi: (0, 0))]
        args += list(router)
    out_shape, out_specs = [], []
    if want_x:
        out_shape.append(jax.ShapeDtypeStruct((r, d), F32))
        out_specs.append(x_spec)
    out_shape.append(jax.ShapeDtypeStruct(h_shape, h_dtype))
    out_specs.append(h_spec)
    if router is not None:
        out_shape += [jax.ShapeDtypeStruct((r, LANES), jnp.int32), jax.ShapeDtypeStruct((r, LANES), F32),
                      jax.ShapeDtypeStruct((r // tm, 1, LANES), F32)]
        out_specs += [pl.BlockSpec((tm, LANES), lambda *gi: (row_tile(*gi), 0))] * 2
        out_specs += [pl.BlockSpec((None, 1, LANES), lambda *gi: (row_tile(*gi), 0, 0))]
    return pl.pallas_call(
        functools.partial(_norm_kernel, combine=combine is not None, want_x=want_x, router=router is not None),
        grid=grid, in_specs=in_specs, out_specs=out_specs, out_shape=out_shape, scratch_shapes=scratch,
        compiler_params=_cparams(("arbitrary",) * len(grid)), name=name)(*args)


def _embed_norm_kernel(xp_ref, meta_ref, xs_ref, g_ref, x_ref, h_ref, *, nch, n_prompt_tiles):
    i = pl.program_id(0)
    c = lax.rem(i, nch)
    tm, d = x_ref.shape

    def finish(x):
        x_ref[...] = x
        ms = jnp.mean(x * x, axis=-1, keepdims=True)
        h_ref[...] = (x * lax.rsqrt(ms + EPS) * g_ref[...]).astype(h_ref.dtype)

    @pl.when(i >= n_prompt_tiles)
    def _():
        finish(xs_ref[...])

    @pl.when(jnp.logical_and(i < n_prompt_tiles, c == 0))
    def _():
        finish(jnp.concatenate([jnp.zeros((tm - meta_ref.shape[0], d), F32), meta_ref[...]], axis=0))

    @pl.when(jnp.logical_and(i < n_prompt_tiles, c > 0))
    def _():
        finish(xp_ref[...])


def _embed_norm_call(x_prompt, meta, x_sample, g, nch, *, name):
    bp, seq, d = x_prompt.shape
    bs = x_sample.shape[0]
    npt = bp * nch
    assert bs % CHUNK == 0 and meta.shape[0] % SUBLANES == 0
    r = npt * CHUNK + bs
    row_spec = pl.BlockSpec((CHUNK, d), lambda i: (i, 0))
    return pl.pallas_call(
        functools.partial(_embed_norm_kernel, nch=nch, n_prompt_tiles=npt), grid=(r // CHUNK,),
        in_specs=[pl.BlockSpec((None, CHUNK, d),
                               lambda i: (jnp.minimum(i // nch, bp - 1), jnp.maximum(lax.rem(i, nch) - 1, 0), 0)),
                  pl.BlockSpec(meta.shape, lambda i: (0, 0)),
                  pl.BlockSpec((CHUNK, d), lambda i: (jnp.maximum(i - npt, 0), 0)),
                  pl.BlockSpec((1, d), lambda i: (0, 0))],
        out_specs=[row_spec, row_spec],
        out_shape=[jax.ShapeDtypeStruct((r, d), F32), jax.ShapeDtypeStruct((r, d), BF16)],
        compiler_params=_cparams(("arbitrary",)), name=name)(x_prompt, meta, x_sample, g.reshape(1, d))


def _zero_blocks(zero_ref, out_hbm, first, last, sem):
    def blk_copy(b):
        return pltpu.make_async_copy(zero_ref, out_hbm.at[pl.ds(pl.multiple_of(b * MOE_ROWS, MOE_ROWS), MOE_ROWS), :], sem)

    def start(b, carry):
        blk_copy(b).start()
        return carry

    def wait(b, carry):
        blk_copy(b).wait()
        return carry

    lax.fori_loop(first, last, start, 0)
    lax.fori_loop(first, last, wait, 0)


def _dispatch_kernel(pe_ref, dest_ref, h_ref, xs_out, zero_ref, sem, zsem):
    tm = h_ref.shape[0]
    n_blocks = xs_out.shape[0] // MOE_ROWS

    @pl.when(pl.program_id(0) == 0)
    def _():
        zero_ref[...] = jnp.zeros_like(zero_ref)

        def last_block(e):
            b = jnp.maximum(pe_ref[e] // MOE_ROWS - 1, 0)
            return pltpu.make_async_copy(
                zero_ref, xs_out.at[pl.ds(pl.multiple_of(b * MOE_ROWS, MOE_ROWS), MOE_ROWS), :], zsem)

        for e in range(N_EXPERTS):
            last_block(e).start()
        for e in range(N_EXPERTS):
            last_block(e).wait()
        _zero_blocks(zero_ref, xs_out, pe_ref[N_EXPERTS - 1] // MOE_ROWS, n_blocks, zsem)

    _row_copies(tm, lambda r, k: pltpu.make_async_copy(
        h_ref.at[pl.ds(r, 1), :], xs_out.at[pl.ds(dest_ref[0, TOP_K * r + k], 1), :], sem))


def _dispatch_call(h2, dest, pad_end, n_slots, *, tm, name):
    r, d = h2.shape
    grid_spec = pltpu.PrefetchScalarGridSpec(
        num_scalar_prefetch=1, grid=(r // tm,),
        in_specs=[pl.BlockSpec((None, 1, TOP_K * tm), lambda i, pe: (i, 0, 0), memory_space=pltpu.SMEM),
                  pl.BlockSpec((tm, d), lambda i, pe: (i, 0))],
        out_specs=pl.BlockSpec(memory_space=pl.ANY),
        scratch_shapes=[pltpu.VMEM((MOE_ROWS, d), h2.dtype), pltpu.SemaphoreType.DMA(()),
                        pltpu.SemaphoreType.DMA(())])
    return pl.pallas_call(
        _dispatch_kernel, grid_spec=grid_spec, out_shape=jax.ShapeDtypeStruct((n_slots, d), h2.dtype),
        compiler_params=_cparams(("arbitrary",)), name=name)(pad_end, dest.reshape(r // tm, 1, TOP_K * tm), h2)


def _mm_kernel(*refs, has_res):
    if has_res:
        x_ref, w_ref, r_ref, o_ref, wb_ref = refs
    else:
        x_ref, w_ref, o_ref, wb_ref = refs

    @pl.when(pl.program_id(1) == 0)
    def _():
        wb_ref[...] = w_ref[...].astype(BF16)

    acc = _dot(x_ref[...], wb_ref[...])
    if has_res:
        acc = acc + r_ref[...]
    o_ref[...] = acc.astype(o_ref.dtype)


def _mm_call(x, w, l, res=None, *, out_dtype=F32, name):
    r, k = x.shape
    n = w.shape[2]
    tm = _pick_tile(r, 1152, BF16_SUBLANES)
    tn = _pick_tile(n, 1024, LANES)
    in_specs = [pl.BlockSpec((tm, k), lambda j, i: (i, 0)), pl.BlockSpec((None, k, tn), lambda j, i: (l, 0, j))]
    args = [x, w]
    if res is not None:
        in_specs.append(pl.BlockSpec((tm, tn), lambda j, i: (i, j)))
        args.append(res)
    return pl.pallas_call(
        functools.partial(_mm_kernel, has_res=res is not None),
        grid=(n // tn, r // tm), in_specs=in_specs,
        out_specs=pl.BlockSpec((tm, tn), lambda j, i: (i, j)),
        out_shape=jax.ShapeDtypeStruct((r, n), out_dtype),
        scratch_shapes=[pltpu.VMEM((k, tn), BF16)],
        compiler_params=_cparams(("arbitrary", "arbitrary")), name=name)(*args)


def _merge_kernel(*refs, n_branch, n_prompt_tiles):
    yp_refs, ys_refs = refs[:n_branch], refs[n_branch:2 * n_branch]
    g_refs = refs[2 * n_branch:3 * n_branch]
    w_ref, o_ref, wb_ref = refs[3 * n_branch:]
    i = pl.program_id(1)

    @pl.when(i == 0)
    def _():
        wb_ref[...] = w_ref[...].astype(BF16)

    def run(y_refs):
        acc = _sigmoid(g_refs[0][...]) * _dot(y_refs[0][...], wb_ref[0])
        for n in range(1, n_branch):
            acc = acc + _sigmoid(g_refs[n][...]) * _dot(y_refs[n][...], wb_ref[n])
        o_ref[...] = acc.astype(o_ref.dtype)

    @pl.when(i < n_prompt_tiles)
    def _():
        run(yp_refs)

    @pl.when(i >= n_prompt_tiles)
    def _():
        run(ys_refs)


def _merge_call(y_prompt, y_sample, z, w_branch, l, gate_col0, *, name):
    rp, db = y_prompt[0].shape
    bs = y_sample[0].shape[0]
    _, nbr, _, d = w_branch.shape
    tn = _pick_tile(d, 1024, LANES)
    tm = _pick_tile(math.gcd(rp, bs), 384, BF16_SUBLANES)
    npt = rp // tm
    assert gate_col0 % tn == 0 and d % tn == 0
    g0 = gate_col0 // tn
    per = d // tn
    yp_spec = pl.BlockSpec((tm, db), lambda j, i: (jnp.minimum(i, npt - 1), 0))
    ys_spec = pl.BlockSpec((tm, db), lambda j, i: (jnp.maximum(i - npt, 0), 0))
    g_specs = [pl.BlockSpec((tm, tn), lambda j, i, n=n: (i, g0 + per * n + j)) for n in range(nbr)]
    return pl.pallas_call(
        functools.partial(_merge_kernel, n_branch=nbr, n_prompt_tiles=npt), grid=(d // tn, (rp + bs) // tm),
        in_specs=[yp_spec] * nbr + [ys_spec] * nbr + g_specs
        + [pl.BlockSpec((None, nbr, db, tn), lambda j, i: (l, 0, 0, j))],
        out_specs=pl.BlockSpec((tm, tn), lambda j, i: (i, j)),
        out_shape=jax.ShapeDtypeStruct((rp + bs, d), BF16),
        scratch_shapes=[pltpu.VMEM((nbr, db, tn), BF16)],
        compiler_params=_cparams(("arbitrary", "arbitrary")), name=name)(
            *y_prompt, *y_sample, *([z] * nbr), w_branch)


def _shift_rows(x, prev8, k, row):
    rolled = pltpu.roll(x, k, 0)
    head = jnp.where(row[:SUBLANES] < k, pltpu.roll(prev8, k, 0), rolled[:SUBLANES])
    return jnp.concatenate([head, rolled[SUBLANES:]], axis=0)


def _chunk_conv(x, prev8, w_ref, b_ref, row):
    y = b_ref[...] + w_ref[CONV_W - 1:CONV_W, :] * x
    for k in range(1, CONV_W):
        y = y + w_ref[CONV_W - 1 - k:CONV_W - k, :] * _shift_rows(x, prev8, k, row)
    return y


def _rglru_coeffs(xc, wa_ref, ba_ref, wx_ref, bx_ref, lam_ref):
    r = _sigmoid(_blockdiag(xc, wa_ref) + ba_ref[...])
    i = _sigmoid(_blockdiag(xc, wx_ref) + bx_ref[...])
    log_a = -RG_C * r * _softplus(-lam_ref[...])
    a = jnp.exp(log_a)
    mult = jnp.sqrt(1.0 - a * a)
    return a, mult * i * xc


def _pa_kernel(x_ref, cw_ref, cb_ref, wa_ref, ba_ref, wx_ref, bx_ref, lam_ref,
               y_ref, hl_ref, cv_ref, prev_ref, h_ref, *, pad):
    c = pl.program_id(1)
    last = pl.num_programs(1) - 1
    L = x_ref.shape[0]
    row = lax.broadcasted_iota(jnp.int32, (L, 1), 0)
    valid = jnp.logical_or(c > 0, row >= pad)

    @pl.when(c == 0)
    def _():
        prev_ref[...] = jnp.zeros_like(prev_ref)
        h_ref[...] = jnp.zeros_like(h_ref)

    x = jnp.where(valid, x_ref[...], 0.0)
    xc = _chunk_conv(x, prev_ref[...], cw_ref, cb_ref, row)
    prev_ref[...] = x[L - SUBLANES:]
    a, b = _rglru_coeffs(xc, wa_ref, ba_ref, wx_ref, bx_ref, lam_ref)
    a = jnp.where(valid, a, 1.0)
    b = jnp.where(valid, b, 0.0)
    hs = []
    for j in range(x.shape[1] // LANES):
        sl = slice(j * LANES, (j + 1) * LANES)
        aj, bj = a[:, sl], b[:, sl]
        d = 1
        while d < L:
            keep = row >= d
            a_s = jnp.where(keep, pltpu.roll(aj, d, 0), 1.0)
            b_s = jnp.where(keep, pltpu.roll(bj, d, 0), 0.0)
            bj = aj * b_s + bj
            aj = aj * a_s
            d *= 2
        hs.append(aj * h_ref[0:1, sl] + bj)
    h = jnp.concatenate(hs, axis=-1)
    h_ref[0:1, :] = h[L - 1:L]
    y_ref[...] = h.astype(y_ref.dtype)

    @pl.when(c == last)
    def _():
        hl_ref[0] = h[L - 1:L]
        cv_ref[0] = x[L - (CONV_W - 1):]


def _pa_call(z, nseq, nch, pad, p, *, name):
    rows = nseq * nch * CHUNK
    c = p['rg_conv_w'].shape[1]
    full = lambda a: pl.BlockSpec(a.shape, lambda s, k: (0,) * a.ndim)
    params = [p['rg_conv_w'], p['rg_conv_b'], p['rg_wa'], p['rg_ba'], p['rg_wx'], p['rg_bx'], p['rg_lambda']]
    return pl.pallas_call(
        functools.partial(_pa_kernel, pad=pad), grid=(nseq, nch),
        in_specs=[pl.BlockSpec((CHUNK, c), lambda s, k: (s * nch + k, 0))] + [full(a) for a in params],
        out_specs=[pl.BlockSpec((CHUNK, c), lambda s, k: (s * nch + k, 0)),
                   pl.BlockSpec((1, 1, c), lambda s, k: (s, 0, 0)),
                   pl.BlockSpec((1, CONV_W - 1, c), lambda s, k: (s, 0, 0))],
        out_shape=[jax.ShapeDtypeStruct((rows, c), BF16),
                   jax.ShapeDtypeStruct((nseq, 1, c), F32),
                   jax.ShapeDtypeStruct((nseq, CONV_W - 1, c), F32)],
        scratch_shapes=[pltpu.VMEM((SUBLANES, c), F32), pltpu.VMEM((SUBLANES, c), F32)],
        compiler_params=_cparams(("arbitrary", "arbitrary")), name=name)(z, *params)


def _step_conv(x, cs_ref, w_ref, b_ref):
    y = b_ref[...] + w_ref[CONV_W - 1:CONV_W, :] * x
    for k in range(CONV_W - 1):
        y = y + w_ref[k:k + 1, :] * cs_ref[k]
    return y


def _step_conv_state(x, cs_ref, out_ref):
    for k in range(CONV_W - 2):
        out_ref[k] = cs_ref[k + 1]
    out_ref[CONV_W - 2] = x


def _sa_kernel(x_ref, cs_ref, h0_ref, cw_ref, cb_ref, wa_ref, ba_ref, wx_ref, bx_ref, lam_ref,
               y_ref, hn_ref, cv_ref):
    x = x_ref[...]
    xc = _step_conv(x, cs_ref, cw_ref, cb_ref)
    a, b = _rglru_coeffs(xc, wa_ref, ba_ref, wx_ref, bx_ref, lam_ref)
    h = a * h0_ref[...] + b
    hn_ref[...] = h
    y_ref[...] = h.astype(y_ref.dtype)
    _step_conv_state(x, cs_ref, cv_ref)


def _sa_call(z, row0, bs, cs, h0, p, *, name):
    c = h0.shape[1]
    assert row0 % bs == 0
    full = lambda a: pl.BlockSpec(a.shape, lambda i: (0,) * a.ndim)
    params = [p['rg_conv_w'], p['rg_conv_b'], p['rg_wa'], p['rg_ba'], p['rg_wx'], p['rg_bx'], p['rg_lambda']]
    return pl.pallas_call(
        _sa_kernel, grid=(1,),
        in_specs=[pl.BlockSpec((bs, c), lambda i: (row0 // bs, 0)), full(cs), full(h0)] + [full(a) for a in params],
        out_specs=[pl.BlockSpec((bs, c), lambda i: (0, 0)), pl.BlockSpec((bs, c), lambda i: (0, 0)),
                   pl.BlockSpec(cs.shape, lambda i: (0, 0, 0))],
        out_shape=[jax.ShapeDtypeStruct((bs, c), BF16), jax.ShapeDtypeStruct((bs, c), F32),
                   jax.ShapeDtypeStruct(cs.shape, F32)],
        compiler_params=_cparams(("arbitrary",)), name=name)(z, cs, h0, *params)


def _ret_log_g(h):
    return math.log1p(-(2.0 ** (-5.0 - h)))


def _rope(x, cos2, sin2):
    return x * cos2 + pltpu.roll(x, x.shape[1] // 2, 1) * sin2


def _pb_kernel(q_ref, k_ref, v_ref, g_ref, cos_ref, sin_ref, gn_ref, y_ref, so_ref, s_ref, *, pad, heads):
    c = pl.program_id(1)
    last = pl.num_programs(1) - 1
    L = q_ref.shape[0]
    dk = q_ref.shape[1] // heads
    dv = v_ref.shape[1] // heads
    row = lax.broadcasted_iota(jnp.int32, (L, 1), 0)
    valid = jnp.logical_or(c > 0, row >= pad)

    @pl.when(c == 0)
    def _():
        s_ref[...] = jnp.zeros_like(s_ref)

    t = row.astype(F32)
    rel = t - lax.broadcasted_iota(jnp.int32, (1, L), 1).astype(F32)
    cos2, sin2 = cos_ref[...], sin_ref[...]
    for h in range(heads):
        lg = _ret_log_g(h)
        qh = jnp.where(valid, _rope(q_ref[:, h * dk:(h + 1) * dk], cos2, sin2), 0.0)
        kh = jnp.where(valid, _rope(k_ref[:, h * dk:(h + 1) * dk], cos2, sin2), 0.0) * (dk ** -0.5)
        vh = jnp.where(valid, v_ref[:, h * dv:(h + 1) * dv], 0.0)
        decay = jnp.where(rel >= 0, jnp.exp(jnp.maximum(rel, 0.0) * lg), 0.0)
        qb, vb = qh.astype(BF16), vh.astype(BF16)
        scores = _dot_nt(qb, kh.astype(BF16)) * decay
        s_old = s_ref[h]
        o = _dot(scores.astype(BF16), vb) + _dot(qb, s_old.astype(BF16)) * jnp.exp((t + 1.0) * lg)
        k_dec = kh * jnp.exp((L - 1.0 - t) * lg)
        s_ref[h] = math.exp(L * lg) * s_old + _dot_tn(k_dec.astype(BF16), vb)
        sl = slice(h * dv, (h + 1) * dv)
        y_ref[:, sl] = (_head_norm(o) * gn_ref[:, sl] * _silu(g_ref[:, sl])).astype(y_ref.dtype)

    @pl.when(c == last)
    def _():
        so_ref[0] = s_ref[...]


def _pb_call(z, nseq, nch, pad, heads, dk, dv, cos2, sin2, ret_norm, *, name):
    rows = nseq * nch * CHUNK
    hk, hv = heads * dk, heads * dv
    assert hv % hk == 0
    q0 = hv // hk
    return pl.pallas_call(
        functools.partial(_pb_kernel, pad=pad, heads=heads), grid=(nseq, nch),
        in_specs=[pl.BlockSpec((CHUNK, hk), lambda s, k: (s * nch + k, q0)),
                  pl.BlockSpec((CHUNK, hk), lambda s, k: (s * nch + k, q0 + 1)),
                  pl.BlockSpec((CHUNK, hv), lambda s, k: (s * nch + k, 2)),
                  pl.BlockSpec((CHUNK, hv), lambda s, k: (s * nch + k, 3)),
                  pl.BlockSpec((CHUNK, dk), lambda s, k: (k, 0)),
                  pl.BlockSpec((CHUNK, dk), lambda s, k: (k, 0)),
                  pl.BlockSpec((1, hv), lambda s, k: (0, 0))],
        out_specs=[pl.BlockSpec((CHUNK, hv), lambda s, k: (s * nch + k, 0)),
                   pl.BlockSpec((1, heads, dk, dv), lambda s, k: (s, 0, 0, 0))],
        out_shape=[jax.ShapeDtypeStruct((rows, hv), BF16),
                   jax.ShapeDtypeStruct((nseq, heads, dk, dv), F32)],
        scratch_shapes=[pltpu.VMEM((heads, dk, dv), F32)],
        compiler_params=_cparams(("arbitrary", "arbitrary")), name=name)(z, z, z, z, cos2, sin2, ret_norm)


def _first_pass_only(body, state_out_ref):
    @pl.when(pl.program_id(0) == 0)
    def _():
        body()

    @pl.when(pl.program_id(0) > 0)
    def _():
        state_out_ref[...] = jnp.zeros_like(state_out_ref)


def _pass_maps(l, prev, nblk, depth):
    if prev is not None:
        return 1, (lambda g, i: i), (lambda g, i: l)
    assert l == 0
    return depth, (lambda g, i: jnp.where(g == 0, i, nblk - 1)), (lambda g, i: g)


def _sb_kernel(*refs, heads):
    _first_pass_only(functools.partial(_sb_body, *refs, heads=heads), refs[-1])


def _sb_body(q_ref, k_ref, v_ref, g_ref, s_ref, cos_ref, sin_ref, gn_ref, y_ref, so_ref, *, heads):
    nb = q_ref.shape[0]
    dk = q_ref.shape[1] // heads
    dv = v_ref.shape[1] // heads
    row = lax.broadcasted_iota(jnp.int32, (nb, 1), 0)
    cos2, sin2 = cos_ref[...], sin_ref[...]
    for h in range(heads):
        g = math.exp(_ret_log_g(h))
        qh = _rope(q_ref[:, h * dk:(h + 1) * dk], cos2, sin2)
        kh = _rope(k_ref[:, h * dk:(h + 1) * dk], cos2, sin2) * (dk ** -0.5)
        vh = v_ref[:, h * dv:(h + 1) * dv]
        qb, vb = qh.astype(BF16), vh.astype(BF16)
        cross = jnp.zeros((nb, dv), F32)
        for i in range(nb):
            s_old = s_ref[i, h]
            cross = jnp.where(row == i, _dot(qb, s_old.astype(BF16)), cross)
            so_ref[i, h] = g * s_old + _dot_tn(jnp.where(row == i, kh, 0.0), vh)
        qk = jnp.sum(qb.astype(F32) * kh.astype(BF16).astype(F32), axis=-1, keepdims=True)
        o = qk * vb.astype(F32) + g * cross
        sl = slice(h * dv, (h + 1) * dv)
        y_ref[:, sl] = (_head_norm(o) * gn_ref[:, sl] * _silu(g_ref[:, sl])).astype(y_ref.dtype)


def _sb_call(z, row0, bs, heads, dk, dv, state, l, prev, cos2, sin2, ret_norm, *, name):
    hk, hv = heads * dk, heads * dv
    nb = SUBLANES
    assert bs % nb == 0 and row0 % nb == 0
    r0 = row0 // nb
    q0 = hv // hk
    passes, blk, lay = _pass_maps(l, prev, bs // nb, state.shape[0])
    return pl.pallas_call(
        _skip_ref(functools.partial(_sb_kernel, heads=heads), 8 if prev is not None else None),
        grid=(passes, bs // nb),
        in_specs=[pl.BlockSpec((nb, hk), lambda g, i: (r0 + blk(g, i), q0)),
                  pl.BlockSpec((nb, hk), lambda g, i: (r0 + blk(g, i), q0 + 1)),
                  pl.BlockSpec((nb, hv), lambda g, i: (r0 + blk(g, i), 2)),
                  pl.BlockSpec((nb, hv), lambda g, i: (r0 + blk(g, i), 3)),
                  pl.BlockSpec((None, nb, heads, dk, dv), lambda g, i: (l, blk(g, i), 0, 0, 0)),
                  pl.BlockSpec((1, dk), lambda g, i: (0, 0)),
                  pl.BlockSpec((1, dk), lambda g, i: (0, 0)),
                  pl.BlockSpec((1, hv), lambda g, i: (0, 0))] + _prev_specs(prev),
        out_specs=[pl.BlockSpec((nb, hv), lambda g, i: (blk(g, i), 0)),
                   pl.BlockSpec((None, nb, heads, dk, dv), lambda g, i: (lay(g, i), i, 0, 0, 0))],
        out_shape=[jax.ShapeDtypeStruct((bs, hv), BF16), jax.ShapeDtypeStruct(state.shape, F32)],
        input_output_aliases={8: 1} if prev is not None else {},
        compiler_params=_cparams(("arbitrary", "arbitrary")), name=name)(
            z, z, z, z, state, cos2, sin2, ret_norm, *([] if prev is None else [prev]))


def _mlstm_qkv_gates(x, xc, wq_ref, wk_ref, wv_ref, wif_ref, bif_ref, dh):
    q = _blockdiag(xc, wq_ref)
    k = _blockdiag(xc, wk_ref, scale=dh ** -0.5)
    v = _blockdiag(x, wv_ref)
    c = x.shape[1]
    gates = (_dot(q.astype(BF16), wif_ref[0:c, :]) + _dot(k.astype(BF16), wif_ref[c:2 * c, :])
             + _dot(v.astype(BF16), wif_ref[2 * c:3 * c, :]) + bif_ref[...])
    return q, k, v, gates


def _pc_kernel(x_ref, o_ref, cw_ref, cb_ref, wq_ref, wk_ref, wv_ref, wif_ref, bif_ref, gn_ref, sk_ref,
               y_ref, co_ref, no_ref, mo_ref, cv_ref, prev_ref, c_ref, n_ref, m_ref, *, pad, heads):
    c = pl.program_id(1)
    last = pl.num_programs(1) - 1
    L = x_ref.shape[0]
    dh = x_ref.shape[1] // heads
    row = lax.broadcasted_iota(jnp.int32, (L, 1), 0)
    valid = jnp.logical_or(c > 0, row >= pad)

    @pl.when(c == 0)
    def _():
        prev_ref[...] = jnp.zeros_like(prev_ref)
        c_ref[...] = jnp.zeros_like(c_ref)
        n_ref[...] = jnp.zeros_like(n_ref)
        m_ref[...] = jnp.zeros_like(m_ref)

    x = jnp.where(valid, x_ref[...], 0.0)
    xc = _silu(_chunk_conv(x, prev_ref[...], cw_ref, cb_ref, row))
    prev_ref[...] = x[L - SUBLANES:]
    q, k, v, gates = _mlstm_qkv_gates(x, xc, wq_ref, wk_ref, wv_ref, wif_ref, bif_ref, dh)
    lane = lax.broadcasted_iota(jnp.int32, gates.shape, 1)
    is_i = lane < heads
    ig_c = jnp.where(jnp.logical_and(valid, is_i), gates, NEG)
    lf_c = jnp.where(valid, _log_sigmoid(gates), 0.0)
    ig_r = ig_c.T
    lf_r = lf_c.T
    ti = lax.broadcasted_iota(jnp.int32, (L, L), 0)
    si = lax.broadcasted_iota(jnp.int32, (L, L), 1)
    causal = ti >= si
    ones_lower = jnp.where(causal, 1.0, 0.0).astype(BF16)
    ones_upper = jnp.where(si >= ti, 1.0, 0.0).astype(BF16)
    b_c = sum(_dot(ones_lower, part) for part in _split3(lf_c))
    b_r = sum(_dot(part, ones_upper) for part in _split3(lf_r))
    m_all = m_ref[...]
    m_lane = lax.broadcasted_iota(jnp.int32, m_all.shape, 1)
    m_new = jnp.zeros_like(m_all)
    for h in range(heads):
        sl = slice(h * dh, (h + 1) * dh)
        bc = b_c[:, heads + h:heads + h + 1]
        br = b_r[heads + h:heads + h + 1, :]
        igr = ig_r[h:h + 1, :]
        igc = ig_c[:, h:h + 1]
        m_prev = m_all[0:1, h:h + 1]
        log_d = jnp.where(causal, bc - br + igr, NEG)
        log_inter = bc + m_prev
        m_t = jnp.maximum(log_inter, jnp.max(log_d, axis=-1, keepdims=True))
        d_m = jnp.exp(log_d - m_t)
        w_inter = jnp.exp(log_inter - m_t)
        qh, kh, vh = q[:, sl], k[:, sl], v[:, sl]
        qb, kb, vb = qh.astype(BF16), kh.astype(BF16), vh.astype(BF16)
        scores = _dot_nt(qb, kb) * d_m
        c_old = c_ref[h]
        n_old = n_ref[h:h + 1, :]
        num = _dot(scores.astype(BF16), vb) + w_inter * _dot(qb, c_old.astype(BF16))
        den = jnp.sum(scores, axis=-1, keepdims=True) + w_inter * jnp.sum(qh * n_old, axis=-1, keepdims=True)
        hh = num / jnp.maximum(jnp.abs(den), jnp.exp(-m_t))
        m_end = m_t[L - 1:L]
        b_last = bc[L - 1:L]
        w_c = jnp.exp(b_last - bc + igc - m_end)
        decay_c = jnp.exp(b_last + m_prev - m_end)
        kw = kh * w_c
        c_ref[h] = decay_c * c_old + _dot_tn(kw.astype(BF16), vb)
        n_ref[h:h + 1, :] = decay_c * n_old + jnp.sum(kw, axis=0, keepdims=True)
        m_new = jnp.where(m_lane == h, m_end, m_new)
        hm = _sigmoid(o_ref[:, sl]) * hh
        y_ref[:, sl] = (_head_norm(hm) * gn_ref[:, sl] + sk_ref[:, sl] * xc[:, sl]).astype(y_ref.dtype)
    m_ref[...] = m_new

    @pl.when(c == last)
    def _():
        co_ref[0] = c_ref[...]
        no_ref[0] = n_ref[...]
        mo_ref[0] = m_ref[...]
        cv_ref[0] = x[L - (CONV_W - 1):]


def _pc_call(z, nseq, nch, pad, heads, xcol, p, *, name):
    rows = nseq * nch * CHUNK
    c = p['m_conv_w'].shape[1]
    dh = c // heads
    full = lambda a: pl.BlockSpec(a.shape, lambda s, k: (0,) * a.ndim)
    params = [p['m_conv_w'], p['m_conv_b'], p['m_wq'], p['m_wk'], p['m_wv'], p['m_w_if'], p['m_b_if'],
              p['m_norm'], p['m_skip']]
    return pl.pallas_call(
        functools.partial(_pc_kernel, pad=pad, heads=heads), grid=(nseq, nch),
        in_specs=[pl.BlockSpec((CHUNK, c), lambda s, k: (s * nch + k, xcol)),
                  pl.BlockSpec((CHUNK, c), lambda s, k: (s * nch + k, xcol + 1))] + [full(a) for a in params],
        out_specs=[pl.BlockSpec((CHUNK, c), lambda s, k: (s * nch + k, 0)),
                   pl.BlockSpec((1, heads, dh, dh), lambda s, k: (s, 0, 0, 0)),
                   pl.BlockSpec((1, SUBLANES, dh), lambda s, k: (s, 0, 0)),
                   pl.BlockSpec((1, SUBLANES, LANES), lambda s, k: (s, 0, 0)),
                   pl.BlockSpec((1, CONV_W - 1, c), lambda s, k: (s, 0, 0))],
        out_shape=[jax.ShapeDtypeStruct((rows, c), BF16),
                   jax.ShapeDtypeStruct((nseq, heads, dh, dh), F32),
                   jax.ShapeDtypeStruct((nseq, SUBLANES, dh), F32),
                   jax.ShapeDtypeStruct((nseq, SUBLANES, LANES), F32),
                   jax.ShapeDtypeStruct((nseq, CONV_W - 1, c), F32)],
        scratch_shapes=[pltpu.VMEM((SUBLANES, c), F32), pltpu.VMEM((heads, dh, dh), F32),
                        pltpu.VMEM((SUBLANES, dh), F32), pltpu.VMEM((SUBLANES, LANES), F32)],
        compiler_params=_cparams(("arbitrary", "arbitrary")), name=name)(z, z, *params)


def _sc_kernel(*refs, heads):
    _first_pass_only(functools.partial(_sc_body, *refs, heads=heads), refs[-4])


def _sc_body(x_ref, o_ref, cs_ref, c_ref, n_ref, m_ref, cw_ref, cb_ref, wq_ref, wk_ref, wv_ref, wif_ref, bif_ref,
             gn_ref, sk_ref, y_ref, co_ref, no_ref, mo_ref, cv_ref, *, heads):
    nb = x_ref.shape[0]
    dh = x_ref.shape[1] // heads
    row = lax.broadcasted_iota(jnp.int32, (nb, 1), 0)
    x = x_ref[...]
    xc = _silu(_step_conv(x, cs_ref, cw_ref, cb_ref))
    q, k, v, gates = _mlstm_qkv_gates(x, xc, wq_ref, wk_ref, wv_ref, wif_ref, bif_ref, dh)
    lf = _log_sigmoid(gates)
    m_old = m_ref[...]
    lane = lax.broadcasted_iota(jnp.int32, m_old.shape, 1)
    m_new = jnp.zeros_like(m_old)
    for h in range(heads):
        sl = slice(h * dh, (h + 1) * dh)
        ig = gates[:, h:h + 1]
        log_inter = lf[:, heads + h:heads + h + 1] + m_old[:, h:h + 1]
        m_t = jnp.maximum(log_inter, ig)
        d_m = jnp.exp(ig - m_t)
        w_inter = jnp.exp(log_inter - m_t)
        qh, kh, vh = q[:, sl], k[:, sl], v[:, sl]
        qb, kb, vb = qh.astype(BF16), kh.astype(BF16), vh.astype(BF16)
        sc = jnp.sum(qb.astype(F32) * kb.astype(F32), axis=-1, keepdims=True) * d_m
        n_old = n_ref[:, sl]
        kw = kh * d_m
        qc = jnp.zeros((nb, dh), F32)
        for i in range(nb):
            c_old = c_ref[i, h]
            qc = jnp.where(row == i, _dot(qb, c_old.astype(BF16)), qc)
            co_ref[i, h] = (w_inter[i:i + 1] * c_old
                            + _dot_tn(jnp.where(row == i, kw, 0.0), vh))
        num = sc * vb.astype(F32) + w_inter * qc
        den = sc + w_inter * jnp.sum(qh * n_old, axis=-1, keepdims=True)
        hh = num / jnp.maximum(jnp.abs(den), jnp.exp(-m_t))
        no_ref[:, sl] = w_inter * n_old + kw
        m_new = jnp.where(lane == h, m_t, m_new)
        hm = _sigmoid(o_ref[:, sl]) * hh
        y_ref[:, sl] = (_head_norm(hm) * gn_ref[:, sl] + sk_ref[:, sl] * xc[:, sl]).astype(y_ref.dtype)
    mo_ref[...] = m_new
    _step_conv_state(x, cs_ref, cv_ref)


def _sc_call(z, row0, bs, heads, xcol, cs, c_state, l, prev, n_state, m_state, p, *, name):
    c = p['m_conv_w'].shape[1]
    dh = c // heads
    nb = SUBLANES
    assert bs % nb == 0 and row0 % nb == 0
    r0 = row0 // nb
    full = lambda a: pl.BlockSpec(a.shape, lambda g, i: (0,) * a.ndim)
    params = [p['m_conv_w'], p['m_conv_b'], p['m_wq'], p['m_wk'], p['m_wv'], p['m_w_if'], p['m_b_if'],
              p['m_norm'], p['m_skip']]
    passes, blk, lay = _pass_maps(l, prev, bs // nb, c_state.shape[0])
    cs_spec = pl.BlockSpec((CONV_W - 1, nb, c), lambda g, i: (0, blk(g, i), 0))
    row_spec = lambda w: pl.BlockSpec((nb, w), lambda g, i: (blk(g, i), 0))
    return pl.pallas_call(
        _skip_ref(functools.partial(_sc_kernel, heads=heads), 6 + len(params) if prev is not None else None),
        grid=(passes, bs // nb),
        in_specs=[pl.BlockSpec((nb, c), lambda g, i: (r0 + blk(g, i), xcol)),
                  pl.BlockSpec((nb, c), lambda g, i: (r0 + blk(g, i), xcol + 1)),
                  cs_spec,
                  pl.BlockSpec((None, nb, heads, dh, dh), lambda g, i: (l, blk(g, i), 0, 0, 0)),
                  row_spec(c), row_spec(LANES)] + [full(a) for a in params] + _prev_specs(prev),
        out_specs=[row_spec(c),
                   pl.BlockSpec((None, nb, heads, dh, dh), lambda g, i: (lay(g, i), i, 0, 0, 0)),
                   row_spec(c), row_spec(LANES), cs_spec],
        out_shape=[jax.ShapeDtypeStruct((bs, c), BF16), jax.ShapeDtypeStruct(c_state.shape, F32),
                   jax.ShapeDtypeStruct((bs, c), F32), jax.ShapeDtypeStruct((bs, LANES), F32),
                   jax.ShapeDtypeStruct(cs.shape, F32)],
        input_output_aliases={6 + len(params): 1} if prev is not None else {},
        compiler_params=_cparams(("arbitrary", "arbitrary")), name=name)(
            z, z, cs, c_state, n_state, m_state, *params, *([] if prev is None else [prev]))


def _ffn_kernel(sbe_ref, sbs_ref, sbn_ref, nv_ref, xs_hbm, wg_ref, wu_ref, wd_ref, ys_hbm,
                acc_ref, xb_ref, h_ref, wgb_ref, wub_ref, wdb_ref, zero_ref, isem, osem, *, nka, nkb):
    del sbe_ref
    j, s = pl.program_id(0), pl.program_id(1)
    nblk = sbn_ref[j]
    blk0 = sbs_ref[j]
    max_blk = acc_ref.shape[0] // MOE_ROWS
    gu, f = wgb_ref.shape
    dn = wdb_ref.shape[0]

    @pl.when(jnp.logical_and(j == 0, s == 0))
    def _():
        zero_ref[...] = jnp.zeros_like(zero_ref)

    def hbm_rows(ref, p):
        return ref.at[pl.ds(pl.multiple_of((blk0 + p) * MOE_ROWS, MOE_ROWS), MOE_ROWS), :]

    def vmem_rows(ref, p):
        return ref.at[pl.ds(p * MOE_ROWS, MOE_ROWS), :]

    def for_blocks(fn, n):
        for p in range(max_blk):
            @pl.when(p < n)
            def _():
                fn(p)

    def block_loop(body):
        def step(b, carry):
            body(pl.ds(pl.multiple_of(b * MOE_ROWS, MOE_ROWS), MOE_ROWS))
            return carry
        lax.fori_loop(0, nblk, step, 0)

    @pl.when(nblk > 0)
    def _():
        @pl.when(s == 0)
        def _():
            x_copy = lambda p: pltpu.make_async_copy(hbm_rows(xs_hbm, p), vmem_rows(acc_ref, p), isem)
            for_blocks(lambda p: x_copy(p).start(), nblk)
            for_blocks(lambda p: x_copy(p).wait(), nblk)

            def cast_block(p):
                xb_ref[p * MOE_ROWS:(p + 1) * MOE_ROWS, :] = acc_ref[p * MOE_ROWS:(p + 1) * MOE_ROWS, :].astype(BF16)
            for_blocks(cast_block, nblk)

        for c in range(nka):
            @pl.when(s == c)
            def _(c=c):
                wgb_ref[...] = wg_ref[...].astype(BF16)
                wub_ref[...] = wu_ref[...].astype(BF16)

                def body(rows):
                    x = xb_ref[rows, c * gu:(c + 1) * gu]
                    g = _dot(x, wgb_ref[...])
                    u = _dot(x, wub_ref[...])
                    if c > 0:
                        g = g + acc_ref[rows, 0:f]
                        u = u + acc_ref[rows, f:2 * f]
                    if c < nka - 1:
                        acc_ref[rows, 0:f] = g
                        acc_ref[rows, f:2 * f] = u
                    else:
                        h_ref[rows, :] = (_silu(g) * u).astype(BF16)
                block_loop(body)

        for c in range(nkb):
            @pl.when(s == nka + c)
            def _(c=c):
                wdb_ref[...] = wd_ref[...].astype(BF16)

                def body(rows):
                    y = _dot(h_ref[rows, c * dn:(c + 1) * dn], wdb_ref[...])
                    if c > 0:
                        y = y + acc_ref[rows, :]
                    acc_ref[rows, :] = y
                block_loop(body)

                if c == nkb - 1:
                    y_copy = lambda p: pltpu.make_async_copy(vmem_rows(acc_ref, p), hbm_rows(ys_hbm, p), osem)
                    for_blocks(lambda p: y_copy(p).start(), nblk)
                    for_blocks(lambda p: y_copy(p).wait(), nblk)

    @pl.when(jnp.logical_and(j == pl.num_programs(0) - 1, s == pl.num_programs(1) - 1))
    def _():
        _zero_blocks(zero_ref, ys_hbm, nv_ref[0], ys_hbm.shape[0] // MOE_ROWS, osem)


def _ffn_call(xs, tables, wg, wu, wd, l, *, name):
    sb_e, sb_blk0, sb_nblk, n_valid = tables
    ns, d = xs.shape
    f = wg.shape[3]
    nsb = sb_e.shape[0]
    assert d % FFN_GU_CHUNK == 0 and f % FFN_DN_CHUNK == 0 and 2 * f <= d
    nka, nkb = d // FFN_GU_CHUNK, f // FFN_DN_CHUNK

    def gu_map(j, s, sbe, sbs, sbn, nv):
        return (l, sbe[j], jnp.where(j < nv[1], jnp.minimum(s, nka - 1), nka - 1), 0)

    def dn_map(j, s, sbe, sbs, sbn, nv):
        prev = jnp.maximum(j - 1, 0)
        e = jnp.where(s >= nka, sbe[j], sbe[prev])
        c = jnp.where(j < nv[1], jnp.where(s >= nka, s - nka, jnp.where(j == 0, 0, nkb - 1)), nkb - 1)
        return (l, e, c, 0)

    grid_spec = pltpu.PrefetchScalarGridSpec(
        num_scalar_prefetch=4, grid=(nsb, nka + nkb),
        in_specs=[pl.BlockSpec(memory_space=pl.ANY),
                  pl.BlockSpec((None, None, FFN_GU_CHUNK, f), gu_map),
                  pl.BlockSpec((None, None, FFN_GU_CHUNK, f), gu_map),
                  pl.BlockSpec((None, None, FFN_DN_CHUNK, d), dn_map)],
        out_specs=pl.BlockSpec(memory_space=pl.ANY),
        scratch_shapes=[pltpu.VMEM((MOE_SB_ROWS, d), F32), pltpu.VMEM((MOE_SB_ROWS, d), BF16),
                        pltpu.VMEM((MOE_SB_ROWS, f), BF16),
                        pltpu.VMEM((FFN_GU_CHUNK, f), BF16), pltpu.VMEM((FFN_GU_CHUNK, f), BF16),
                        pltpu.VMEM((FFN_DN_CHUNK, d), BF16), pltpu.VMEM((MOE_ROWS, d), F32),
                        pltpu.SemaphoreType.DMA(()), pltpu.SemaphoreType.DMA(())])
    return pl.pallas_call(
        functools.partial(_ffn_kernel, nka=nka, nkb=nkb), grid_spec=grid_spec,
        out_shape=jax.ShapeDtypeStruct((ns, d), F32),
        compiler_params=_cparams(("arbitrary", "arbitrary")), name=name)(
            sb_e, sb_blk0, sb_nblk, n_valid, xs, wg, wu, wd)


def _take(table, idx):
    return jnp.sum(jnp.where(idx[:, None] == jnp.arange(table.shape[0])[None, :], table[None, :], 0), axis=1)


def _route_tables(ids, tile_cnt, r, tm):
    eidx = jnp.arange(N_EXPERTS, dtype=jnp.int32)
    cnt_t = tile_cnt[:, 0, :N_EXPERTS].astype(jnp.int32)
    counts = jnp.sum(cnt_t, axis=0)
    tile_off = jnp.cumsum(cnt_t, axis=0) - cnt_t
    padded = (counts + MOE_ROWS - 1) // MOE_ROWS * MOE_ROWS
    pad_end = jnp.cumsum(padded)
    pad_start = pad_end - padded
    base = jnp.repeat(pad_start[None, :] + tile_off, tm, axis=0)
    e = ids[:, :TOP_K]
    rank = ids[:, TOP_K:2 * TOP_K]
    dest = rank + jnp.sum(jnp.where(e[:, :, None] == eidx, base[:, None, :], 0), axis=-1)
    n_slots = (r * TOP_K // MOE_ROWS + N_EXPERTS) * MOE_ROWS
    nsb_e = (padded + MOE_SB_ROWS - 1) // MOE_SB_ROWS
    sb_end = jnp.cumsum(nsb_e)
    n_sb = sb_end[-1]
    nsb = n_slots // MOE_SB_ROWS + N_EXPERTS
    j = jnp.arange(nsb, dtype=jnp.int32)
    valid = j < n_sb
    e_j = jnp.minimum(jnp.sum(sb_end[None, :] <= j[:, None], axis=1), N_EXPERTS - 1)
    t = j - _take(sb_end - nsb_e, e_j)
    sb_blk0 = jnp.where(valid, (_take(pad_start, e_j) + t * MOE_SB_ROWS) // MOE_ROWS, 0)
    sb_nblk = jnp.where(valid, jnp.clip((_take(padded, e_j) - t * MOE_SB_ROWS) // MOE_ROWS,
                                        0, MOE_SB_ROWS // MOE_ROWS), 0)
    e_last = jnp.sum(jnp.where(j == n_sb - 1, e_j, 0))
    sb_e = jnp.where(valid, e_j, e_last)
    n_valid = jnp.stack([pad_end[-1] // MOE_ROWS, n_sb])
    i32 = lambda a: a.astype(jnp.int32)
    return i32(dest), i32(pad_end), (i32(sb_e), i32(sb_blk0), i32(sb_nblk), i32(n_valid)), n_slots


def _expand_blockdiag(w):
    nb, bs, _ = w.shape
    if bs == LANES:
        return w.astype(BF16)
    per = LANES // bs
    eye = jnp.eye(per, dtype=w.dtype)
    wt = w.reshape(nb // per, per, bs, bs)
    out = jnp.einsum('tpbc,pq->tpbqc', wt, eye)
    return out.reshape(nb // per, LANES, LANES).astype(BF16)


def _rope_tables(pos, half):
    freq = ROPE_BASE ** (-jnp.arange(half, dtype=F32) / half)
    ang = pos[:, None] * freq[None, :]
    cos, sin = jnp.cos(ang), jnp.sin(ang)
    return jnp.concatenate([cos, cos], axis=-1), jnp.concatenate([-sin, sin], axis=-1)


def kernel(x_prompt, x_sample, state_rglru_h, state_rglru_conv, state_ret, state_mlstm_C, state_mlstm_n,
           state_mlstm_m, state_mlstm_conv, meta_tokens, norm_mix, norm_ffn, norm_final, w_in, rg_conv_w,
           rg_conv_b, rg_wa, rg_ba, rg_wx, rg_bx, rg_lambda, ret_norm, m_conv_w, m_conv_b, m_wq, m_wk, m_wv,
           m_w_if, m_b_if, m_norm, m_skip, w_branch, w_out, moe_w_group, moe_b_group, moe_w_expert,
           moe_b_expert, moe_w_gate, moe_w_up, moe_w_down):
    bp, seq, d = x_prompt.shape
    bs = x_sample.shape[0]
    n_meta = meta_tokens.shape[0]
    depth = w_in.shape[0]
    d_rnn = state_rglru_h.shape[2]
    _, _, r_heads, r_dk, r_dv = state_ret.shape
    m_heads, m_dh = state_mlstm_C.shape[2], state_mlstm_C.shape[3]
    d_m = m_heads * m_dh
    assert seq % CHUNK == 0 and n_meta <= CHUNK and x_sample.shape[1] == 1
    pad = CHUNK - n_meta
    nch = 1 + seq // CHUNK
    rp = bp * nch * CHUNK
    r = rp + bs
    assert d_rnn == r_heads * r_dv == d_m
    xm_col = (d_rnn + 2 * r_heads * r_dk + 2 * r_heads * r_dv) // d_m
    gate_col0 = d_rnn + 2 * r_heads * r_dk + 2 * r_heads * r_dv + 2 * d_m

    pos_p = jnp.arange(nch * CHUNK, dtype=F32) - pad
    cos_p, sin_p = _rope_tables(pos_p, r_dk // 2)
    pos_s = jnp.full((1,), float(PAST_LEN), F32)
    cos_s, sin_s = _rope_tables(pos_s, r_dk // 2)

    tm = _pick_tile(r, 384, BF16_SUBLANES)
    assert rp % CHUNK == 0 and bs % BF16_SUBLANES == 0
    prompt_states, sample_states = [], []
    combine = None
    ret_s = c_s = None
    for l in range(depth):
        p = {'rg_conv_w': rg_conv_w[l], 'rg_conv_b': rg_conv_b[l].reshape(1, -1),
             'rg_wa': _expand_blockdiag(rg_wa[l]), 'rg_ba': rg_ba[l].reshape(1, -1),
             'rg_wx': _expand_blockdiag(rg_wx[l]), 'rg_bx': rg_bx[l].reshape(1, -1),
             'rg_lambda': rg_lambda[l].reshape(1, -1),
             'm_conv_w': m_conv_w[l], 'm_conv_b': m_conv_b[l].reshape(1, -1),
             'm_wq': _expand_blockdiag(m_wq[l]), 'm_wk': _expand_blockdiag(m_wk[l]),
             'm_wv': _expand_blockdiag(m_wv[l]),
             'm_w_if': jnp.pad(m_w_if[l], ((0, 0), (0, LANES - 2 * m_heads))).astype(BF16),
             'm_b_if': jnp.pad(m_b_if[l], (0, LANES - 2 * m_heads)).reshape(1, LANES),
             'm_norm': m_norm[l].reshape(1, -1), 'm_skip': m_skip[l].reshape(1, -1)}
        gn_ret = ret_norm[l].reshape(1, -1)

        if l == 0:
            x, h = _embed_norm_call(x_prompt, meta_tokens, x_sample.reshape(bs, d), norm_mix[l], nch,
                                    name=f'norm_mix{l}')
        else:
            x, h = _norm_call(x, norm_mix[l], tm=tm, grid=(r // tm,), row_tile=lambda i: i, want_x=True,
                              h_shape=(r, d), h_dtype=BF16, h_spec=pl.BlockSpec((tm, d), lambda i: (i, 0)),
                              combine=combine, name=f'norm_mix{l}')
        z = _mm_call(h, w_in, l, name=f'in_proj{l}')

        ya_p, rgh_p, rgc_p = _pa_call(z, bp, nch, pad, p, name=f'rglru_p{l}')
        yb_p, ret_p = _pb_call(z, bp, nch, pad, r_heads, r_dk, r_dv, cos_p, sin_p, gn_ret, name=f'ret_p{l}')
        yc_p, c_p, n_p, m_p, mc_p = _pc_call(z, bp, nch, pad, m_heads, xm_col, p, name=f'mlstm_p{l}')

        rg_cs = jnp.transpose(state_rglru_conv[l], (1, 0, 2))
        ya_s, rgh_s, rgc_s = _sa_call(z, rp, bs, rg_cs, state_rglru_h[l], p, name=f'rglru_s{l}')
        yb_s, ret_s = _sb_call(z, rp, bs, r_heads, r_dk, r_dv, state_ret, l, ret_s, cos_s, sin_s, gn_ret,
                               name=f'ret_s{l}')
        m_cs = jnp.transpose(state_mlstm_conv[l], (1, 0, 2))
        m_in = jnp.pad(state_mlstm_m[l], ((0, 0), (0, LANES - m_heads)))
        yc_s, c_s, n_s, m_s, mc_s = _sc_call(z, rp, bs, m_heads, xm_col, m_cs, state_mlstm_C, l, c_s,
                                             state_mlstm_n[l].reshape(bs, d_m), m_in, p, name=f'mlstm_s{l}')

        prompt_states.append((rgh_p[:, 0], rgc_p, ret_p, c_p, n_p[:, :m_heads], m_p[:, 0, :m_heads], mc_p))
        sample_states.append((rgh_s, jnp.transpose(rgc_s, (1, 0, 2)), None, None,
                              n_s.reshape(bs, m_heads, m_dh), m_s[:, :m_heads], jnp.transpose(mc_s, (1, 0, 2))))

        merged = _merge_call((ya_p, yb_p, yc_p), (ya_s, yb_s, yc_s), z, w_branch, l, gate_col0, name=f'merge{l}')
        x = _mm_call(merged, w_out, l, res=x, name=f'out_proj{l}')

        wr = jnp.pad(jnp.concatenate([moe_w_group[l], moe_w_expert[l]], axis=1),
                     ((0, 0), (0, LANES - N_GROUPS - N_EXPERTS)))
        br = jnp.pad(jnp.concatenate([moe_b_group[l], moe_b_expert[l]]), (0, LANES - N_GROUPS - N_EXPERTS))
        h2, ids, wts, tile_cnt = _norm_call(
            x, norm_ffn[l], tm=tm, grid=(r // tm,), row_tile=lambda i: i, want_x=False, h_shape=(r, d),
            h_dtype=F32, h_spec=pl.BlockSpec((tm, d), lambda i: (i, 0)), router=(wr, br.reshape(1, LANES)),
            name=f'norm_router{l}')
        dest, pad_end, sb_tables, n_slots = _route_tables(ids, tile_cnt, r, tm)
        xs = _dispatch_call(h2, dest, pad_end, n_slots, tm=tm, name=f'moe_dispatch{l}')
        ys = _ffn_call(xs, sb_tables, moe_w_gate, moe_w_up, moe_w_down, l, name=f'moe_ffn{l}')
        combine = (dest, wts, ys)

    (y_prompt,) = _norm_call(
        x, norm_final, tm=CHUNK, grid=(bp, seq // CHUNK), row_tile=lambda b, c: b * nch + 1 + c, want_x=False,
        h_shape=(bp, seq, d), h_dtype=F32, h_spec=pl.BlockSpec((None, CHUNK, d), lambda b, c: (b, c, 0)),
        combine=combine, name='norm_final_p')
    ts = _pick_tile(bs, CHUNK, SUBLANES)
    assert rp % ts == 0
    (y_sample,) = _norm_call(
        x, norm_final, tm=ts, grid=(bs // ts,), row_tile=lambda i: rp // ts + i, want_x=False,
        h_shape=(bs, d), h_dtype=F32, h_spec=pl.BlockSpec((ts, d), lambda i: (i, 0)),
        combine=combine, name='norm_final_s')
    pn = [jnp.stack([s[i] for s in prompt_states], axis=0) for i in range(7)]
    sn = [None if sample_states[0][i] is None else jnp.stack([s[i] for s in sample_states], axis=0)
          for i in range(7)]
    sn[2], sn[3] = ret_s, c_s
    return (y_prompt, y_sample.reshape(bs, 1, d), *pn, *sn)
```

```python
import functools
import math

import jax
import jax.numpy as jnp
import numpy as np
from jax import lax
from jax.experimental import pallas as pl
from jax.experimental.pallas import tpu as pltpu

F32 = jnp.float32
BF16 = jnp.bfloat16

LANES = 128
SUBLANES = 8
BF16_SUBLANES = 16
VMEM_LIMIT_BYTES = 56 * 1024 * 1024

CHUNK = 128
CONV_W = 4
EPS = 1e-6
RG_C = 8.0
ROPE_BASE = 10000.0
PAST_LEN = 16384
N_GROUPS = 4
EXP_PER_GROUP = 8
N_EXPERTS = N_GROUPS * EXP_PER_GROUP
TOP_K = 2
MOE_ROWS = 128
MOE_SB_ROWS = 1024
FFN_ROWS = 256
FFN_GU_CHUNK = 1024
FFN_DN_CHUNK = 512
NEG = -1e30


def _cparams(sem, vmem=VMEM_LIMIT_BYTES):
    return pltpu.CompilerParams(dimension_semantics=sem, vmem_limit_bytes=vmem)


def _pick_tile(n, cap, mult):
    best = None
    for t in range(mult, min(n, cap) + 1, mult):
        if n % t == 0:
            best = t
    assert best is not None, (n, cap, mult)
    return best


def _skip_ref(fn, idx):
    if idx is None:
        return fn

    def wrapped(*refs):
        return fn(*refs[:idx], *refs[idx + 1:])
    return wrapped


def _prev_specs(prev):
    return [] if prev is None else [pl.BlockSpec(memory_space=pl.ANY)]


def _dot(a, b):
    return jnp.dot(a, b, preferred_element_type=F32)


def _dot_nt(a, b):
    return lax.dot_general(a, b, (((1,), (1,)), ((), ())), preferred_element_type=F32)


def _dot_tn(a, b):
    return lax.dot_general(a, b, (((0,), (0,)), ((), ())), preferred_element_type=F32)


def _split3(x):
    hi = x.astype(BF16)
    r1 = x - hi.astype(F32)
    mid = r1.astype(BF16)
    lo = (r1 - mid.astype(F32)).astype(BF16)
    return hi, mid, lo


U32 = jnp.uint32


def _pack_halves(x):
    n = x.shape[1] // 2
    lo = lax.bitcast_convert_type(x[:, :n].astype(BF16).astype(F32), U32)
    hi = lax.bitcast_convert_type(x[:, n:].astype(BF16).astype(F32), U32)
    return (lo >> 16) | (hi & U32(0xFFFF0000))


def _unpack_half(p, half):
    bits = (p << 16) if half == 0 else (p & U32(0xFFFF0000))
    return lax.bitcast_convert_type(bits, F32)


def _dot_hi(a, b):
    return jnp.dot(a, b, preferred_element_type=F32, precision=lax.Precision.HIGHEST)


def _sigmoid(x):
    return 1.0 / (1.0 + jnp.exp(-x))


def _silu(x):
    return x * _sigmoid(x)


def _log_sigmoid(x):
    return jnp.minimum(x, 0.0) - jnp.log(1.0 + jnp.exp(-jnp.abs(x)))


def _softplus(x):
    return jnp.maximum(x, 0.0) + jnp.log(1.0 + jnp.exp(-jnp.abs(x)))


def _head_norm(o):
    mu = jnp.mean(o, axis=-1, keepdims=True)
    d = o - mu
    var = jnp.mean(d * d, axis=-1, keepdims=True)
    return d * lax.rsqrt(var + EPS)


def _blockdiag(x, w_ref, scale=None):
    outs = []
    for j in range(w_ref.shape[0]):
        o = _dot(x[:, j * LANES:(j + 1) * LANES].astype(BF16), w_ref[j])
        outs.append(o if scale is None else o * scale)
    return jnp.concatenate(outs, axis=-1)


ROW_DMA_UNROLL = 8


def _row_copies(n_rows, row_copy):
    def start(r, carry):
        for k in range(TOP_K):
            row_copy(r, k).start(priority=k % 2)
        return carry

    def wait(r, carry):
        for k in range(TOP_K):
            row_copy(r, k).wait()
        return carry

    lax.fori_loop(0, n_rows, start, 0, unroll=ROW_DMA_UNROLL)
    lax.fori_loop(0, n_rows, wait, 0, unroll=ROW_DMA_UNROLL)


def _norm_kernel(*refs, combine, want_x, router):
    refs = list(refs)
    if combine:
        dest_ref, wts_in_ref, yb_hbm = refs[:3]
        refs = refs[3:]
    x_ref, g_ref = refs[:2]
    refs = refs[2:]
    if router:
        wr_ref, br_ref = refs[:2]
        refs = refs[2:]
    n_out = int(want_x) + 1 + (3 if router else 0)
    outs, scratch = refs[:n_out], refs[n_out:]
    x = x_ref[...]
    tm = x.shape[0]
    if combine:
        gbuf, gsem = scratch
        _row_copies(tm, lambda r, k: pltpu.make_async_copy(
            yb_hbm.at[pl.ds(dest_ref[0, TOP_K * r + k], 1), :], gbuf.at[k, pl.ds(r, 1), :], gsem))
        for k in range(TOP_K):
            y = jnp.concatenate([_unpack_half(gbuf[k], 0), _unpack_half(gbuf[k], 1)], axis=-1)
            x = x + wts_in_ref[:, k:k + 1] * y
    k = 0
    if want_x:
        outs[k][...] = x
        k += 1
    ms = jnp.mean(x * x, axis=-1, keepdims=True)
    h = x * lax.rsqrt(ms + EPS) * g_ref[...]
    outs[k][...] = _pack_halves(h) if router else h.astype(outs[k].dtype)
    k += 1
    if router:
        ids_ref, wts_ref, cnt_ref = outs[k], outs[k + 1], outs[k + 2]
        logits = _dot_hi(h, wr_ref[...]) + br_ref[...]
        lane = lax.broadcasted_iota(jnp.int32, logits.shape, 1)
        big = jnp.int32(1 << 20)
        is_g = lane < N_GROUPS
        gl = jnp.where(is_g, logits, NEG)
        gmax = jnp.max(gl, axis=-1, keepdims=True)
        gidx = jnp.min(jnp.where(gl == gmax, lane, big), axis=-1, keepdims=True)
        gsum = jnp.sum(jnp.where(is_g, jnp.exp(gl - gmax), 0.0), axis=-1, keepdims=True)
        gprob = 1.0 / gsum
        lo = N_GROUPS + EXP_PER_GROUP * gidx
        em = jnp.where(lane >= lo, jnp.where(lane < lo + EXP_PER_GROUP, logits, NEG), NEG)
        e1v = jnp.max(em, axis=-1, keepdims=True)
        e1i = jnp.min(jnp.where(em == e1v, lane, big), axis=-1, keepdims=True)
        em2 = jnp.where(lane == e1i, NEG, em)
        e2v = jnp.max(em2, axis=-1, keepdims=True)
        e2i = jnp.min(jnp.where(em2 == e2v, lane, big), axis=-1, keepdims=True)
        t = jnp.exp(e2v - e1v)
        w1 = gprob / (1.0 + t)
        w2 = gprob * t / (1.0 + t)
        e1, e2 = e1i - N_GROUPS, e2i - N_GROUPS
        chosen = jnp.where(lane == e1, 1.0, jnp.where(lane == e2, 1.0, 0.0))
        ti = lax.broadcasted_iota(jnp.int32, (tm, tm), 0)
        si = lax.broadcasted_iota(jnp.int32, (tm, tm), 1)
        before = _dot(jnp.where(ti > si, 1.0, 0.0).astype(BF16), chosen.astype(BF16))
        rank1 = jnp.sum(jnp.where(lane == e1, before, 0.0), axis=-1, keepdims=True).astype(jnp.int32)
        rank2 = jnp.sum(jnp.where(lane == e2, before, 0.0), axis=-1, keepdims=True).astype(jnp.int32)
        ids_ref[...] = jnp.where(lane == 0, e1, jnp.where(lane == 1, e2,
                                 jnp.where(lane == 2, rank1, jnp.where(lane == 3, rank2, 0))))
        wts_ref[...] = jnp.where(lane == 0, w1, jnp.where(lane == 1, w2, 0.0))
        cnt_ref[...] = jnp.sum(chosen, axis=0, keepdims=True)


def _norm_call(x, g, *, tm, grid, row_tile, want_x, h_shape, h_dtype, h_spec, combine=None, router=None, name):
    r, d = x.shape
    x_spec = pl.BlockSpec((tm, d), lambda *gi: (row_tile(*gi), 0))
    in_specs, args, scratch = [], [], []
    if combine is not None:
        dest, wts, yb = combine
        in_specs += [pl.BlockSpec((None, 1, TOP_K * tm), lambda *gi: (row_tile(*gi), 0, 0), memory_space=pltpu.SMEM),
                     pl.BlockSpec((tm, LANES), lambda *gi: (row_tile(*gi), 0)),
                     pl.BlockSpec(memory_space=pl.ANY)]
        args += [dest.reshape(r // tm, 1, TOP_K * tm), wts, yb]
        assert yb.dtype == U32 and yb.shape[1] * 2 == d
        scratch = [pltpu.VMEM((TOP_K, tm, d // 2), U32), pltpu.SemaphoreType.DMA(())]
    in_specs += [x_spec, pl.BlockSpec((1, d), lambda *gi: (0, 0))]
    args += [x, g.reshape(1, d)]
    if router is not None:
        in_specs += [pl.BlockSpec((d, LANES), lambda *gi: (0, 0)), pl.BlockSpec((1, LANES), lambda *gi: (0, 0))]
        args += list(router)
    out_shape, out_specs = [], []
    if want_x:
        out_shape.append(jax.ShapeDtypeStruct((r, d), F32))
        out_specs.append(x_spec)
    out_shape.append(jax.ShapeDtypeStruct(h_shape, h_dtype))
    out_specs.append(h_spec)
    if router is not None:
        out_shape += [jax.ShapeDtypeStruct((r, LANES), jnp.int32), jax.ShapeDtypeStruct((r, LANES), F32),
                      jax.ShapeDtypeStruct((r // tm, 1, LANES), F32)]
        out_specs += [pl.BlockSpec((tm, LANES), lambda *gi: (row_tile(*gi), 0))] * 2
        out_specs += [pl.BlockSpec((None, 1, LANES), lambda *gi: (row_tile(*gi), 0, 0))]
    return pl.pallas_call(
        functools.partial(_norm_kernel, combine=combine is not None, want_x=want_x, router=router is not None),
        grid=grid, in_specs=in_specs, out_specs=out_specs, out_shape=out_shape, scratch_shapes=scratch,
        compiler_params=_cparams(("arbitrary",) * len(grid)), name=name)(*args)


def _embed_norm_kernel(xp_ref, meta_ref, xs_ref, g_ref, x_ref, h_ref, *, nch, n_prompt_tiles):
    i = pl.program_id(0)
    c = lax.rem(i, nch)
    tm, d = x_ref.shape

    def finish(x):
        x_ref[...] = x
        ms = jnp.mean(x * x, axis=-1, keepdims=True)
        h_ref[...] = (x * lax.rsqrt(ms + EPS) * g_ref[...]).astype(h_ref.dtype)

    @pl.when(i >= n_prompt_tiles)
    def _():
        finish(xs_ref[...])

    @pl.when(jnp.logical_and(i < n_prompt_tiles, c == 0))
    def _():
        finish(jnp.concatenate([jnp.zeros((tm - meta_ref.shape[0], d), F32), meta_ref[...]], axis=0))

    @pl.when(jnp.logical_and(i < n_prompt_tiles, c > 0))
    def _():
        finish(xp_ref[...])


def _embed_norm_call(x_prompt, meta, x_sample, g, nch, *, name):
    bp, seq, d = x_prompt.shape
    bs = x_sample.shape[0]
    npt = bp * nch
    assert bs % CHUNK == 0 and meta.shape[0] % SUBLANES == 0
    r = npt * CHUNK + bs
    row_spec = pl.BlockSpec((CHUNK, d), lambda i: (i, 0))
    return pl.pallas_call(
        functools.partial(_embed_norm_kernel, nch=nch, n_prompt_tiles=npt), grid=(r // CHUNK,),
        in_specs=[pl.BlockSpec((None, CHUNK, d),
                               lambda i: (jnp.minimum(i // nch, bp - 1), jnp.maximum(lax.rem(i, nch) - 1, 0), 0)),
                  pl.BlockSpec(meta.shape, lambda i: (0, 0)),
                  pl.BlockSpec((CHUNK, d), lambda i: (jnp.maximum(i - npt, 0), 0)),
                  pl.BlockSpec((1, d), lambda i: (0, 0))],
        out_specs=[row_spec, row_spec],
        out_shape=[jax.ShapeDtypeStruct((r, d), F32), jax.ShapeDtypeStruct((r, d), BF16)],
        compiler_params=_cparams(("arbitrary",)), name=name)(x_prompt, meta, x_sample, g.reshape(1, d))


def _zero_blocks(zero_ref, out_hbm, first, last, sem):
    def blk_copy(b):
        return pltpu.make_async_copy(zero_ref, out_hbm.at[pl.ds(pl.multiple_of(b * MOE_ROWS, MOE_ROWS), MOE_ROWS), :], sem)

    def start(b, carry):
        blk_copy(b).start()
        return carry

    def wait(b, carry):
        blk_copy(b).wait()
        return carry

    lax.fori_loop(first, last, start, 0)
    lax.fori_loop(first, last, wait, 0)


def _dispatch_kernel(pe_ref, dest_ref, h_ref, xs_out, zero_ref, sem, zsem):
    tm = h_ref.shape[0]
    n_blocks = xs_out.shape[0] // MOE_ROWS

    @pl.when(pl.program_id(0) == 0)
    def _():
        zero_ref[...] = jnp.zeros_like(zero_ref)

        def last_block(e):
            b = jnp.maximum(pe_ref[e] // MOE_ROWS - 1, 0)
            return pltpu.make_async_copy(
                zero_ref, xs_out.at[pl.ds(pl.multiple_of(b * MOE_ROWS, MOE_ROWS), MOE_ROWS), :], zsem)

        for e in range(N_EXPERTS):
            last_block(e).start()
        for e in range(N_EXPERTS):
            last_block(e).wait()
        _zero_blocks(zero_ref, xs_out, pe_ref[N_EXPERTS - 1] // MOE_ROWS, n_blocks, zsem)

    _row_copies(tm, lambda r, k: pltpu.make_async_copy(
        h_ref.at[pl.ds(r, 1), :], xs_out.at[pl.ds(dest_ref[0, TOP_K * r + k], 1), :], sem))


def _dispatch_call(h2, dest, pad_end, n_slots, *, tm, name):
    r, d = h2.shape
    grid_spec = pltpu.PrefetchScalarGridSpec(
        num_scalar_prefetch=1, grid=(r // tm,),
        in_specs=[pl.BlockSpec((None, 1, TOP_K * tm), lambda i, pe: (i, 0, 0), memory_space=pltpu.SMEM),
                  pl.BlockSpec((tm, d), lambda i, pe: (i, 0))],
        out_specs=pl.BlockSpec(memory_space=pl.ANY),
        scratch_shapes=[pltpu.VMEM((MOE_ROWS, d), h2.dtype), pltpu.SemaphoreType.DMA(()),
                        pltpu.SemaphoreType.DMA(())])
    return pl.pallas_call(
        _dispatch_kernel, grid_spec=grid_spec, out_shape=jax.ShapeDtypeStruct((n_slots, d), h2.dtype),
        compiler_params=_cparams(("arbitrary",)), name=name)(pad_end, dest.reshape(r // tm, 1, TOP_K * tm), h2)


def _mm_kernel(*refs, has_res):
    if has_res:
        x_ref, w_ref, r_ref, o_ref, wb_ref = refs
    else:
        x_ref, w_ref, o_ref, wb_ref = refs

    @pl.when(pl.program_id(1) == 0)
    def _():
        wb_ref[...] = w_ref[...].astype(BF16)

    acc = _dot(x_ref[...], wb_ref[...])
    if has_res:
        acc = acc + r_ref[...]
    o_ref[...] = acc.astype(o_ref.dtype)


def _mm_call(x, w, l, res=None, *, out_dtype=F32, name):
    r, k = x.shape
    n = w.shape[2]
    tm = _pick_tile(r, 1152, BF16_SUBLANES)
    tn = _pick_tile(n, 1024, LANES)
    in_specs = [pl.BlockSpec((tm, k), lambda j, i: (i, 0)), pl.BlockSpec((None, k, tn), lambda j, i: (l, 0, j))]
    args = [x, w]
    if res is not None:
        in_specs.append(pl.BlockSpec((tm, tn), lambda j, i: (i, j)))
        args.append(res)
    return pl.pallas_call(
        functools.partial(_mm_kernel, has_res=res is not None),
        grid=(n // tn, r // tm), in_specs=in_specs,
        out_specs=pl.BlockSpec((tm, tn), lambda j, i: (i, j)),
        out_shape=jax.ShapeDtypeStruct((r, n), out_dtype),
        scratch_shapes=[pltpu.VMEM((k, tn), BF16)],
        compiler_params=_cparams(("arbitrary", "arbitrary")), name=name)(*args)


def _merge_kernel(*refs, n_branch, n_prompt_tiles):
    yp_refs, ys_refs = refs[:n_branch], refs[n_branch:2 * n_branch]
    g_refs = refs[2 * n_branch:3 * n_branch]
    w_ref, o_ref, wb_ref = refs[3 * n_branch:]
    i = pl.program_id(1)

    @pl.when(i == 0)
    def _():
        wb_ref[...] = w_ref[...].astype(BF16)

    def run(y_refs):
        acc = _sigmoid(g_refs[0][...]) * _dot(y_refs[0][...], wb_ref[0])
        for n in range(1, n_branch):
            acc = acc + _sigmoid(g_refs[n][...]) * _dot(y_refs[n][...], wb_ref[n])
        o_ref[...] = acc.astype(o_ref.dtype)

    @pl.when(i < n_prompt_tiles)
    def _():
        run(yp_refs)

    @pl.when(i >= n_prompt_tiles)
    def _():
        run(ys_refs)


def _merge_call(y_prompt, y_sample, z, w_branch, l, gate_col0, *, name):
    rp, db = y_prompt[0].shape
    bs = y_sample[0].shape[0]
    _, nbr, _, d = w_branch.shape
    tn = _pick_tile(d, 1024, LANES)
    tm = _pick_tile(math.gcd(rp, bs), 384, BF16_SUBLANES)
    npt = rp // tm
    assert gate_col0 % tn == 0 and d % tn == 0
    g0 = gate_col0 // tn
    per = d // tn
    yp_spec = pl.BlockSpec((tm, db), lambda j, i: (jnp.minimum(i, npt - 1), 0))
    ys_spec = pl.BlockSpec((tm, db), lambda j, i: (jnp.maximum(i - npt, 0), 0))
    g_specs = [pl.BlockSpec((tm, tn), lambda j, i, n=n: (i, g0 + per * n + j)) for n in range(nbr)]
    return pl.pallas_call(
        functools.partial(_merge_kernel, n_branch=nbr, n_prompt_tiles=npt), grid=(d // tn, (rp + bs) // tm),
        in_specs=[yp_spec] * nbr + [ys_spec] * nbr + g_specs
        + [pl.BlockSpec((None, nbr, db, tn), lambda j, i: (l, 0, 0, j))],
        out_specs=pl.BlockSpec((tm, tn), lambda j, i: (i, j)),
        out_shape=jax.ShapeDtypeStruct((rp + bs, d), BF16),
        scratch_shapes=[pltpu.VMEM((nbr, db, tn), BF16)],
        compiler_params=_cparams(("arbitrary", "arbitrary")), name=name)(
            *y_prompt, *y_sample, *([z] * nbr), w_branch)


def _shift_rows(x, prev8, k, row):
    rolled = pltpu.roll(x, k, 0)
    head = jnp.where(row[:SUBLANES] < k, pltpu.roll(prev8, k, 0), rolled[:SUBLANES])
    return jnp.concatenate([head, rolled[SUBLANES:]], axis=0)


def _chunk_conv(x, prev8, w_ref, b_ref, row):
    y = b_ref[...] + w_ref[CONV_W - 1:CONV_W, :] * x
    for k in range(1, CONV_W):
        y = y + w_ref[CONV_W - 1 - k:CONV_W - k, :] * _shift_rows(x, prev8, k, row)
    return y


def _rglru_coeffs(xc, wa_ref, ba_ref, wx_ref, bx_ref, lam_ref):
    r = _sigmoid(_blockdiag(xc, wa_ref) + ba_ref[...])
    i = _sigmoid(_blockdiag(xc, wx_ref) + bx_ref[...])
    log_a = -RG_C * r * _softplus(-lam_ref[...])
    a = jnp.exp(log_a)
    mult = jnp.sqrt(1.0 - a * a)
    return a, mult * i * xc


def _pa_kernel(x_ref, cw_ref, cb_ref, wa_ref, ba_ref, wx_ref, bx_ref, lam_ref,
               y_ref, hl_ref, cv_ref, prev_ref, h_ref, *, pad):
    c = pl.program_id(1)
    last = pl.num_programs(1) - 1
    L = x_ref.shape[0]
    row = lax.broadcasted_iota(jnp.int32, (L, 1), 0)
    valid = jnp.logical_or(c > 0, row >= pad)

    @pl.when(c == 0)
    def _():
        prev_ref[...] = jnp.zeros_like(prev_ref)
        h_ref[...] = jnp.zeros_like(h_ref)

    x = jnp.where(valid, x_ref[...], 0.0)
    xc = _chunk_conv(x, prev_ref[...], cw_ref, cb_ref, row)
    prev_ref[...] = x[L - SUBLANES:]
    a, b = _rglru_coeffs(xc, wa_ref, ba_ref, wx_ref, bx_ref, lam_ref)
    a = jnp.where(valid, a, 1.0)
    b = jnp.where(valid, b, 0.0)
    hs = []
    for j in range(x.shape[1] // LANES):
        sl = slice(j * LANES, (j + 1) * LANES)
        aj, bj = a[:, sl], b[:, sl]
        d = 1
        while d < L:
            keep = row >= d
            a_s = jnp.where(keep, pltpu.roll(aj, d, 0), 1.0)
            b_s = jnp.where(keep, pltpu.roll(bj, d, 0), 0.0)
            bj = aj * b_s + bj
            aj = aj * a_s
            d *= 2
        hs.append(aj * h_ref[0:1, sl] + bj)
    h = jnp.concatenate(hs, axis=-1)
    h_ref[0:1, :] = h[L - 1:L]
    y_ref[...] = h.astype(y_ref.dtype)

    @pl.when(c == last)
    def _():
        hl_ref[0] = h[L - 1:L]
        cv_ref[0] = x[L - (CONV_W - 1):]


def _pa_call(z, nseq, nch, pad, p, *, name):
    rows = nseq * nch * CHUNK
    c = p['rg_conv_w'].shape[1]
    full = lambda a: pl.BlockSpec(a.shape, lambda s, k: (0,) * a.ndim)
    params = [p['rg_conv_w'], p['rg_conv_b'], p['rg_wa'], p['rg_ba'], p['rg_wx'], p['rg_bx'], p['rg_lambda']]
    return pl.pallas_call(
        functools.partial(_pa_kernel, pad=pad), grid=(nseq, nch),
        in_specs=[pl.BlockSpec((CHUNK, c), lambda s, k: (s * nch + k, 0))] + [full(a) for a in params],
        out_specs=[pl.BlockSpec((CHUNK, c), lambda s, k: (s * nch + k, 0)),
                   pl.BlockSpec((1, 1, c), lambda s, k: (s, 0, 0)),
                   pl.BlockSpec((1, CONV_W - 1, c), lambda s, k: (s, 0, 0))],
        out_shape=[jax.ShapeDtypeStruct((rows, c), BF16),
                   jax.ShapeDtypeStruct((nseq, 1, c), F32),
                   jax.ShapeDtypeStruct((nseq, CONV_W - 1, c), F32)],
        scratch_shapes=[pltpu.VMEM((SUBLANES, c), F32), pltpu.VMEM((SUBLANES, c), F32)],
        compiler_params=_cparams(("arbitrary", "arbitrary")), name=name)(z, *params)


def _step_conv(x, cs_ref, w_ref, b_ref):
    y = b_ref[...] + w_ref[CONV_W - 1:CONV_W, :] * x
    for k in range(CONV_W - 1):
        y = y + w_ref[k:k + 1, :] * cs_ref[k]
    return y


def _step_conv_state(x, cs_ref, out_ref):
    for k in range(CONV_W - 2):
        out_ref[k] = cs_ref[k + 1]
    out_ref[CONV_W - 2] = x


def _sa_kernel(x_ref, cs_ref, h0_ref, cw_ref, cb_ref, wa_ref, ba_ref, wx_ref, bx_ref, lam_ref,
               y_ref, hn_ref, cv_ref):
    x = x_ref[...]
    xc = _step_conv(x, cs_ref, cw_ref, cb_ref)
    a, b = _rglru_coeffs(xc, wa_ref, ba_ref, wx_ref, bx_ref, lam_ref)
    h = a * h0_ref[...] + b
    hn_ref[...] = h
    y_ref[...] = h.astype(y_ref.dtype)
    _step_conv_state(x, cs_ref, cv_ref)


def _sa_call(z, row0, bs, cs, h0, p, *, name):
    c = h0.shape[1]
    assert row0 % bs == 0
    full = lambda a: pl.BlockSpec(a.shape, lambda i: (0,) * a.ndim)
    params = [p['rg_conv_w'], p['rg_conv_b'], p['rg_wa'], p['rg_ba'], p['rg_wx'], p['rg_bx'], p['rg_lambda']]
    return pl.pallas_call(
        _sa_kernel, grid=(1,),
        in_specs=[pl.BlockSpec((bs, c), lambda i: (row0 // bs, 0)), full(cs), full(h0)] + [full(a) for a in params],
        out_specs=[pl.BlockSpec((bs, c), lambda i: (0, 0)), pl.BlockSpec((bs, c), lambda i: (0, 0)),
                   pl.BlockSpec(cs.shape, lambda i: (0, 0, 0))],
        out_shape=[jax.ShapeDtypeStruct((bs, c), BF16), jax.ShapeDtypeStruct((bs, c), F32),
                   jax.ShapeDtypeStruct(cs.shape, F32)],
        compiler_params=_cparams(("arbitrary",)), name=name)(z, cs, h0, *params)


def _ret_log_g(h):
    return math.log1p(-(2.0 ** (-5.0 - h)))


def _rope(x, cos2, sin2):
    return x * cos2 + pltpu.roll(x, x.shape[1] // 2, 1) * sin2


def _pb_kernel(q_ref, k_ref, v_ref, g_ref, cos_ref, sin_ref, gn_ref, y_ref, so_ref, s_ref, *, pad, heads):
    c = pl.program_id(1)
    last = pl.num_programs(1) - 1
    L = q_ref.shape[0]
    dk = q_ref.shape[1] // heads
    dv = v_ref.shape[1] // heads
    row = lax.broadcasted_iota(jnp.int32, (L, 1), 0)
    valid = jnp.logical_or(c > 0, row >= pad)

    @pl.when(c == 0)
    def _():
        s_ref[...] = jnp.zeros_like(s_ref)

    t = row.astype(F32)
    rel = t - lax.broadcasted_iota(jnp.int32, (1, L), 1).astype(F32)
    cos2, sin2 = cos_ref[...], sin_ref[...]
    for h in range(heads):
        lg = _ret_log_g(h)
        qh = jnp.where(valid, _rope(q_ref[:, h * dk:(h + 1) * dk], cos2, sin2), 0.0)
        kh = jnp.where(valid, _rope(k_ref[:, h * dk:(h + 1) * dk], cos2, sin2), 0.0) * (dk ** -0.5)
        vh = jnp.where(valid, v_ref[:, h * dv:(h + 1) * dv], 0.0)
        decay = jnp.where(rel >= 0, jnp.exp(jnp.maximum(rel, 0.0) * lg), 0.0)
        qb, vb = qh.astype(BF16), vh.astype(BF16)
        scores = _dot_nt(qb, kh.astype(BF16)) * decay
        s_old = s_ref[h]
        o = _dot(scores.astype(BF16), vb) + _dot(qb, s_old.astype(BF16)) * jnp.exp((t + 1.0) * lg)
        k_dec = kh * jnp.exp((L - 1.0 - t) * lg)
        s_ref[h] = math.exp(L * lg) * s_old + _dot_tn(k_dec.astype(BF16), vb)
        sl = slice(h * dv, (h + 1) * dv)
        y_ref[:, sl] = (_head_norm(o) * gn_ref[:, sl] * _silu(g_ref[:, sl])).astype(y_ref.dtype)

    @pl.when(c == last)
    def _():
        so_ref[0] = s_ref[...]


def _pb_call(z, nseq, nch, pad, heads, dk, dv, cos2, sin2, ret_norm, *, name):
    rows = nseq * nch * CHUNK
    hk, hv = heads * dk, heads * dv
    assert hv % hk == 0
    q0 = hv // hk
    return pl.pallas_call(
        functools.partial(_pb_kernel, pad=pad, heads=heads), grid=(nseq, nch),
        in_specs=[pl.BlockSpec((CHUNK, hk), lambda s, k: (s * nch + k, q0)),
                  pl.BlockSpec((CHUNK, hk), lambda s, k: (s * nch + k, q0 + 1)),
                  pl.BlockSpec((CHUNK, hv), lambda s, k: (s * nch + k, 2)),
                  pl.BlockSpec((CHUNK, hv), lambda s, k: (s * nch + k, 3)),
                  pl.BlockSpec((CHUNK, dk), lambda s, k: (k, 0)),
                  pl.BlockSpec((CHUNK, dk), lambda s, k: (k, 0)),
                  pl.BlockSpec((1, hv), lambda s, k: (0, 0))],
        out_specs=[pl.BlockSpec((CHUNK, hv), lambda s, k: (s * nch + k, 0)),
                   pl.BlockSpec((1, heads, dk, dv), lambda s, k: (s, 0, 0, 0))],
        out_shape=[jax.ShapeDtypeStruct((rows, hv), BF16),
                   jax.ShapeDtypeStruct((nseq, heads, dk, dv), F32)],
        scratch_shapes=[pltpu.VMEM((heads, dk, dv), F32)],
        compiler_params=_cparams(("arbitrary", "arbitrary")), name=name)(z, z, z, z, cos2, sin2, ret_norm)


def _first_pass_only(body, state_out_ref):
    @pl.when(pl.program_id(0) == 0)
    def _():
        body()

    @pl.when(pl.program_id(0) > 0)
    def _():
        state_out_ref[...] = jnp.zeros_like(state_out_ref)


def _pass_maps(l, prev, nblk, depth):
    if prev is not None:
        return 1, (lambda g, i: i), (lambda g, i: l)
    assert l == 0
    return depth, (lambda g, i: jnp.where(g == 0, i, nblk - 1)), (lambda g, i: g)


def _sb_kernel(*refs, heads):
    _first_pass_only(functools.partial(_sb_body, *refs, heads=heads), refs[-1])


def _sb_body(q_ref, k_ref, v_ref, g_ref, s_ref, cos_ref, sin_ref, gn_ref, y_ref, so_ref, *, heads):
    nb = q_ref.shape[0]
    dk = q_ref.shape[1] // heads
    dv = v_ref.shape[1] // heads
    row = lax.broadcasted_iota(jnp.int32, (nb, 1), 0)
    cos2, sin2 = cos_ref[...], sin_ref[...]
    for h in range(heads):
        g = math.exp(_ret_log_g(h))
        qh = _rope(q_ref[:, h * dk:(h + 1) * dk], cos2, sin2)
        kh = _rope(k_ref[:, h * dk:(h + 1) * dk], cos2, sin2) * (dk ** -0.5)
        vh = v_ref[:, h * dv:(h + 1) * dv]
        qb, vb = qh.astype(BF16), vh.astype(BF16)
        cross = jnp.zeros((nb, dv), F32)
        for i in range(nb):
            s_old = s_ref[i, h]
            cross = jnp.where(row == i, _dot(qb, s_old.astype(BF16)), cross)
            so_ref[i, h] = g * s_old + _dot_tn(jnp.where(row == i, kh, 0.0), vh)
        qk = jnp.sum(qb.astype(F32) * kh.astype(BF16).astype(F32), axis=-1, keepdims=True)
        o = qk * vb.astype(F32) + g * cross
        sl = slice(h * dv, (h + 1) * dv)
        y_ref[:, sl] = (_head_norm(o) * gn_ref[:, sl] * _silu(g_ref[:, sl])).astype(y_ref.dtype)


def _sb_call(z, row0, bs, heads, dk, dv, state, l, prev, cos2, sin2, ret_norm, *, name):
    hk, hv = heads * dk, heads * dv
    nb = SUBLANES
    assert bs % nb == 0 and row0 % nb == 0
    r0 = row0 // nb
    q0 = hv // hk
    passes, blk, lay = _pass_maps(l, prev, bs // nb, state.shape[0])
    return pl.pallas_call(
        _skip_ref(functools.partial(_sb_kernel, heads=heads), 8 if prev is not None else None),
        grid=(passes, bs // nb),
        in_specs=[pl.BlockSpec((nb, hk), lambda g, i: (r0 + blk(g, i), q0)),
                  pl.BlockSpec((nb, hk), lambda g, i: (r0 + blk(g, i), q0 + 1)),
                  pl.BlockSpec((nb, hv), lambda g, i: (r0 + blk(g, i), 2)),
                  pl.BlockSpec((nb, hv), lambda g, i: (r0 + blk(g, i), 3)),
                  pl.BlockSpec((None, nb, heads, dk, dv), lambda g, i: (l, blk(g, i), 0, 0, 0)),
                  pl.BlockSpec((1, dk), lambda g, i: (0, 0)),
                  pl.BlockSpec((1, dk), lambda g, i: (0, 0)),
                  pl.BlockSpec((1, hv), lambda g, i: (0, 0))] + _prev_specs(prev),
        out_specs=[pl.BlockSpec((nb, hv), lambda g, i: (blk(g, i), 0)),
                   pl.BlockSpec((None, nb, heads, dk, dv), lambda g, i: (lay(g, i), i, 0, 0, 0))],
        out_shape=[jax.ShapeDtypeStruct((bs, hv), BF16), jax.ShapeDtypeStruct(state.shape, F32)],
        input_output_aliases={8: 1} if prev is not None else {},
        compiler_params=_cparams(("arbitrary", "arbitrary")), name=name)(
            z, z, z, z, state, cos2, sin2, ret_norm, *([] if prev is None else [prev]))


def _mlstm_qkv_gates(x, xc, wq_ref, wk_ref, wv_ref, wif_ref, bif_ref, dh):
    q = _blockdiag(xc, wq_ref)
    k = _blockdiag(xc, wk_ref, scale=dh ** -0.5)
    v = _blockdiag(x, wv_ref)
    c = x.shape[1]
    gates = (_dot(q.astype(BF16), wif_ref[0:c, :]) + _dot(k.astype(BF16), wif_ref[c:2 * c, :])
             + _dot(v.astype(BF16), wif_ref[2 * c:3 * c, :]) + bif_ref[...])
    return q, k, v, gates


def _pc_kernel(x_ref, o_ref, cw_ref, cb_ref, wq_ref, wk_ref, wv_ref, wif_ref, bif_ref, gn_ref, sk_ref,
               y_ref, co_ref, no_ref, mo_ref, cv_ref, prev_ref, c_ref, n_ref, m_ref, *, pad, heads):
    c = pl.program_id(1)
    last = pl.num_programs(1) - 1
    L = x_ref.shape[0]
    dh = x_ref.shape[1] // heads
    row = lax.broadcasted_iota(jnp.int32, (L, 1), 0)
    valid = jnp.logical_or(c > 0, row >= pad)

    @pl.when(c == 0)
    def _():
        prev_ref[...] = jnp.zeros_like(prev_ref)
        c_ref[...] = jnp.zeros_like(c_ref)
        n_ref[...] = jnp.zeros_like(n_ref)
        m_ref[...] = jnp.zeros_like(m_ref)

    x = jnp.where(valid, x_ref[...], 0.0)
    xc = _silu(_chunk_conv(x, prev_ref[...], cw_ref, cb_ref, row))
    prev_ref[...] = x[L - SUBLANES:]
    q, k, v, gates = _mlstm_qkv_gates(x, xc, wq_ref, wk_ref, wv_ref, wif_ref, bif_ref, dh)
    lane = lax.broadcasted_iota(jnp.int32, gates.shape, 1)
    is_i = lane < heads
    ig_c = jnp.where(jnp.logical_and(valid, is_i), gates, NEG)
    lf_c = jnp.where(valid, _log_sigmoid(gates), 0.0)
    ig_r = ig_c.T
    lf_r = lf_c.T
    ti = lax.broadcasted_iota(jnp.int32, (L, L), 0)
    si = lax.broadcasted_iota(jnp.int32, (L, L), 1)
    causal = ti >= si
    ones_lower = jnp.where(causal, 1.0, 0.0).astype(BF16)
    ones_upper = jnp.where(si >= ti, 1.0, 0.0).astype(BF16)
    b_c = sum(_dot(ones_lower, part) for part in _split3(lf_c))
    b_r = sum(_dot(part, ones_upper) for part in _split3(lf_r))
    m_all = m_ref[...]
    m_lane = lax.broadcasted_iota(jnp.int32, m_all.shape, 1)
    m_new = jnp.zeros_like(m_all)
    for h in range(heads):
        sl = slice(h * dh, (h + 1) * dh)
        bc = b_c[:, heads + h:heads + h + 1]
        br = b_r[heads + h:heads + h + 1, :]
        igr = ig_r[h:h + 1, :]
        igc = ig_c[:, h:h + 1]
        m_prev = m_all[0:1, h:h + 1]
        log_d = jnp.where(causal, bc - br + igr, NEG)
        log_inter = bc + m_prev
        m_t = jnp.maximum(log_inter, jnp.max(log_d, axis=-1, keepdims=True))
        d_m = jnp.exp(log_d - m_t)
        w_inter = jnp.exp(log_inter - m_t)
        qh, kh, vh = q[:, sl], k[:, sl], v[:, sl]
        qb, kb, vb = qh.astype(BF16), kh.astype(BF16), vh.astype(BF16)
        scores = _dot_nt(qb, kb) * d_m
        c_old = c_ref[h]
        n_old = n_ref[h:h + 1, :]
        num = _dot(scores.astype(BF16), vb) + w_inter * _dot(qb, c_old.astype(BF16))
        den = jnp.sum(scores, axis=-1, keepdims=True) + w_inter * jnp.sum(qh * n_old, axis=-1, keepdims=True)
        hh = num / jnp.maximum(jnp.abs(den), jnp.exp(-m_t))
        m_end = m_t[L - 1:L]
        b_last = bc[L - 1:L]
        w_c = jnp.exp(b_last - bc + igc - m_end)
        decay_c = jnp.exp(b_last + m_prev - m_end)
        kw = kh * w_c
        c_ref[h] = decay_c * c_old + _dot_tn(kw.astype(BF16), vb)
        n_ref[h:h + 1, :] = decay_c * n_old + jnp.sum(kw, axis=0, keepdims=True)
        m_new = jnp.where(m_lane == h, m_end, m_new)
        hm = _sigmoid(o_ref[:, sl]) * hh
        y_ref[:, sl] = (_head_norm(hm) * gn_ref[:, sl] + sk_ref[:, sl] * xc[:, sl]).astype(y_ref.dtype)
    m_ref[...] = m_new

    @pl.when(c == last)
    def _():
        co_ref[0] = c_ref[...]
        no_ref[0] = n_ref[...]
        mo_ref[0] = m_ref[...]
        cv_ref[0] = x[L - (CONV_W - 1):]


def _pc_call(z, nseq, nch, pad, heads, xcol, p, *, name):
    rows = nseq * nch * CHUNK
    c = p['m_conv_w'].shape[1]
    dh = c // heads
    full = lambda a: pl.BlockSpec(a.shape, lambda s, k: (0,) * a.ndim)
    params = [p['m_conv_w'], p['m_conv_b'], p['m_wq'], p['m_wk'], p['m_wv'], p['m_w_if'], p['m_b_if'],
              p['m_norm'], p['m_skip']]
    return pl.pallas_call(
        functools.partial(_pc_kernel, pad=pad, heads=heads), grid=(nseq, nch),
        in_specs=[pl.BlockSpec((CHUNK, c), lambda s, k: (s * nch + k, xcol)),
                  pl.BlockSpec((CHUNK, c), lambda s, k: (s * nch + k, xcol + 1))] + [full(a) for a in params],
        out_specs=[pl.BlockSpec((CHUNK, c), lambda s, k: (s * nch + k, 0)),
                   pl.BlockSpec((1, heads, dh, dh), lambda s, k: (s, 0, 0, 0)),
                   pl.BlockSpec((1, SUBLANES, dh), lambda s, k: (s, 0, 0)),
                   pl.BlockSpec((1, SUBLANES, LANES), lambda s, k: (s, 0, 0)),
                   pl.BlockSpec((1, CONV_W - 1, c), lambda s, k: (s, 0, 0))],
        out_shape=[jax.ShapeDtypeStruct((rows, c), BF16),
                   jax.ShapeDtypeStruct((nseq, heads, dh, dh), F32),
                   jax.ShapeDtypeStruct((nseq, SUBLANES, dh), F32),
                   jax.ShapeDtypeStruct((nseq, SUBLANES, LANES), F32),
                   jax.ShapeDtypeStruct((nseq, CONV_W - 1, c), F32)],
        scratch_shapes=[pltpu.VMEM((SUBLANES, c), F32), pltpu.VMEM((heads, dh, dh), F32),
                        pltpu.VMEM((SUBLANES, dh), F32), pltpu.VMEM((SUBLANES, LANES), F32)],
        compiler_params=_cparams(("arbitrary", "arbitrary")), name=name)(z, z, *params)


def _sc_kernel(*refs, heads):
    _first_pass_only(functools.partial(_sc_body, *refs, heads=heads), refs[-4])


def _sc_body(x_ref, o_ref, cs_ref, c_ref, n_ref, m_ref, cw_ref, cb_ref, wq_ref, wk_ref, wv_ref, wif_ref, bif_ref,
             gn_ref, sk_ref, y_ref, co_ref, no_ref, mo_ref, cv_ref, *, heads):
    nb = x_ref.shape[0]
    dh = x_ref.shape[1] // heads
    row = lax.broadcasted_iota(jnp.int32, (nb, 1), 0)
    x = x_ref[...]
    xc = _silu(_step_conv(x, cs_ref, cw_ref, cb_ref))
    q, k, v, gates = _mlstm_qkv_gates(x, xc, wq_ref, wk_ref, wv_ref, wif_ref, bif_ref, dh)
    lf = _log_sigmoid(gates)
    m_old = m_ref[...]
    lane = lax.broadcasted_iota(jnp.int32, m_old.shape, 1)
    m_new = jnp.zeros_like(m_old)
    for h in range(heads):
        sl = slice(h * dh, (h + 1) * dh)
        ig = gates[:, h:h + 1]
        log_inter = lf[:, heads + h:heads + h + 1] + m_old[:, h:h + 1]
        m_t = jnp.maximum(log_inter, ig)
        d_m = jnp.exp(ig - m_t)
        w_inter = jnp.exp(log_inter - m_t)
        qh, kh, vh = q[:, sl], k[:, sl], v[:, sl]
        qb, kb, vb = qh.astype(BF16), kh.astype(BF16), vh.astype(BF16)
        sc = jnp.sum(qb.astype(F32) * kb.astype(F32), axis=-1, keepdims=True) * d_m
        n_old = n_ref[:, sl]
        kw = kh * d_m
        qc = jnp.zeros((nb, dh), F32)
        for i in range(nb):
            c_old = c_ref[i, h]
            qc = jnp.where(row == i, _dot(qb, c_old.astype(BF16)), qc)
            co_ref[i, h] = (w_inter[i:i + 1] * c_old
                            + _dot_tn(jnp.where(row == i, kw, 0.0), vh))
        num = sc * vb.astype(F32) + w_inter * qc
        den = sc + w_inter * jnp.sum(qh * n_old, axis=-1, keepdims=True)
        hh = num / jnp.maximum(jnp.abs(den), jnp.exp(-m_t))
        no_ref[:, sl] = w_inter * n_old + kw
        m_new = jnp.where(lane == h, m_t, m_new)
        hm = _sigmoid(o_ref[:, sl]) * hh
        y_ref[:, sl] = (_head_norm(hm) * gn_ref[:, sl] + sk_ref[:, sl] * xc[:, sl]).astype(y_ref.dtype)
    mo_ref[...] = m_new
    _step_conv_state(x, cs_ref, cv_ref)


def _sc_call(z, row0, bs, heads, xcol, cs, c_state, l, prev, n_state, m_state, p, *, name):
    c = p['m_conv_w'].shape[1]
    dh = c // heads
    nb = SUBLANES
    assert bs % nb == 0 and row0 % nb == 0
    r0 = row0 // nb
    full = lambda a: pl.BlockSpec(a.shape, lambda g, i: (0,) * a.ndim)
    params = [p['m_conv_w'], p['m_conv_b'], p['m_wq'], p['m_wk'], p['m_wv'], p['m_w_if'], p['m_b_if'],
              p['m_norm'], p['m_skip']]
    passes, blk, lay = _pass_maps(l, prev, bs // nb, c_state.shape[0])
    cs_spec = pl.BlockSpec((CONV_W - 1, nb, c), lambda g, i: (0, blk(g, i), 0))
    row_spec = lambda w: pl.BlockSpec((nb, w), lambda g, i: (blk(g, i), 0))
    return pl.pallas_call(
        _skip_ref(functools.partial(_sc_kernel, heads=heads), 6 + len(params) if prev is not None else None),
        grid=(passes, bs // nb),
        in_specs=[pl.BlockSpec((nb, c), lambda g, i: (r0 + blk(g, i), xcol)),
                  pl.BlockSpec((nb, c), lambda g, i: (r0 + blk(g, i), xcol + 1)),
                  cs_spec,
                  pl.BlockSpec((None, nb, heads, dh, dh), lambda g, i: (l, blk(g, i), 0, 0, 0)),
                  row_spec(c), row_spec(LANES)] + [full(a) for a in params] + _prev_specs(prev),
        out_specs=[row_spec(c),
                   pl.BlockSpec((None, nb, heads, dh, dh), lambda g, i: (lay(g, i), i, 0, 0, 0)),
                   row_spec(c), row_spec(LANES), cs_spec],
        out_shape=[jax.ShapeDtypeStruct((bs, c), BF16), jax.ShapeDtypeStruct(c_state.shape, F32),
                   jax.ShapeDtypeStruct((bs, c), F32), jax.ShapeDtypeStruct((bs, LANES), F32),
                   jax.ShapeDtypeStruct(cs.shape, F32)],
        input_output_aliases={6 + len(params): 1} if prev is not None else {},
        compiler_params=_cparams(("arbitrary", "arbitrary")), name=name)(
            z, z, cs, c_state, n_state, m_state, *params, *([] if prev is None else [prev]))


def _ffn_kernel(sbe_ref, sbs_ref, sbn_ref, nv_ref, xs_hbm, wg_ref, wu_ref, wd_ref, ys_hbm,
                acc_ref, xp_ref, yp_ref, h_ref, wgb_ref, wub_ref, wdb_ref, zero_ref, isem, osem, *, nka, nkb):
    del sbe_ref
    j, s = pl.program_id(0), pl.program_id(1)
    nblk = sbn_ref[j]
    blk0 = sbs_ref[j]
    max_blk = acc_ref.shape[0] // MOE_ROWS
    f = wgb_ref.shape[1]
    dn = wdb_ref.shape[0]

    @pl.when(jnp.logical_and(j == 0, s == 0))
    def _():
        zero_ref[...] = jnp.zeros_like(zero_ref)
        xp_ref[...] = jnp.zeros_like(xp_ref)

    def hbm_rows(ref, p):
        return ref.at[pl.ds(pl.multiple_of((blk0 + p) * MOE_ROWS, MOE_ROWS), MOE_ROWS), :]

    def vmem_rows(ref, p):
        return ref.at[pl.ds(p * MOE_ROWS, MOE_ROWS), :]

    def for_blocks(fn, n):
        for p in range(max_blk):
            @pl.when(p < n)
            def _():
                fn(p)

    def block_loop(body):
        per = FFN_ROWS // MOE_ROWS

        def step(b, carry):
            body(pl.ds(pl.multiple_of(b * FFN_ROWS, FFN_ROWS), FFN_ROWS))
            return carry
        lax.fori_loop(0, (nblk + per - 1) // per, step, 0)

    @pl.when(nblk > 0)
    def _():
        @pl.when(s == 0)
        def _():
            x_copy = lambda p: pltpu.make_async_copy(hbm_rows(xs_hbm, p), vmem_rows(xp_ref, p), isem)
            for_blocks(lambda p: x_copy(p).start(), nblk)
            for_blocks(lambda p: x_copy(p).wait(), nblk)

        for c in range(nka):
            @pl.when(s == c)
            def _(c=c):
                wgb_ref[...] = wg_ref[...].astype(BF16)
                wub_ref[...] = wu_ref[...].astype(BF16)

                def body(rows):
                    x = _unpack_half(xp_ref[rows, :], c).astype(BF16)
                    g = _dot(x, wgb_ref[...])
                    u = _dot(x, wub_ref[...])
                    if c > 0:
                        g = g + acc_ref[rows, 0:f]
                        u = u + acc_ref[rows, f:2 * f]
                    if c < nka - 1:
                        acc_ref[rows, 0:f] = g
                        acc_ref[rows, f:2 * f] = u
                    else:
                        h_ref[rows, :] = (_silu(g) * u).astype(BF16)
                block_loop(body)

        for c in range(nkb):
            @pl.when(s == nka + c)
            def _(c=c):
                wdb_ref[...] = wd_ref[...].astype(BF16)

                def body(rows):
                    y = _dot(h_ref[rows, c * dn:(c + 1) * dn], wdb_ref[...])
                    if c > 0:
                        y = y + acc_ref[rows, :]
                    acc_ref[rows, :] = y
                block_loop(body)

                if c == nkb - 1:
                    def pack_block(p):
                        sl = slice(p * MOE_ROWS, (p + 1) * MOE_ROWS)
                        yp_ref[sl, :] = _pack_halves(acc_ref[sl, :])
                    for_blocks(pack_block, nblk)
                    y_copy = lambda p: pltpu.make_async_copy(vmem_rows(yp_ref, p), hbm_rows(ys_hbm, p), osem)
                    for_blocks(lambda p: y_copy(p).start(), nblk)
                    for_blocks(lambda p: y_copy(p).wait(), nblk)

    @pl.when(jnp.logical_and(j == pl.num_programs(0) - 1, s == pl.num_programs(1) - 1))
    def _():
        _zero_blocks(zero_ref, ys_hbm, nv_ref[0], ys_hbm.shape[0] // MOE_ROWS, osem)


def _ffn_call(xs, tables, wg, wu, wd, l, *, name):
    sb_e, sb_blk0, sb_nblk, n_valid = tables
    ns, dp = xs.shape
    d, f = wg.shape[2], wg.shape[3]
    nsb = sb_e.shape[0]
    nka, nkb = d // FFN_GU_CHUNK, f // FFN_DN_CHUNK
    assert xs.dtype == U32 and dp * 2 == d and nka == 2 and FFN_GU_CHUNK == dp
    assert f % FFN_DN_CHUNK == 0 and 2 * f <= d and MOE_SB_ROWS % FFN_ROWS == 0 and FFN_ROWS % MOE_ROWS == 0

    def gu_map(j, s, sbe, sbs, sbn, nv):
        return (l, sbe[j], jnp.where(j < nv[1], jnp.minimum(s, nka - 1), nka - 1), 0)

    def dn_map(j, s, sbe, sbs, sbn, nv):
        prev = jnp.maximum(j - 1, 0)
        e = jnp.where(s >= nka, sbe[j], sbe[prev])
        c = jnp.where(j < nv[1], jnp.where(s >= nka, s - nka, jnp.where(j == 0, 0, nkb - 1)), nkb - 1)
        return (l, e, c, 0)

    grid_spec = pltpu.PrefetchScalarGridSpec(
        num_scalar_prefetch=4, grid=(nsb, nka + nkb),
        in_specs=[pl.BlockSpec(memory_space=pl.ANY),
                  pl.BlockSpec((None, None, FFN_GU_CHUNK, f), gu_map),
                  pl.BlockSpec((None, None, FFN_GU_CHUNK, f), gu_map),
                  pl.BlockSpec((None, None, FFN_DN_CHUNK, d), dn_map)],
        out_specs=pl.BlockSpec(memory_space=pl.ANY),
        scratch_shapes=[pltpu.VMEM((MOE_SB_ROWS, d), F32), pltpu.VMEM((MOE_SB_ROWS, dp), U32),
                        pltpu.VMEM((MOE_SB_ROWS, dp), U32), pltpu.VMEM((MOE_SB_ROWS, f), BF16),
                        pltpu.VMEM((FFN_GU_CHUNK, f), BF16), pltpu.VMEM((FFN_GU_CHUNK, f), BF16),
                        pltpu.VMEM((FFN_DN_CHUNK, d), BF16), pltpu.VMEM((MOE_ROWS, dp), U32),
                        pltpu.SemaphoreType.DMA(()), pltpu.SemaphoreType.DMA(())])
    return pl.pallas_call(
        functools.partial(_ffn_kernel, nka=nka, nkb=nkb), grid_spec=grid_spec,
        out_shape=jax.ShapeDtypeStruct((ns, dp), U32),
        compiler_params=_cparams(("arbitrary", "arbitrary")), name=name)(
            sb_e, sb_blk0, sb_nblk, n_valid, xs, wg, wu, wd)


def _take(table, idx):
    return jnp.sum(jnp.where(idx[:, None] == jnp.arange(table.shape[0])[None, :], table[None, :], 0), axis=1)


def _route_tables(ids, tile_cnt, r, tm):
    eidx = jnp.arange(N_EXPERTS, dtype=jnp.int32)
    cnt_t = tile_cnt[:, 0, :N_EXPERTS].astype(jnp.int32)
    counts = jnp.sum(cnt_t, axis=0)
    tile_off = jnp.cumsum(cnt_t, axis=0) - cnt_t
    padded = (counts + MOE_ROWS - 1) // MOE_ROWS * MOE_ROWS
    pad_end = jnp.cumsum(padded)
    pad_start = pad_end - padded
    base = jnp.repeat(pad_start[None, :] + tile_off, tm, axis=0)
    e = ids[:, :TOP_K]
    rank = ids[:, TOP_K:2 * TOP_K]
    dest = rank + jnp.sum(jnp.where(e[:, :, None] == eidx, base[:, None, :], 0), axis=-1)
    n_slots = (r * TOP_K // MOE_ROWS + N_EXPERTS) * MOE_ROWS
    nsb_e = (padded + MOE_SB_ROWS - 1) // MOE_SB_ROWS
    sb_end = jnp.cumsum(nsb_e)
    n_sb = sb_end[-1]
    nsb = n_slots // MOE_SB_ROWS + N_EXPERTS
    j = jnp.arange(nsb, dtype=jnp.int32)
    valid = j < n_sb
    e_j = jnp.minimum(jnp.sum(sb_end[None, :] <= j[:, None], axis=1), N_EXPERTS - 1)
    t = j - _take(sb_end - nsb_e, e_j)
    sb_blk0 = jnp.where(valid, (_take(pad_start, e_j) + t * MOE_SB_ROWS) // MOE_ROWS, 0)
    sb_nblk = jnp.where(valid, jnp.clip((_take(padded, e_j) - t * MOE_SB_ROWS) // MOE_ROWS,
                                        0, MOE_SB_ROWS // MOE_ROWS), 0)
    e_last = jnp.sum(jnp.where(j == n_sb - 1, e_j, 0))
    sb_e = jnp.where(valid, e_j, e_last)
    n_valid = jnp.stack([pad_end[-1] // MOE_ROWS, n_sb])
    i32 = lambda a: a.astype(jnp.int32)
    return i32(dest), i32(pad_end), (i32(sb_e), i32(sb_blk0), i32(sb_nblk), i32(n_valid)), n_slots


def _expand_blockdiag(w):
    nb, bs, _ = w.shape
    if bs == LANES:
        return w.astype(BF16)
    per = LANES // bs
    eye = jnp.eye(per, dtype=w.dtype)
    wt = w.reshape(nb // per, per, bs, bs)
    out = jnp.einsum('tpbc,pq->tpbqc', wt, eye)
    return out.reshape(nb // per, LANES, LANES).astype(BF16)


def _rope_tables(pos, half):
    freq = ROPE_BASE ** (-jnp.arange(half, dtype=F32) / half)
    ang = pos[:, None] * freq[None, :]
    cos, sin = jnp.cos(ang), jnp.sin(ang)
    return jnp.concatenate([cos, cos], axis=-1), jnp.concatenate([-sin, sin], axis=-1)


def kernel(x_prompt, x_sample, state_rglru_h, state_rglru_conv, state_ret, state_mlstm_C, state_mlstm_n,
           state_mlstm_m, state_mlstm_conv, meta_tokens, norm_mix, norm_ffn, norm_final, w_in, rg_conv_w,
           rg_conv_b, rg_wa, rg_ba, rg_wx, rg_bx, rg_lambda, ret_norm, m_conv_w, m_conv_b, m_wq, m_wk, m_wv,
           m_w_if, m_b_if, m_norm, m_skip, w_branch, w_out, moe_w_group, moe_b_group, moe_w_expert,
           moe_b_expert, moe_w_gate, moe_w_up, moe_w_down):
    bp, seq, d = x_prompt.shape
    bs = x_sample.shape[0]
    n_meta = meta_tokens.shape[0]
    depth = w_in.shape[0]
    d_rnn = state_rglru_h.shape[2]
    _, _, r_heads, r_dk, r_dv = state_ret.shape
    m_heads, m_dh = state_mlstm_C.shape[2], state_mlstm_C.shape[3]
    d_m = m_heads * m_dh
    assert seq % CHUNK == 0 and n_meta <= CHUNK and x_sample.shape[1] == 1
    pad = CHUNK - n_meta
    nch = 1 + seq // CHUNK
    rp = bp * nch * CHUNK
    r = rp + bs
    assert d_rnn == r_heads * r_dv == d_m
    xm_col = (d_rnn + 2 * r_heads * r_dk + 2 * r_heads * r_dv) // d_m
    gate_col0 = d_rnn + 2 * r_heads * r_dk + 2 * r_heads * r_dv + 2 * d_m

    pos_p = jnp.arange(nch * CHUNK, dtype=F32) - pad
    cos_p, sin_p = _rope_tables(pos_p, r_dk // 2)
    pos_s = jnp.full((1,), float(PAST_LEN), F32)
    cos_s, sin_s = _rope_tables(pos_s, r_dk // 2)

    tm = _pick_tile(r, 384, BF16_SUBLANES)
    assert rp % CHUNK == 0 and bs % BF16_SUBLANES == 0
    prompt_states, sample_states = [], []
    combine = None
    ret_s = c_s = None
    for l in range(depth):
        p = {'rg_conv_w': rg_conv_w[l], 'rg_conv_b': rg_conv_b[l].reshape(1, -1),
             'rg_wa': _expand_blockdiag(rg_wa[l]), 'rg_ba': rg_ba[l].reshape(1, -1),
             'rg_wx': _expand_blockdiag(rg_wx[l]), 'rg_bx': rg_bx[l].reshape(1, -1),
             'rg_lambda': rg_lambda[l].reshape(1, -1),
             'm_conv_w': m_conv_w[l], 'm_conv_b': m_conv_b[l].reshape(1, -1),
             'm_wq': _expand_blockdiag(m_wq[l]), 'm_wk': _expand_blockdiag(m_wk[l]),
             'm_wv': _expand_blockdiag(m_wv[l]),
             'm_w_if': jnp.pad(m_w_if[l], ((0, 0), (0, LANES - 2 * m_heads))).astype(BF16),
             'm_b_if': jnp.pad(m_b_if[l], (0, LANES - 2 * m_heads)).reshape(1, LANES),
             'm_norm': m_norm[l].reshape(1, -1), 'm_skip': m_skip[l].reshape(1, -1)}
        gn_ret = ret_norm[l].reshape(1, -1)

        if l == 0:
            x, h = _embed_norm_call(x_prompt, meta_tokens, x_sample.reshape(bs, d), norm_mix[l], nch,
                                    name=f'norm_mix{l}')
        else:
            x, h = _norm_call(x, norm_mix[l], tm=tm, grid=(r // tm,), row_tile=lambda i: i, want_x=True,
                              h_shape=(r, d), h_dtype=BF16, h_spec=pl.BlockSpec((tm, d), lambda i: (i, 0)),
                              combine=combine, name=f'norm_mix{l}')
        z = _mm_call(h, w_in, l, name=f'in_proj{l}')

        ya_p, rgh_p, rgc_p = _pa_call(z, bp, nch, pad, p, name=f'rglru_p{l}')
        yb_p, ret_p = _pb_call(z, bp, nch, pad, r_heads, r_dk, r_dv, cos_p, sin_p, gn_ret, name=f'ret_p{l}')
        yc_p, c_p, n_p, m_p, mc_p = _pc_call(z, bp, nch, pad, m_heads, xm_col, p, name=f'mlstm_p{l}')

        rg_cs = jnp.transpose(state_rglru_conv[l], (1, 0, 2))
        ya_s, rgh_s, rgc_s = _sa_call(z, rp, bs, rg_cs, state_rglru_h[l], p, name=f'rglru_s{l}')
        yb_s, ret_s = _sb_call(z, rp, bs, r_heads, r_dk, r_dv, state_ret, l, ret_s, cos_s, sin_s, gn_ret,
                               name=f'ret_s{l}')
        m_cs = jnp.transpose(state_mlstm_conv[l], (1, 0, 2))
        m_in = jnp.pad(state_mlstm_m[l], ((0, 0), (0, LANES - m_heads)))
        yc_s, c_s, n_s, m_s, mc_s = _sc_call(z, rp, bs, m_heads, xm_col, m_cs, state_mlstm_C, l, c_s,
                                             state_mlstm_n[l].reshape(bs, d_m), m_in, p, name=f'mlstm_s{l}')

        prompt_states.append((rgh_p[:, 0], rgc_p, ret_p, c_p, n_p[:, :m_heads], m_p[:, 0, :m_heads], mc_p))
        sample_states.append((rgh_s, jnp.transpose(rgc_s, (1, 0, 2)), None, None,
                              n_s.reshape(bs, m_heads, m_dh), m_s[:, :m_heads], jnp.transpose(mc_s, (1, 0, 2))))

        merged = _merge_call((ya_p, yb_p, yc_p), (ya_s, yb_s, yc_s), z, w_branch, l, gate_col0, name=f'merge{l}')
        x = _mm_call(merged, w_out, l, res=x, name=f'out_proj{l}')

        wr = jnp.pad(jnp.concatenate([moe_w_group[l], moe_w_expert[l]], axis=1),
                     ((0, 0), (0, LANES - N_GROUPS - N_EXPERTS)))
        br = jnp.pad(jnp.concatenate([moe_b_group[l], moe_b_expert[l]]), (0, LANES - N_GROUPS - N_EXPERTS))
        h2, ids, wts, tile_cnt = _norm_call(
            x, norm_ffn[l], tm=tm, grid=(r // tm,), row_tile=lambda i: i, want_x=False, h_shape=(r, d // 2),
            h_dtype=U32, h_spec=pl.BlockSpec((tm, d // 2), lambda i: (i, 0)), router=(wr, br.reshape(1, LANES)),
            name=f'norm_router{l}')
        dest, pad_end, sb_tables, n_slots = _route_tables(ids, tile_cnt, r, tm)
        xs = _dispatch_call(h2, dest, pad_end, n_slots, tm=tm, name=f'moe_dispatch{l}')
        ys = _ffn_call(xs, sb_tables, moe_w_gate, moe_w_up, moe_w_down, l, name=f'moe_ffn{l}')
        combine = (dest, wts, ys)

    (y_prompt,) = _norm_call(
        x, norm_final, tm=CHUNK, grid=(bp, seq // CHUNK), row_tile=lambda b, c: b * nch + 1 + c, want_x=False,
        h_shape=(bp, seq, d), h_dtype=F32, h_spec=pl.BlockSpec((None, CHUNK, d), lambda b, c: (b, c, 0)),
        combine=combine, name='norm_final_p')
    ts = _pick_tile(bs, CHUNK, SUBLANES)
    assert rp % ts == 0
    (y_sample,) = _norm_call(
        x, norm_final, tm=ts, grid=(bs // ts,), row_tile=lambda i: rp // ts + i, want_x=False,
        h_shape=(bs, d), h_dtype=F32, h_spec=pl.BlockSpec((ts, d), lambda i: (i, 0)),
        combine=combine, name='norm_final_s')
    pn = [jnp.stack([s[i] for s in prompt_states], axis=0) for i in range(7)]
    sn = [None if sample_states[0][i] is None else jnp.stack([s[i] for s in sample_states], axis=0)
          for i in range(7)]
    sn[2], sn[3] = ret_s, c_s
    return (y_prompt, y_sample.reshape(bs, 1, d), *pn, *sn)
```

```python
import functools
import math

import jax
import jax.numpy as jnp
import numpy as np
from jax import lax
from jax.experimental import pallas as pl
from jax.experimental.pallas import tpu as pltpu

F32 = jnp.float32
BF16 = jnp.bfloat16

LANES = 128
SUBLANES = 8
BF16_SUBLANES = 16
VMEM_LIMIT_BYTES = 56 * 1024 * 1024

CHUNK = 128
CONV_W = 4
EPS = 1e-6
RG_C = 8.0
ROPE_BASE = 10000.0
PAST_LEN = 16384
N_GROUPS = 4
EXP_PER_GROUP = 8
N_EXPERTS = N_GROUPS * EXP_PER_GROUP
TOP_K = 2
MOE_ROWS = 128
MOE_SB_ROWS = 1024
FFN_ROWS = 256
FFN_GU_CHUNK = 1024
FFN_DN_CHUNK = 512
NEG = -1e30


def _cparams(sem, vmem=VMEM_LIMIT_BYTES):
    return pltpu.CompilerParams(dimension_semantics=sem, vmem_limit_bytes=vmem)


def _pick_tile(n, cap, mult):
    best = None
    for t in range(mult, min(n, cap) + 1, mult):
        if n % t == 0:
            best = t
    assert best is not None, (n, cap, mult)
    return best


def _skip_ref(fn, idx):
    if idx is None:
        return fn

    def wrapped(*refs):
        return fn(*refs[:idx], *refs[idx + 1:])
    return wrapped


def _prev_specs(prev):
    return [] if prev is None else [pl.BlockSpec(memory_space=pl.ANY)]


def _dot(a, b):
    return jnp.dot(a, b, preferred_element_type=F32)


def _dot_nt(a, b):
    return lax.dot_general(a, b, (((1,), (1,)), ((), ())), preferred_element_type=F32)


def _dot_tn(a, b):
    return lax.dot_general(a, b, (((0,), (0,)), ((), ())), preferred_element_type=F32)


def _split3(x):
    hi = x.astype(BF16)
    r1 = x - hi.astype(F32)
    mid = r1.astype(BF16)
    lo = (r1 - mid.astype(F32)).astype(BF16)
    return hi, mid, lo


U32 = jnp.uint32


def _pack_halves(x):
    n = x.shape[1] // 2
    lo = lax.bitcast_convert_type(x[:, :n].astype(BF16).astype(F32), U32)
    hi = lax.bitcast_convert_type(x[:, n:].astype(BF16).astype(F32), U32)
    return (lo >> 16) | (hi & U32(0xFFFF0000))


def _unpack_half(p, half):
    bits = (p << 16) if half == 0 else (p & U32(0xFFFF0000))
    return lax.bitcast_convert_type(bits, F32)


def _dot_hi(a, b):
    return jnp.dot(a, b, preferred_element_type=F32, precision=lax.Precision.HIGHEST)


def _sigmoid(x):
    return 1.0 / (1.0 + jnp.exp(-x))


def _silu(x):
    return x * _sigmoid(x)


def _log_sigmoid(x):
    return jnp.minimum(x, 0.0) - jnp.log(1.0 + jnp.exp(-jnp.abs(x)))


def _softplus(x):
    return jnp.maximum(x, 0.0) + jnp.log(1.0 + jnp.exp(-jnp.abs(x)))


def _head_norm(o):
    mu = jnp.mean(o, axis=-1, keepdims=True)
    d = o - mu
    var = jnp.mean(d * d, axis=-1, keepdims=True)
    return d * lax.rsqrt(var + EPS)


def _blockdiag(x, w_ref, scale=None):
    outs = []
    for j in range(w_ref.shape[0]):
        o = _dot(x[:, j * LANES:(j + 1) * LANES].astype(BF16), w_ref[j])
        outs.append(o if scale is None else o * scale)
    return jnp.concatenate(outs, axis=-1)


ROW_DMA_UNROLL = 8


def _row_copies(n_rows, row_copy):
    def start(r, carry):
        for k in range(TOP_K):
            row_copy(r, k).start(priority=k % 2)
        return carry

    def wait(r, carry):
        for k in range(TOP_K):
            row_copy(r, k).wait()
        return carry

    lax.fori_loop(0, n_rows, start, 0, unroll=ROW_DMA_UNROLL)
    lax.fori_loop(0, n_rows, wait, 0, unroll=ROW_DMA_UNROLL)


def _norm_kernel(*refs, combine, want_x, router):
    refs = list(refs)
    if combine:
        dest_ref, wts_in_ref, yb_hbm = refs[:3]
        refs = refs[3:]
    x_ref, g_ref = refs[:2]
    refs = refs[2:]
    if router:
        wr_ref, br_ref = refs[:2]
        refs = refs[2:]
    n_out = int(want_x) + 1 + (3 if router else 0)
    outs, scratch = refs[:n_out], refs[n_out:]
    x = x_ref[...]
    tm = x.shape[0]
    if combine:
        gbuf, gsem = scratch
        _row_copies(tm, lambda r, k: pltpu.make_async_copy(
            yb_hbm.at[pl.ds(dest_ref[0, TOP_K * r + k], 1), :], gbuf.at[k, pl.ds(r, 1), :], gsem))
        for k in range(TOP_K):
            y = jnp.concatenate([_unpack_half(gbuf[k], 0), _unpack_half(gbuf[k], 1)], axis=-1)
            x = x + wts_in_ref[:, k:k + 1] * y
    k = 0
    if want_x:
        outs[k][...] = x
        k += 1
    ms = jnp.mean(x * x, axis=-1, keepdims=True)
    h = x * lax.rsqrt(ms + EPS) * g_ref[...]
    outs[k][...] = _pack_halves(h) if router else h.astype(outs[k].dtype)
    k += 1
    if router:
        ids_ref, wts_ref, cnt_ref = outs[k], outs[k + 1], outs[k + 2]
        logits = _dot_hi(h, wr_ref[...]) + br_ref[...]
        lane = lax.broadcasted_iota(jnp.int32, logits.shape, 1)
        big = jnp.int32(1 << 20)
        is_g = lane < N_GROUPS
        gl = jnp.where(is_g, logits, NEG)
        gmax = jnp.max(gl, axis=-1, keepdims=True)
        gidx = jnp.min(jnp.where(gl == gmax, lane, big), axis=-1, keepdims=True)
        gsum = jnp.sum(jnp.where(is_g, jnp.exp(gl - gmax), 0.0), axis=-1, keepdims=True)
        gprob = 1.0 / gsum
        lo = N_GROUPS + EXP_PER_GROUP * gidx
        em = jnp.where(lane >= lo, jnp.where(lane < lo + EXP_PER_GROUP, logits, NEG), NEG)
        e1v = jnp.max(em, axis=-1, keepdims=True)
        e1i = jnp.min(jnp.where(em == e1v, lane, big), axis=-1, keepdims=True)
        em2 = jnp.where(lane == e1i, NEG, em)
        e2v = jnp.max(em2, axis=-1, keepdims=True)
        e2i = jnp.min(jnp.where(em2 == e2v, lane, big), axis=-1, keepdims=True)
        t = jnp.exp(e2v - e1v)
        w1 = gprob / (1.0 + t)
        w2 = gprob * t / (1.0 + t)
        e1, e2 = e1i - N_GROUPS, e2i - N_GROUPS
        chosen = jnp.where(lane == e1, 1.0, jnp.where(lane == e2, 1.0, 0.0))
        ti = lax.broadcasted_iota(jnp.int32, (tm, tm), 0)
        si = lax.broadcasted_iota(jnp.int32, (tm, tm), 1)
        before = _dot(jnp.where(ti > si, 1.0, 0.0).astype(BF16), chosen.astype(BF16))
        rank1 = jnp.sum(jnp.where(lane == e1, before, 0.0), axis=-1, keepdims=True).astype(jnp.int32)
        rank2 = jnp.sum(jnp.where(lane == e2, before, 0.0), axis=-1, keepdims=True).astype(jnp.int32)
        ids_ref[...] = jnp.where(lane == 0, e1, jnp.where(lane == 1, e2,
                                 jnp.where(lane == 2, rank1, jnp.where(lane == 3, rank2, 0))))
        wts_ref[...] = jnp.where(lane == 0, w1, jnp.where(lane == 1, w2, 0.0))
        cnt_ref[...] = jnp.sum(chosen, axis=0, keepdims=True)


def _norm_call(x, g, *, tm, grid, row_tile, want_x, h_shape, h_dtype, h_spec, combine=None, router=None, name):
    r, d = x.shape
    x_spec = pl.BlockSpec((tm, d), lambda *gi: (row_tile(*gi), 0))
    in_specs, args, scratch = [], [], []
    if combine is not None:
        dest, wts, yb = combine
        in_specs += [pl.BlockSpec((None, 1, TOP_K * tm), lambda *gi: (row_tile(*gi), 0, 0), memory_space=pltpu.SMEM),
                     pl.BlockSpec((tm, LANES), lambda *gi: (row_tile(*gi), 0)),
                     pl.BlockSpec(memory_space=pl.ANY)]
        args += [dest.reshape(r // tm, 1, TOP_K * tm), wts, yb]
        assert yb.dtype == U32 and yb.shape[1] * 2 == d
        scratch = [pltpu.VMEM((TOP_K, tm, d // 2), U32), pltpu.SemaphoreType.DMA(())]
    in_specs += [x_spec, pl.BlockSpec((1, d), lambda *gi: (0, 0))]
    args += [x, g.reshape(1, d)]
    if router is not None:
        in_specs += [pl.BlockSpec((d, LANES), lambda *gi: (0, 0)), pl.BlockSpec((1, LANES), lambda *gi: (0, 0))]
        args += list(router)
    out_shape, out_specs = [], []
    if want_x:
        out_shape.append(jax.ShapeDtypeStruct((r, d), F32))
        out_specs.append(x_spec)
    out_shape.append(jax.ShapeDtypeStruct(h_shape, h_dtype))
    out_specs.append(h_spec)
    if router is not None:
        out_shape += [jax.ShapeDtypeStruct((r, LANES), jnp.int32), jax.ShapeDtypeStruct((r, LANES), F32),
                      jax.ShapeDtypeStruct((r // tm, 1, LANES), F32)]
        out_specs += [pl.BlockSpec((tm, LANES), lambda *gi: (row_tile(*gi), 0))] * 2
        out_specs += [pl.BlockSpec((None, 1, LANES), lambda *gi: (row_tile(*gi), 0, 0))]
    return pl.pallas_call(
        functools.partial(_norm_kernel, combine=combine is not None, want_x=want_x, router=router is not None),
        grid=grid, in_specs=in_specs, out_specs=out_specs, out_shape=out_shape, scratch_shapes=scratch,
        compiler_params=_cparams(("arbitrary",) * len(grid)), name=name)(*args)


def _embed_norm_kernel(xp_ref, meta_ref, xs_ref, g_ref, x_ref, h_ref, *, n_seq, n_prompt_tiles):
    i = pl.program_id(0)
    c = i // n_seq
    tm, d = x_ref.shape

    def finish(x):
        x_ref[...] = x
        ms = jnp.mean(x * x, axis=-1, keepdims=True)
        h_ref[...] = (x * lax.rsqrt(ms + EPS) * g_ref[...]).astype(h_ref.dtype)

    @pl.when(i >= n_prompt_tiles)
    def _():
        finish(xs_ref[...])

    @pl.when(jnp.logical_and(i < n_prompt_tiles, c == 0))
    def _():
        finish(jnp.concatenate([jnp.zeros((tm - meta_ref.shape[0], d), F32), meta_ref[...]], axis=0))

    @pl.when(jnp.logical_and(i < n_prompt_tiles, c > 0))
    def _():
        finish(xp_ref[...])


def _embed_norm_call(x_prompt, meta, x_sample, g, nch, *, name):
    bp, seq, d = x_prompt.shape
    bs = x_sample.shape[0]
    npt = bp * nch
    assert bs % CHUNK == 0 and meta.shape[0] % SUBLANES == 0
    r = npt * CHUNK + bs
    row_spec = pl.BlockSpec((CHUNK, d), lambda i: (i, 0))
    return pl.pallas_call(
        functools.partial(_embed_norm_kernel, n_seq=bp, n_prompt_tiles=npt), grid=(r // CHUNK,),
        in_specs=[pl.BlockSpec((None, CHUNK, d),
                               lambda i: (lax.rem(i, bp), jnp.clip(i // bp - 1, 0, nch - 2), 0)),
                  pl.BlockSpec(meta.shape, lambda i: (0, 0)),
                  pl.BlockSpec((CHUNK, d), lambda i: (jnp.maximum(i - npt, 0), 0)),
                  pl.BlockSpec((1, d), lambda i: (0, 0))],
        out_specs=[row_spec, row_spec],
        out_shape=[jax.ShapeDtypeStruct((r, d), F32), jax.ShapeDtypeStruct((r, d), BF16)],
        compiler_params=_cparams(("arbitrary",)), name=name)(x_prompt, meta, x_sample, g.reshape(1, d))


def _zero_blocks(zero_ref, out_hbm, first, last, sem):
    def blk_copy(b):
        return pltpu.make_async_copy(zero_ref, out_hbm.at[pl.ds(pl.multiple_of(b * MOE_ROWS, MOE_ROWS), MOE_ROWS), :], sem)

    def start(b, carry):
        blk_copy(b).start()
        return carry

    def wait(b, carry):
        blk_copy(b).wait()
        return carry

    lax.fori_loop(first, last, start, 0)
    lax.fori_loop(first, last, wait, 0)


def _dispatch_kernel(pe_ref, dest_ref, h_ref, xs_out, zero_ref, sem, zsem):
    tm = h_ref.shape[0]
    n_blocks = xs_out.shape[0] // MOE_ROWS

    @pl.when(pl.program_id(0) == 0)
    def _():
        zero_ref[...] = jnp.zeros_like(zero_ref)

        def last_block(e):
            b = jnp.maximum(pe_ref[e] // MOE_ROWS - 1, 0)
            return pltpu.make_async_copy(
                zero_ref, xs_out.at[pl.ds(pl.multiple_of(b * MOE_ROWS, MOE_ROWS), MOE_ROWS), :], zsem)

        for e in range(N_EXPERTS):
            last_block(e).start()
        for e in range(N_EXPERTS):
            last_block(e).wait()
        _zero_blocks(zero_ref, xs_out, pe_ref[N_EXPERTS - 1] // MOE_ROWS, n_blocks, zsem)

    _row_copies(tm, lambda r, k: pltpu.make_async_copy(
        h_ref.at[pl.ds(r, 1), :], xs_out.at[pl.ds(dest_ref[0, TOP_K * r + k], 1), :], sem))


def _dispatch_call(h2, dest, pad_end, n_slots, *, tm, name):
    r, d = h2.shape
    grid_spec = pltpu.PrefetchScalarGridSpec(
        num_scalar_prefetch=1, grid=(r // tm,),
        in_specs=[pl.BlockSpec((None, 1, TOP_K * tm), lambda i, pe: (i, 0, 0), memory_space=pltpu.SMEM),
                  pl.BlockSpec((tm, d), lambda i, pe: (i, 0))],
        out_specs=pl.BlockSpec(memory_space=pl.ANY),
        scratch_shapes=[pltpu.VMEM((MOE_ROWS, d), h2.dtype), pltpu.SemaphoreType.DMA(()),
                        pltpu.SemaphoreType.DMA(())])
    return pl.pallas_call(
        _dispatch_kernel, grid_spec=grid_spec, out_shape=jax.ShapeDtypeStruct((n_slots, d), h2.dtype),
        compiler_params=_cparams(("arbitrary",)), name=name)(pad_end, dest.reshape(r // tm, 1, TOP_K * tm), h2)


def _mm_kernel(*refs, has_res):
    if has_res:
        x_ref, w_ref, r_ref, o_ref, wb_ref = refs
    else:
        x_ref, w_ref, o_ref, wb_ref = refs

    @pl.when(pl.program_id(1) == 0)
    def _():
        wb_ref[...] = w_ref[...].astype(BF16)

    acc = _dot(x_ref[...], wb_ref[...])
    if has_res:
        acc = acc + r_ref[...]
    o_ref[...] = acc.astype(o_ref.dtype)


def _mm_call(x, w, l, res=None, *, out_dtype=F32, name):
    r, k = x.shape
    n = w.shape[2]
    tm = _pick_tile(r, 1152, BF16_SUBLANES)
    tn = _pick_tile(n, 1024, LANES)
    in_specs = [pl.BlockSpec((tm, k), lambda j, i: (i, 0)), pl.BlockSpec((None, k, tn), lambda j, i: (l, 0, j))]
    args = [x, w]
    if res is not None:
        in_specs.append(pl.BlockSpec((tm, tn), lambda j, i: (i, j)))
        args.append(res)
    return pl.pallas_call(
        functools.partial(_mm_kernel, has_res=res is not None),
        grid=(n // tn, r // tm), in_specs=in_specs,
        out_specs=pl.BlockSpec((tm, tn), lambda j, i: (i, j)),
        out_shape=jax.ShapeDtypeStruct((r, n), out_dtype),
        scratch_shapes=[pltpu.VMEM((k, tn), BF16)],
        compiler_params=_cparams(("arbitrary", "arbitrary")), name=name)(*args)


def _merge_kernel(*refs, n_branch, n_prompt_tiles):
    yp_refs, ys_refs = refs[:n_branch], refs[n_branch:2 * n_branch]
    g_refs = refs[2 * n_branch:3 * n_branch]
    w_ref, o_ref, wb_ref = refs[3 * n_branch:]
    i = pl.program_id(1)

    @pl.when(i == 0)
    def _():
        wb_ref[...] = w_ref[...].astype(BF16)

    def run(y_refs):
        acc = _sigmoid(g_refs[0][...]) * _dot(y_refs[0][...], wb_ref[0])
        for n in range(1, n_branch):
            acc = acc + _sigmoid(g_refs[n][...]) * _dot(y_refs[n][...], wb_ref[n])
        o_ref[...] = acc.astype(o_ref.dtype)

    @pl.when(i < n_prompt_tiles)
    def _():
        run(yp_refs)

    @pl.when(i >= n_prompt_tiles)
    def _():
        run(ys_refs)


def _merge_call(y_prompt, y_sample, z, w_branch, l, gate_col0, *, name):
    rp, db = y_prompt[0].shape
    bs = y_sample[0].shape[0]
    _, nbr, _, d = w_branch.shape
    tn = _pick_tile(d, 1024, LANES)
    tm = _pick_tile(math.gcd(rp, bs), 384, BF16_SUBLANES)
    npt = rp // tm
    assert gate_col0 % tn == 0 and d % tn == 0
    g0 = gate_col0 // tn
    per = d // tn
    yp_spec = pl.BlockSpec((tm, db), lambda j, i: (jnp.minimum(i, npt - 1), 0))
    ys_spec = pl.BlockSpec((tm, db), lambda j, i: (jnp.maximum(i - npt, 0), 0))
    g_specs = [pl.BlockSpec((tm, tn), lambda j, i, n=n: (i, g0 + per * n + j)) for n in range(nbr)]
    return pl.pallas_call(
        functools.partial(_merge_kernel, n_branch=nbr, n_prompt_tiles=npt), grid=(d // tn, (rp + bs) // tm),
        in_specs=[yp_spec] * nbr + [ys_spec] * nbr + g_specs
        + [pl.BlockSpec((None, nbr, db, tn), lambda j, i: (l, 0, 0, j))],
        out_specs=pl.BlockSpec((tm, tn), lambda j, i: (i, j)),
        out_shape=jax.ShapeDtypeStruct((rp + bs, d), BF16),
        scratch_shapes=[pltpu.VMEM((nbr, db, tn), BF16)],
        compiler_params=_cparams(("arbitrary", "arbitrary")), name=name)(
            *y_prompt, *y_sample, *([z] * nbr), w_branch)


def _shift_rows(x, prev8, k, row):
    rolled = pltpu.roll(x, k, 0)
    head = jnp.where(row[:SUBLANES] < k, pltpu.roll(prev8, k, 0), rolled[:SUBLANES])
    return jnp.concatenate([head, rolled[SUBLANES:]], axis=0)


def _chunk_conv(x, prev8, w_ref, b_ref, row):
    y = b_ref[...] + w_ref[CONV_W - 1:CONV_W, :] * x
    for k in range(1, CONV_W):
        y = y + w_ref[CONV_W - 1 - k:CONV_W - k, :] * _shift_rows(x, prev8, k, row)
    return y


def _rglru_coeffs(xc, wa_ref, ba_ref, wx_ref, bx_ref, lam_ref):
    r = _sigmoid(_blockdiag(xc, wa_ref) + ba_ref[...])
    i = _sigmoid(_blockdiag(xc, wx_ref) + bx_ref[...])
    log_a = -RG_C * r * _softplus(-lam_ref[...])
    a = jnp.exp(log_a)
    mult = jnp.sqrt(1.0 - a * a)
    return a, mult * i * xc


def _per_sequence(body, nseq, n_rows_in, n_const, n_rows_out, n_state_out):
    def kernel(*refs):
        rows_in = refs[:n_rows_in]
        consts = refs[n_rows_in:n_rows_in + n_const]
        o = n_rows_in + n_const
        rows_out = refs[o:o + n_rows_out]
        state_out = refs[o + n_rows_out:o + n_rows_out + n_state_out]
        scratch = refs[o + n_rows_out + n_state_out:]
        L = rows_in[0].shape[0] // nseq
        for b in range(nseq):
            rows = lambda r, b=b: r.at[pl.ds(b * L, L), :]
            body(*[rows(r) for r in rows_in], *consts, *[rows(r) for r in rows_out],
                 *[r.at[pl.ds(b, 1)] for r in state_out], *[r.at[b] for r in scratch])
    return kernel


def _pa_kernel(x_ref, cw_ref, cb_ref, wa_ref, ba_ref, wx_ref, bx_ref, lam_ref,
               y_ref, hl_ref, cv_ref, prev_ref, h_ref, *, pad):
    c = pl.program_id(0)
    last = pl.num_programs(0) - 1
    L = x_ref.shape[0]
    row = lax.broadcasted_iota(jnp.int32, (L, 1), 0)
    valid = jnp.logical_or(c > 0, row >= pad)

    @pl.when(c == 0)
    def _():
        prev_ref[...] = jnp.zeros_like(prev_ref)
        h_ref[...] = jnp.zeros_like(h_ref)

    x = jnp.where(valid, x_ref[...], 0.0)
    xc = _chunk_conv(x, prev_ref[...], cw_ref, cb_ref, row)
    prev_ref[...] = x[L - SUBLANES:]
    a, b = _rglru_coeffs(xc, wa_ref, ba_ref, wx_ref, bx_ref, lam_ref)
    a = jnp.where(valid, a, 1.0)
    b = jnp.where(valid, b, 0.0)
    hs = []
    for j in range(x.shape[1] // LANES):
        sl = slice(j * LANES, (j + 1) * LANES)
        aj, bj = a[:, sl], b[:, sl]
        d = 1
        while d < L:
            keep = row >= d
            a_s = jnp.where(keep, pltpu.roll(aj, d, 0), 1.0)
            b_s = jnp.where(keep, pltpu.roll(bj, d, 0), 0.0)
            bj = aj * b_s + bj
            aj = aj * a_s
            d *= 2
        hs.append(aj * h_ref[0:1, sl] + bj)
    h = jnp.concatenate(hs, axis=-1)
    h_ref[0:1, :] = h[L - 1:L]
    y_ref[...] = h.astype(y_ref.dtype)

    @pl.when(c == last)
    def _():
        hl_ref[0] = h[L - 1:L]
        cv_ref[0] = x[L - (CONV_W - 1):]


def _pa_call(z, nseq, nch, pad, p, *, name):
    rows = nseq * nch * CHUNK
    c = p['rg_conv_w'].shape[1]
    full = lambda a: pl.BlockSpec(a.shape, lambda k: (0,) * a.ndim)
    whole = lambda *shape: pl.BlockSpec(shape, lambda k: (0,) * len(shape))
    params = [p['rg_conv_w'], p['rg_conv_b'], p['rg_wa'], p['rg_ba'], p['rg_wx'], p['rg_bx'], p['rg_lambda']]
    return pl.pallas_call(
        _per_sequence(functools.partial(_pa_kernel, pad=pad), nseq, 1, len(params), 1, 2), grid=(nch,),
        in_specs=[pl.BlockSpec((nseq * CHUNK, c), lambda k: (k, 0))] + [full(a) for a in params],
        out_specs=[pl.BlockSpec((nseq * CHUNK, c), lambda k: (k, 0)),
                   whole(nseq, 1, c), whole(nseq, CONV_W - 1, c)],
        out_shape=[jax.ShapeDtypeStruct((rows, c), BF16),
                   jax.ShapeDtypeStruct((nseq, 1, c), F32),
                   jax.ShapeDtypeStruct((nseq, CONV_W - 1, c), F32)],
        scratch_shapes=[pltpu.VMEM((nseq, SUBLANES, c), F32), pltpu.VMEM((nseq, SUBLANES, c), F32)],
        compiler_params=_cparams(("arbitrary",)), name=name)(z, *params)


def _step_conv(x, cs_ref, w_ref, b_ref):
    y = b_ref[...] + w_ref[CONV_W - 1:CONV_W, :] * x
    for k in range(CONV_W - 1):
        y = y + w_ref[k:k + 1, :] * cs_ref[k]
    return y


def _step_conv_state(x, cs_ref, out_ref):
    for k in range(CONV_W - 2):
        out_ref[k] = cs_ref[k + 1]
    out_ref[CONV_W - 2] = x


def _sa_kernel(x_ref, cs_ref, h0_ref, cw_ref, cb_ref, wa_ref, ba_ref, wx_ref, bx_ref, lam_ref,
               y_ref, hn_ref, cv_ref):
    x = x_ref[...]
    xc = _step_conv(x, cs_ref, cw_ref, cb_ref)
    a, b = _rglru_coeffs(xc, wa_ref, ba_ref, wx_ref, bx_ref, lam_ref)
    h = a * h0_ref[...] + b
    hn_ref[...] = h
    y_ref[...] = h.astype(y_ref.dtype)
    _step_conv_state(x, cs_ref, cv_ref)


def _sa_call(z, row0, bs, cs, h0, p, *, name):
    c = h0.shape[1]
    assert row0 % bs == 0
    full = lambda a: pl.BlockSpec(a.shape, lambda i: (0,) * a.ndim)
    params = [p['rg_conv_w'], p['rg_conv_b'], p['rg_wa'], p['rg_ba'], p['rg_wx'], p['rg_bx'], p['rg_lambda']]
    return pl.pallas_call(
        _sa_kernel, grid=(1,),
        in_specs=[pl.BlockSpec((bs, c), lambda i: (row0 // bs, 0)), full(cs), full(h0)] + [full(a) for a in params],
        out_specs=[pl.BlockSpec((bs, c), lambda i: (0, 0)), pl.BlockSpec((bs, c), lambda i: (0, 0)),
                   pl.BlockSpec(cs.shape, lambda i: (0, 0, 0))],
        out_shape=[jax.ShapeDtypeStruct((bs, c), BF16), jax.ShapeDtypeStruct((bs, c), F32),
                   jax.ShapeDtypeStruct(cs.shape, F32)],
        compiler_params=_cparams(("arbitrary",)), name=name)(z, cs, h0, *params)


def _ret_log_g(h):
    return math.log1p(-(2.0 ** (-5.0 - h)))


def _rope(x, cos2, sin2):
    return x * cos2 + pltpu.roll(x, x.shape[1] // 2, 1) * sin2


def _pb_kernel(q_ref, k_ref, v_ref, g_ref, cos_ref, sin_ref, gn_ref, y_ref, so_ref, s_ref, *, pad, heads):
    c = pl.program_id(0)
    last = pl.num_programs(0) - 1
    L = q_ref.shape[0]
    dk = q_ref.shape[1] // heads
    dv = v_ref.shape[1] // heads
    row = lax.broadcasted_iota(jnp.int32, (L, 1), 0)
    valid = jnp.logical_or(c > 0, row >= pad)

    @pl.when(c == 0)
    def _():
        s_ref[...] = jnp.zeros_like(s_ref)

    t = row.astype(F32)
    rel = t - lax.broadcasted_iota(jnp.int32, (1, L), 1).astype(F32)
    cos2, sin2 = cos_ref[...], sin_ref[...]
    for h in range(heads):
        lg = _ret_log_g(h)
        qh = jnp.where(valid, _rope(q_ref[:, h * dk:(h + 1) * dk], cos2, sin2), 0.0)
        kh = jnp.where(valid, _rope(k_ref[:, h * dk:(h + 1) * dk], cos2, sin2), 0.0) * (dk ** -0.5)
        vh = jnp.where(valid, v_ref[:, h * dv:(h + 1) * dv], 0.0)
        decay = jnp.where(rel >= 0, jnp.exp(jnp.maximum(rel, 0.0) * lg), 0.0)
        qb, vb = qh.astype(BF16), vh.astype(BF16)
        scores = _dot_nt(qb, kh.astype(BF16)) * decay
        s_old = s_ref[h]
        o = _dot(scores.astype(BF16), vb) + _dot(qb, s_old.astype(BF16)) * jnp.exp((t + 1.0) * lg)
        k_dec = kh * jnp.exp((L - 1.0 - t) * lg)
        s_ref[h] = math.exp(L * lg) * s_old + _dot_tn(k_dec.astype(BF16), vb)
        sl = slice(h * dv, (h + 1) * dv)
        y_ref[:, sl] = (_head_norm(o) * gn_ref[:, sl] * _silu(g_ref[:, sl])).astype(y_ref.dtype)

    @pl.when(c == last)
    def _():
        so_ref[0] = s_ref[...]


def _pb_call(z, nseq, nch, pad, heads, dk, dv, cos2, sin2, ret_norm, *, name):
    rows = nseq * nch * CHUNK
    hk, hv = heads * dk, heads * dv
    assert hv % hk == 0
    q0 = hv // hk
    blk = nseq * CHUNK
    return pl.pallas_call(
        _per_sequence(functools.partial(_pb_kernel, pad=pad, heads=heads), nseq, 4, 3, 1, 1), grid=(nch,),
        in_specs=[pl.BlockSpec((blk, hk), lambda k: (k, q0)),
                  pl.BlockSpec((blk, hk), lambda k: (k, q0 + 1)),
                  pl.BlockSpec((blk, hv), lambda k: (k, 2)),
                  pl.BlockSpec((blk, hv), lambda k: (k, 3)),
                  pl.BlockSpec((CHUNK, dk), lambda k: (k, 0)),
                  pl.BlockSpec((CHUNK, dk), lambda k: (k, 0)),
                  pl.BlockSpec((1, hv), lambda k: (0, 0))],
        out_specs=[pl.BlockSpec((blk, hv), lambda k: (k, 0)),
                   pl.BlockSpec((nseq, heads, dk, dv), lambda k: (0, 0, 0, 0))],
        out_shape=[jax.ShapeDtypeStruct((rows, hv), BF16),
                   jax.ShapeDtypeStruct((nseq, heads, dk, dv), F32)],
        scratch_shapes=[pltpu.VMEM((nseq, heads, dk, dv), F32)],
        compiler_params=_cparams(("arbitrary",)), name=name)(z, z, z, z, cos2, sin2, ret_norm)


def _first_pass_only(body, state_out_ref):
    @pl.when(pl.program_id(0) == 0)
    def _():
        body()

    @pl.when(pl.program_id(0) > 0)
    def _():
        state_out_ref[...] = jnp.zeros_like(state_out_ref)


def _pass_maps(l, prev, nblk, depth):
    if prev is not None:
        return 1, (lambda g, i: i), (lambda g, i: l)
    assert l == 0
    return depth, (lambda g, i: jnp.where(g == 0, i, nblk - 1)), (lambda g, i: g)


def _sb_kernel(*refs, heads):
    _first_pass_only(functools.partial(_sb_body, *refs, heads=heads), refs[-1])


def _sb_body(q_ref, k_ref, v_ref, g_ref, s_ref, cos_ref, sin_ref, gn_ref, y_ref, so_ref, *, heads):
    nb = q_ref.shape[0]
    dk = q_ref.shape[1] // heads
    dv = v_ref.shape[1] // heads
    row = lax.broadcasted_iota(jnp.int32, (nb, 1), 0)
    cos2, sin2 = cos_ref[...], sin_ref[...]
    for h in range(heads):
        g = math.exp(_ret_log_g(h))
        qh = _rope(q_ref[:, h * dk:(h + 1) * dk], cos2, sin2)
        kh = _rope(k_ref[:, h * dk:(h + 1) * dk], cos2, sin2) * (dk ** -0.5)
        vh = v_ref[:, h * dv:(h + 1) * dv]
        qb, vb = qh.astype(BF16), vh.astype(BF16)
        cross = jnp.zeros((nb, dv), F32)
        for i in range(nb):
            s_old = s_ref[i, h]
            cross = jnp.where(row == i, _dot(qb, s_old.astype(BF16)), cross)
            so_ref[i, h] = g * s_old + _dot_tn(jnp.where(row == i, kh, 0.0), vh)
        qk = jnp.sum(qb.astype(F32) * kh.astype(BF16).astype(F32), axis=-1, keepdims=True)
        o = qk * vb.astype(F32) + g * cross
        sl = slice(h * dv, (h + 1) * dv)
        y_ref[:, sl] = (_head_norm(o) * gn_ref[:, sl] * _silu(g_ref[:, sl])).astype(y_ref.dtype)


def _sb_call(z, row0, bs, heads, dk, dv, state, l, prev, cos2, sin2, ret_norm, *, name):
    hk, hv = heads * dk, heads * dv
    nb = SUBLANES
    assert bs % nb == 0 and row0 % nb == 0
    r0 = row0 // nb
    q0 = hv // hk
    passes, blk, lay = _pass_maps(l, prev, bs // nb, state.shape[0])
    return pl.pallas_call(
        _skip_ref(functools.partial(_sb_kernel, heads=heads), 8 if prev is not None else None),
        grid=(passes, bs // nb),
        in_specs=[pl.BlockSpec((nb, hk), lambda g, i: (r0 + blk(g, i), q0)),
                  pl.BlockSpec((nb, hk), lambda g, i: (r0 + blk(g, i), q0 + 1)),
                  pl.BlockSpec((nb, hv), lambda g, i: (r0 + blk(g, i), 2)),
                  pl.BlockSpec((nb, hv), lambda g, i: (r0 + blk(g, i), 3)),
                  pl.BlockSpec((None, nb, heads, dk, dv), lambda g, i: (l, blk(g, i), 0, 0, 0)),
                  pl.BlockSpec((1, dk), lambda g, i: (0, 0)),
                  pl.BlockSpec((1, dk), lambda g, i: (0, 0)),
                  pl.BlockSpec((1, hv), lambda g, i: (0, 0))] + _prev_specs(prev),
        out_specs=[pl.BlockSpec((nb, hv), lambda g, i: (blk(g, i), 0)),
                   pl.BlockSpec((None, nb, heads, dk, dv), lambda g, i: (lay(g, i), i, 0, 0, 0))],
        out_shape=[jax.ShapeDtypeStruct((bs, hv), BF16), jax.ShapeDtypeStruct(state.shape, F32)],
        input_output_aliases={8: 1} if prev is not None else {},
        compiler_params=_cparams(("arbitrary", "arbitrary")), name=name)(
            z, z, z, z, state, cos2, sin2, ret_norm, *([] if prev is None else [prev]))


def _mlstm_qkv_gates(x, xc, wq_ref, wk_ref, wv_ref, wif_ref, bif_ref, dh):
    q = _blockdiag(xc, wq_ref)
    k = _blockdiag(xc, wk_ref, scale=dh ** -0.5)
    v = _blockdiag(x, wv_ref)
    c = x.shape[1]
    gates = (_dot(q.astype(BF16), wif_ref[0:c, :]) + _dot(k.astype(BF16), wif_ref[c:2 * c, :])
             + _dot(v.astype(BF16), wif_ref[2 * c:3 * c, :]) + bif_ref[...])
    return q, k, v, gates


def _pc_kernel(x_ref, o_ref, cw_ref, cb_ref, wq_ref, wk_ref, wv_ref, wif_ref, bif_ref, gn_ref, sk_ref,
               y_ref, co_ref, no_ref, mo_ref, cv_ref, prev_ref, c_ref, n_ref, m_ref, *, pad, heads):
    c = pl.program_id(0)
    last = pl.num_programs(0) - 1
    L = x_ref.shape[0]
    dh = x_ref.shape[1] // heads
    row = lax.broadcasted_iota(jnp.int32, (L, 1), 0)
    valid = jnp.logical_or(c > 0, row >= pad)

    @pl.when(c == 0)
    def _():
        prev_ref[...] = jnp.zeros_like(prev_ref)
        c_ref[...] = jnp.zeros_like(c_ref)
        n_ref[...] = jnp.zeros_like(n_ref)
        m_ref[...] = jnp.zeros_like(m_ref)

    x = jnp.where(valid, x_ref[...], 0.0)
    xc = _silu(_chunk_conv(x, prev_ref[...], cw_ref, cb_ref, row))
    prev_ref[...] = x[L - SUBLANES:]
    q, k, v, gates = _mlstm_qkv_gates(x, xc, wq_ref, wk_ref, wv_ref, wif_ref, bif_ref, dh)
    lane = lax.broadcasted_iota(jnp.int32, gates.shape, 1)
    is_i = lane < heads
    ig_c = jnp.where(jnp.logical_and(valid, is_i), gates, NEG)
    lf_c = jnp.where(valid, _log_sigmoid(gates), 0.0)
    ig_r = ig_c.T
    lf_r = lf_c.T
    ti = lax.broadcasted_iota(jnp.int32, (L, L), 0)
    si = lax.broadcasted_iota(jnp.int32, (L, L), 1)
    causal = ti >= si
    ones_lower = jnp.where(causal, 1.0, 0.0).astype(BF16)
    ones_upper = jnp.where(si >= ti, 1.0, 0.0).astype(BF16)
    b_c = sum(_dot(ones_lower, part) for part in _split3(lf_c))
    b_r = sum(_dot(part, ones_upper) for part in _split3(lf_r))
    m_all = m_ref[...]
    m_lane = lax.broadcasted_iota(jnp.int32, m_all.shape, 1)
    m_new = jnp.zeros_like(m_all)
    for h in range(heads):
        sl = slice(h * dh, (h + 1) * dh)
        bc = b_c[:, heads + h:heads + h + 1]
        br = b_r[heads + h:heads + h + 1, :]
        igr = ig_r[h:h + 1, :]
        igc = ig_c[:, h:h + 1]
        m_prev = m_all[0:1, h:h + 1]
        log_d = jnp.where(causal, bc - br + igr, NEG)
        log_inter = bc + m_prev
        m_t = jnp.maximum(log_inter, jnp.max(log_d, axis=-1, keepdims=True))
        d_m = jnp.exp(log_d - m_t)
        w_inter = jnp.exp(log_inter - m_t)
        qh, kh, vh = q[:, sl], k[:, sl], v[:, sl]
        qb, kb, vb = qh.astype(BF16), kh.astype(BF16), vh.astype(BF16)
        scores = _dot_nt(qb, kb) * d_m
        c_old = c_ref[h]
        n_old = n_ref[h:h + 1, :]
        num = _dot(scores.astype(BF16), vb) + w_inter * _dot(qb, c_old.astype(BF16))
        den = jnp.sum(scores, axis=-1, keepdims=True) + w_inter * jnp.sum(qh * n_old, axis=-1, keepdims=True)
        hh = num / jnp.maximum(jnp.abs(den), jnp.exp(-m_t))
        m_end = m_t[L - 1:L]
        b_last = bc[L - 1:L]
        w_c = jnp.exp(b_last - bc + igc - m_end)
        decay_c = jnp.exp(b_last + m_prev - m_end)
        kw = kh * w_c
        c_ref[h] = decay_c * c_old + _dot_tn(kw.astype(BF16), vb)
        n_ref[h:h + 1, :] = decay_c * n_old + jnp.sum(kw, axis=0, keepdims=True)
        m_new = jnp.where(m_lane == h, m_end, m_new)
        hm = _sigmoid(o_ref[:, sl]) * hh
        y_ref[:, sl] = (_head_norm(hm) * gn_ref[:, sl] + sk_ref[:, sl] * xc[:, sl]).astype(y_ref.dtype)
    m_ref[...] = m_new

    @pl.when(c == last)
    def _():
        co_ref[0] = c_ref[...]
        no_ref[0] = n_ref[...]
        mo_ref[0] = m_ref[...]
        cv_ref[0] = x[L - (CONV_W - 1):]


def _pc_call(z, nseq, nch, pad, heads, xcol, p, *, name):
    rows = nseq * nch * CHUNK
    c = p['m_conv_w'].shape[1]
    dh = c // heads
    full = lambda a: pl.BlockSpec(a.shape, lambda k: (0,) * a.ndim)
    whole = lambda *shape: pl.BlockSpec(shape, lambda k: (0,) * len(shape))
    params = [p['m_conv_w'], p['m_conv_b'], p['m_wq'], p['m_wk'], p['m_wv'], p['m_w_if'], p['m_b_if'],
              p['m_norm'], p['m_skip']]
    blk = nseq * CHUNK
    return pl.pallas_call(
        _per_sequence(functools.partial(_pc_kernel, pad=pad, heads=heads), nseq, 2, len(params), 1, 4),
        grid=(nch,),
        in_specs=[pl.BlockSpec((blk, c), lambda k: (k, xcol)),
                  pl.BlockSpec((blk, c), lambda k: (k, xcol + 1))] + [full(a) for a in params],
        out_specs=[pl.BlockSpec((blk, c), lambda k: (k, 0)),
                   whole(nseq, heads, dh, dh), whole(nseq, SUBLANES, dh), whole(nseq, SUBLANES, LANES),
                   whole(nseq, CONV_W - 1, c)],
        out_shape=[jax.ShapeDtypeStruct((rows, c), BF16),
                   jax.ShapeDtypeStruct((nseq, heads, dh, dh), F32),
                   jax.ShapeDtypeStruct((nseq, SUBLANES, dh), F32),
                   jax.ShapeDtypeStruct((nseq, SUBLANES, LANES), F32),
                   jax.ShapeDtypeStruct((nseq, CONV_W - 1, c), F32)],
        scratch_shapes=[pltpu.VMEM((nseq, SUBLANES, c), F32), pltpu.VMEM((nseq, heads, dh, dh), F32),
                        pltpu.VMEM((nseq, SUBLANES, dh), F32), pltpu.VMEM((nseq, SUBLANES, LANES), F32)],
        compiler_params=_cparams(("arbitrary",)), name=name)(z, z, *params)


def _sc_kernel(*refs, heads):
    _first_pass_only(functools.partial(_sc_body, *refs, heads=heads), refs[-4])


def _sc_body(x_ref, o_ref, cs_ref, c_ref, n_ref, m_ref, cw_ref, cb_ref, wq_ref, wk_ref, wv_ref, wif_ref, bif_ref,
             gn_ref, sk_ref, y_ref, co_ref, no_ref, mo_ref, cv_ref, *, heads):
    nb = x_ref.shape[0]
    dh = x_ref.shape[1] // heads
    row = lax.broadcasted_iota(jnp.int32, (nb, 1), 0)
    x = x_ref[...]
    xc = _silu(_step_conv(x, cs_ref, cw_ref, cb_ref))
    q, k, v, gates = _mlstm_qkv_gates(x, xc, wq_ref, wk_ref, wv_ref, wif_ref, bif_ref, dh)
    lf = _log_sigmoid(gates)
    m_old = m_ref[...]
    lane = lax.broadcasted_iota(jnp.int32, m_old.shape, 1)
    m_new = jnp.zeros_like(m_old)
    for h in range(heads):
        sl = slice(h * dh, (h + 1) * dh)
        ig = gates[:, h:h + 1]
        log_inter = lf[:, heads + h:heads + h + 1] + m_old[:, h:h + 1]
        m_t = jnp.maximum(log_inter, ig)
        d_m = jnp.exp(ig - m_t)
        w_inter = jnp.exp(log_inter - m_t)
        qh, kh, vh = q[:, sl], k[:, sl], v[:, sl]
        qb, kb, vb = qh.astype(BF16), kh.astype(BF16), vh.astype(BF16)
        sc = jnp.sum(qb.astype(F32) * kb.astype(F32), axis=-1, keepdims=True) * d_m
        n_old = n_ref[:, sl]
        kw = kh * d_m
        qc = jnp.zeros((nb, dh), F32)
        for i in range(nb):
            c_old = c_ref[i, h]
            qc = jnp.where(row == i, _dot(qb, c_old.astype(BF16)), qc)
            co_ref[i, h] = (w_inter[i:i + 1] * c_old
                            + _dot_tn(jnp.where(row == i, kw, 0.0), vh))
        num = sc * vb.astype(F32) + w_inter * qc
        den = sc + w_inter * jnp.sum(qh * n_old, axis=-1, keepdims=True)
        hh = num / jnp.maximum(jnp.abs(den), jnp.exp(-m_t))
        no_ref[:, sl] = w_inter * n_old + kw
        m_new = jnp.where(lane == h, m_t, m_new)
        hm = _sigmoid(o_ref[:, sl]) * hh
        y_ref[:, sl] = (_head_norm(hm) * gn_ref[:, sl] + sk_ref[:, sl] * xc[:, sl]).astype(y_ref.dtype)
    mo_ref[...] = m_new
    _step_conv_state(x, cs_ref, cv_ref)


def _sc_call(z, row0, bs, heads, xcol, cs, c_state, l, prev, n_state, m_state, p, *, name):
    c = p['m_conv_w'].shape[1]
    dh = c // heads
    nb = SUBLANES
    assert bs % nb == 0 and row0 % nb == 0
    r0 = row0 // nb
    full = lambda a: pl.BlockSpec(a.shape, lambda g, i: (0,) * a.ndim)
    params = [p['m_conv_w'], p['m_conv_b'], p['m_wq'], p['m_wk'], p['m_wv'], p['m_w_if'], p['m_b_if'],
              p['m_norm'], p['m_skip']]
    passes, blk, lay = _pass_maps(l, prev, bs // nb, c_state.shape[0])
    cs_spec = pl.BlockSpec((CONV_W - 1, nb, c), lambda g, i: (0, blk(g, i), 0))
    row_spec = lambda w: pl.BlockSpec((nb, w), lambda g, i: (blk(g, i), 0))
    return pl.pallas_call(
        _skip_ref(functools.partial(_sc_kernel, heads=heads), 6 + len(params) if prev is not None else None),
        grid=(passes, bs // nb),
        in_specs=[pl.BlockSpec((nb, c), lambda g, i: (r0 + blk(g, i), xcol)),
                  pl.BlockSpec((nb, c), lambda g, i: (r0 + blk(g, i), xcol + 1)),
                  cs_spec,
                  pl.BlockSpec((None, nb, heads, dh, dh), lambda g, i: (l, blk(g, i), 0, 0, 0)),
                  row_spec(c), row_spec(LANES)] + [full(a) for a in params] + _prev_specs(prev),
        out_specs=[row_spec(c),
                   pl.BlockSpec((None, nb, heads, dh, dh), lambda g, i: (lay(g, i), i, 0, 0, 0)),
                   row_spec(c), row_spec(LANES), cs_spec],
        out_shape=[jax.ShapeDtypeStruct((bs, c), BF16), jax.ShapeDtypeStruct(c_state.shape, F32),
                   jax.ShapeDtypeStruct((bs, c), F32), jax.ShapeDtypeStruct((bs, LANES), F32),
                   jax.ShapeDtypeStruct(cs.shape, F32)],
        input_output_aliases={6 + len(params): 1} if prev is not None else {},
        compiler_params=_cparams(("arbitrary", "arbitrary")), name=name)(
            z, z, cs, c_state, n_state, m_state, *params, *([] if prev is None else [prev]))


def _ffn_kernel(sbe_ref, sbs_ref, sbn_ref, nv_ref, xs_hbm, wg_ref, wu_ref, wd_ref, ys_hbm,
                acc_ref, xp_ref, yp_ref, h_ref, wgb_ref, wub_ref, wdb_ref, zero_ref, isem, osem, *, nka, nkb):
    del sbe_ref
    j, s = pl.program_id(0), pl.program_id(1)
    nblk = sbn_ref[j]
    blk0 = sbs_ref[j]
    max_blk = acc_ref.shape[0] // MOE_ROWS
    f = wgb_ref.shape[1]
    dn = wdb_ref.shape[0]

    @pl.when(jnp.logical_and(j == 0, s == 0))
    def _():
        zero_ref[...] = jnp.zeros_like(zero_ref)
        xp_ref[...] = jnp.zeros_like(xp_ref)

    def hbm_rows(ref, p):
        return ref.at[pl.ds(pl.multiple_of((blk0 + p) * MOE_ROWS, MOE_ROWS), MOE_ROWS), :]

    def vmem_rows(ref, p):
        return ref.at[pl.ds(p * MOE_ROWS, MOE_ROWS), :]

    def for_blocks(fn, n):
        for p in range(max_blk):
            @pl.when(p < n)
            def _():
                fn(p)

    def block_loop(body):
        per = FFN_ROWS // MOE_ROWS

        def step(b, carry):
            body(pl.ds(pl.multiple_of(b * FFN_ROWS, FFN_ROWS), FFN_ROWS))
            return carry
        lax.fori_loop(0, (nblk + per - 1) // per, step, 0)

    @pl.when(nblk > 0)
    def _():
        @pl.when(s == 0)
        def _():
            x_copy = lambda p: pltpu.make_async_copy(hbm_rows(xs_hbm, p), vmem_rows(xp_ref, p), isem)
            for_blocks(lambda p: x_copy(p).start(), nblk)
            for_blocks(lambda p: x_copy(p).wait(), nblk)

        for c in range(nka):
            @pl.when(s == c)
            def _(c=c):
                wgb_ref[...] = wg_ref[...].astype(BF16)
                wub_ref[...] = wu_ref[...].astype(BF16)

                def body(rows):
                    x = _unpack_half(xp_ref[rows, :], c).astype(BF16)
                    g = _dot(x, wgb_ref[...])
                    u = _dot(x, wub_ref[...])
                    if c > 0:
                        g = g + acc_ref[rows, 0:f]
                        u = u + acc_ref[rows, f:2 * f]
                    if c < nka - 1:
                        acc_ref[rows, 0:f] = g
                        acc_ref[rows, f:2 * f] = u
                    else:
                        h_ref[rows, :] = (_silu(g) * u).astype(BF16)
                block_loop(body)

        for c in range(nkb):
            @pl.when(s == nka + c)
            def _(c=c):
                wdb_ref[...] = wd_ref[...].astype(BF16)

                def body(rows):
                    y = _dot(h_ref[rows, c * dn:(c + 1) * dn], wdb_ref[...])
                    if c > 0:
                        y = y + acc_ref[rows, :]
                    acc_ref[rows, :] = y
                block_loop(body)

                if c == nkb - 1:
                    def pack_block(p):
                        sl = slice(p * MOE_ROWS, (p + 1) * MOE_ROWS)
                        yp_ref[sl, :] = _pack_halves(acc_ref[sl, :])
                    for_blocks(pack_block, nblk)
                    y_copy = lambda p: pltpu.make_async_copy(vmem_rows(yp_ref, p), hbm_rows(ys_hbm, p), osem)
                    for_blocks(lambda p: y_copy(p).start(), nblk)
                    for_blocks(lambda p: y_copy(p).wait(), nblk)

    @pl.when(jnp.logical_and(j == pl.num_programs(0) - 1, s == pl.num_programs(1) - 1))
    def _():
        _zero_blocks(zero_ref, ys_hbm, nv_ref[0], ys_hbm.shape[0] // MOE_ROWS, osem)


def _ffn_call(xs, tables, wg, wu, wd, l, *, name):
    sb_e, sb_blk0, sb_nblk, n_valid = tables
    ns, dp = xs.shape
    d, f = wg.shape[2], wg.shape[3]
    nsb = sb_e.shape[0]
    nka, nkb = d // FFN_GU_CHUNK, f // FFN_DN_CHUNK
    assert xs.dtype == U32 and dp * 2 == d and nka == 2 and FFN_GU_CHUNK == dp
    assert f % FFN_DN_CHUNK == 0 and 2 * f <= d and MOE_SB_ROWS % FFN_ROWS == 0 and FFN_ROWS % MOE_ROWS == 0

    def gu_map(j, s, sbe, sbs, sbn, nv):
        return (l, sbe[j], jnp.where(j < nv[1], jnp.minimum(s, nka - 1), nka - 1), 0)

    def dn_map(j, s, sbe, sbs, sbn, nv):
        prev = jnp.maximum(j - 1, 0)
        e = jnp.where(s >= nka, sbe[j], sbe[prev])
        c = jnp.where(j < nv[1], jnp.where(s >= nka, s - nka, jnp.where(j == 0, 0, nkb - 1)), nkb - 1)
        return (l, e, c, 0)

    grid_spec = pltpu.PrefetchScalarGridSpec(
        num_scalar_prefetch=4, grid=(nsb, nka + nkb),
        in_specs=[pl.BlockSpec(memory_space=pl.ANY),
                  pl.BlockSpec((None, None, FFN_GU_CHUNK, f), gu_map),
                  pl.BlockSpec((None, None, FFN_GU_CHUNK, f), gu_map),
                  pl.BlockSpec((None, None, FFN_DN_CHUNK, d), dn_map)],
        out_specs=pl.BlockSpec(memory_space=pl.ANY),
        scratch_shapes=[pltpu.VMEM((MOE_SB_ROWS, d), F32), pltpu.VMEM((MOE_SB_ROWS, dp), U32),
                        pltpu.VMEM((MOE_SB_ROWS, dp), U32), pltpu.VMEM((MOE_SB_ROWS, f), BF16),
                        pltpu.VMEM((FFN_GU_CHUNK, f), BF16), pltpu.VMEM((FFN_GU_CHUNK, f), BF16),
                        pltpu.VMEM((FFN_DN_CHUNK, d), BF16), pltpu.VMEM((MOE_ROWS, dp), U32),
                        pltpu.SemaphoreType.DMA(()), pltpu.SemaphoreType.DMA(())])
    return pl.pallas_call(
        functools.partial(_ffn_kernel, nka=nka, nkb=nkb), grid_spec=grid_spec,
        out_shape=jax.ShapeDtypeStruct((ns, dp), U32),
        compiler_params=_cparams(("arbitrary", "arbitrary")), name=name)(
            sb_e, sb_blk0, sb_nblk, n_valid, xs, wg, wu, wd)


def _take(table, idx):
    return jnp.sum(jnp.where(idx[:, None] == jnp.arange(table.shape[0])[None, :], table[None, :], 0), axis=1)


def _route_tables(ids, tile_cnt, r, tm):
    eidx = jnp.arange(N_EXPERTS, dtype=jnp.int32)
    cnt_t = tile_cnt[:, 0, :N_EXPERTS].astype(jnp.int32)
    counts = jnp.sum(cnt_t, axis=0)
    tile_off = jnp.cumsum(cnt_t, axis=0) - cnt_t
    padded = (counts + MOE_ROWS - 1) // MOE_ROWS * MOE_ROWS
    pad_end = jnp.cumsum(padded)
    pad_start = pad_end - padded
    base = jnp.repeat(pad_start[None, :] + tile_off, tm, axis=0)
    e = ids[:, :TOP_K]
    rank = ids[:, TOP_K:2 * TOP_K]
    dest = rank + jnp.sum(jnp.where(e[:, :, None] == eidx, base[:, None, :], 0), axis=-1)
    n_slots = (r * TOP_K // MOE_ROWS + N_EXPERTS) * MOE_ROWS
    nsb_e = (padded + MOE_SB_ROWS - 1) // MOE_SB_ROWS
    sb_end = jnp.cumsum(nsb_e)
    n_sb = sb_end[-1]
    nsb = n_slots // MOE_SB_ROWS + N_EXPERTS
    j = jnp.arange(nsb, dtype=jnp.int32)
    valid = j < n_sb
    e_j = jnp.minimum(jnp.sum(sb_end[None, :] <= j[:, None], axis=1), N_EXPERTS - 1)
    t = j - _take(sb_end - nsb_e, e_j)
    sb_blk0 = jnp.where(valid, (_take(pad_start, e_j) + t * MOE_SB_ROWS) // MOE_ROWS, 0)
    sb_nblk = jnp.where(valid, jnp.clip((_take(padded, e_j) - t * MOE_SB_ROWS) // MOE_ROWS,
                                        0, MOE_SB_ROWS // MOE_ROWS), 0)
    e_last = jnp.sum(jnp.where(j == n_sb - 1, e_j, 0))
    sb_e = jnp.where(valid, e_j, e_last)
    n_valid = jnp.stack([pad_end[-1] // MOE_ROWS, n_sb])
    i32 = lambda a: a.astype(jnp.int32)
    return i32(dest), i32(pad_end), (i32(sb_e), i32(sb_blk0), i32(sb_nblk), i32(n_valid)), n_slots


def _expand_blockdiag(w):
    nb, bs, _ = w.shape
    if bs == LANES:
        return w.astype(BF16)
    per = LANES // bs
    eye = jnp.eye(per, dtype=w.dtype)
    wt = w.reshape(nb // per, per, bs, bs)
    out = jnp.einsum('tpbc,pq->tpbqc', wt, eye)
    return out.reshape(nb // per, LANES, LANES).astype(BF16)


def _rope_tables(pos, half):
    freq = ROPE_BASE ** (-jnp.arange(half, dtype=F32) / half)
    ang = pos[:, None] * freq[None, :]
    cos, sin = jnp.cos(ang), jnp.sin(ang)
    return jnp.concatenate([cos, cos], axis=-1), jnp.concatenate([-sin, sin], axis=-1)


def kernel(x_prompt, x_sample, state_rglru_h, state_rglru_conv, state_ret, state_mlstm_C, state_mlstm_n,
           state_mlstm_m, state_mlstm_conv, meta_tokens, norm_mix, norm_ffn, norm_final, w_in, rg_conv_w,
           rg_conv_b, rg_wa, rg_ba, rg_wx, rg_bx, rg_lambda, ret_norm, m_conv_w, m_conv_b, m_wq, m_wk, m_wv,
           m_w_if, m_b_if, m_norm, m_skip, w_branch, w_out, moe_w_group, moe_b_group, moe_w_expert,
           moe_b_expert, moe_w_gate, moe_w_up, moe_w_down):
    bp, seq, d = x_prompt.shape
    bs = x_sample.shape[0]
    n_meta = meta_tokens.shape[0]
    depth = w_in.shape[0]
    d_rnn = state_rglru_h.shape[2]
    _, _, r_heads, r_dk, r_dv = state_ret.shape
    m_heads, m_dh = state_mlstm_C.shape[2], state_mlstm_C.shape[3]
    d_m = m_heads * m_dh
    assert seq % CHUNK == 0 and n_meta <= CHUNK and x_sample.shape[1] == 1
    pad = CHUNK - n_meta
    nch = 1 + seq // CHUNK
    rp = bp * nch * CHUNK
    r = rp + bs
    assert d_rnn == r_heads * r_dv == d_m
    xm_col = (d_rnn + 2 * r_heads * r_dk + 2 * r_heads * r_dv) // d_m
    gate_col0 = d_rnn + 2 * r_heads * r_dk + 2 * r_heads * r_dv + 2 * d_m

    pos_p = jnp.arange(nch * CHUNK, dtype=F32) - pad
    cos_p, sin_p = _rope_tables(pos_p, r_dk // 2)
    pos_s = jnp.full((1,), float(PAST_LEN), F32)
    cos_s, sin_s = _rope_tables(pos_s, r_dk // 2)

    tm = _pick_tile(r, 384, BF16_SUBLANES)
    assert rp % CHUNK == 0 and bs % BF16_SUBLANES == 0
    prompt_states, sample_states = [], []
    combine = None
    ret_s = c_s = None
    for l in range(depth):
        p = {'rg_conv_w': rg_conv_w[l], 'rg_conv_b': rg_conv_b[l].reshape(1, -1),
             'rg_wa': _expand_blockdiag(rg_wa[l]), 'rg_ba': rg_ba[l].reshape(1, -1),
             'rg_wx': _expand_blockdiag(rg_wx[l]), 'rg_bx': rg_bx[l].reshape(1, -1),
             'rg_lambda': rg_lambda[l].reshape(1, -1),
             'm_conv_w': m_conv_w[l], 'm_conv_b': m_conv_b[l].reshape(1, -1),
             'm_wq': _expand_blockdiag(m_wq[l]), 'm_wk': _expand_blockdiag(m_wk[l]),
             'm_wv': _expand_blockdiag(m_wv[l]),
             'm_w_if': jnp.pad(m_w_if[l], ((0, 0), (0, LANES - 2 * m_heads))).astype(BF16),
             'm_b_if': jnp.pad(m_b_if[l], (0, LANES - 2 * m_heads)).reshape(1, LANES),
             'm_norm': m_norm[l].reshape(1, -1), 'm_skip': m_skip[l].reshape(1, -1)}
        gn_ret = ret_norm[l].reshape(1, -1)

        if l == 0:
            x, h = _embed_norm_call(x_prompt, meta_tokens, x_sample.reshape(bs, d), norm_mix[l], nch,
                                    name=f'norm_mix{l}')
        else:
            x, h = _norm_call(x, norm_mix[l], tm=tm, grid=(r // tm,), row_tile=lambda i: i, want_x=True,
                              h_shape=(r, d), h_dtype=BF16, h_spec=pl.BlockSpec((tm, d), lambda i: (i, 0)),
                              combine=combine, name=f'norm_mix{l}')
        z = _mm_call(h, w_in, l, name=f'in_proj{l}')

        ya_p, rgh_p, rgc_p = _pa_call(z, bp, nch, pad, p, name=f'rglru_p{l}')
        yb_p, ret_p = _pb_call(z, bp, nch, pad, r_heads, r_dk, r_dv, cos_p, sin_p, gn_ret, name=f'ret_p{l}')
        yc_p, c_p, n_p, m_p, mc_p = _pc_call(z, bp, nch, pad, m_heads, xm_col, p, name=f'mlstm_p{l}')

        rg_cs = jnp.transpose(state_rglru_conv[l], (1, 0, 2))
        ya_s, rgh_s, rgc_s = _sa_call(z, rp, bs, rg_cs, state_rglru_h[l], p, name=f'rglru_s{l}')
        yb_s, ret_s = _sb_call(z, rp, bs, r_heads, r_dk, r_dv, state_ret, l, ret_s, cos_s, sin_s, gn_ret,
                               name=f'ret_s{l}')
        m_cs = jnp.transpose(state_mlstm_conv[l], (1, 0, 2))
        m_in = jnp.pad(state_mlstm_m[l], ((0, 0), (0, LANES - m_heads)))
        yc_s, c_s, n_s, m_s, mc_s = _sc_call(z, rp, bs, m_heads, xm_col, m_cs, state_mlstm_C, l, c_s,
                                             state_mlstm_n[l].reshape(bs, d_m), m_in, p, name=f'mlstm_s{l}')

        prompt_states.append((rgh_p[:, 0], rgc_p, ret_p, c_p, n_p[:, :m_heads], m_p[:, 0, :m_heads], mc_p))
        sample_states.append((rgh_s, jnp.transpose(rgc_s, (1, 0, 2)), None, None,
                              n_s.reshape(bs, m_heads, m_dh), m_s[:, :m_heads], jnp.transpose(mc_s, (1, 0, 2))))

        merged = _merge_call((ya_p, yb_p, yc_p), (ya_s, yb_s, yc_s), z, w_branch, l, gate_col0, name=f'merge{l}')
        x = _mm_call(merged, w_out, l, res=x, name=f'out_proj{l}')

        wr = jnp.pad(jnp.concatenate([moe_w_group[l], moe_w_expert[l]], axis=1),
                     ((0, 0), (0, LANES - N_GROUPS - N_EXPERTS)))
        br = jnp.pad(jnp.concatenate([moe_b_group[l], moe_b_expert[l]]), (0, LANES - N_GROUPS - N_EXPERTS))
        h2, ids, wts, tile_cnt = _norm_call(
            x, norm_ffn[l], tm=tm, grid=(r // tm,), row_tile=lambda i: i, want_x=False, h_shape=(r, d // 2),
            h_dtype=U32, h_spec=pl.BlockSpec((tm, d // 2), lambda i: (i, 0)), router=(wr, br.reshape(1, LANES)),
            name=f'norm_router{l}')
        dest, pad_end, sb_tables, n_slots = _route_tables(ids, tile_cnt, r, tm)
        xs = _dispatch_call(h2, dest, pad_end, n_slots, tm=tm, name=f'moe_dispatch{l}')
        ys = _ffn_call(xs, sb_tables, moe_w_gate, moe_w_up, moe_w_down, l, name=f'moe_ffn{l}')
        combine = (dest, wts, ys)

    (y_prompt,) = _norm_call(
        x, norm_final, tm=CHUNK, grid=(bp, seq // CHUNK), row_tile=lambda b, c: (1 + c) * bp + b, want_x=False,
        h_shape=(bp, seq, d), h_dtype=F32, h_spec=pl.BlockSpec((None, CHUNK, d), lambda b, c: (b, c, 0)),
        combine=combine, name='norm_final_p')
    ts = _pick_tile(bs, CHUNK, SUBLANES)
    assert rp % ts == 0
    (y_sample,) = _norm_call(
        x, norm_final, tm=ts, grid=(bs // ts,), row_tile=lambda i: rp // ts + i, want_x=False,
        h_shape=(bs, d), h_dtype=F32, h_spec=pl.BlockSpec((ts, d), lambda i: (i, 0)),
        combine=combine, name='norm_final_s')
    pn = [jnp.stack([s[i] for s in prompt_states], axis=0) for i in range(7)]
    sn = [None if sample_states[0][i] is None else jnp.stack([s[i] for s in sample_states], axis=0)
          for i in range(7)]
    sn[2], sn[3] = ret_s, c_s
    return (y_prompt, y_sample.reshape(bs, 1, d), *pn, *sn)
```

```python
import functools
import math

import jax
import jax.numpy as jnp
import numpy as np
from jax import lax
from jax.experimental import pallas as pl
from jax.experimental.pallas import tpu as pltpu

F32 = jnp.float32
BF16 = jnp.bfloat16

LANES = 128
SUBLANES = 8
BF16_SUBLANES = 16
VMEM_LIMIT_BYTES = 56 * 1024 * 1024

CHUNK = 128
CONV_W = 4
EPS = 1e-6
RG_C = 8.0
ROPE_BASE = 10000.0
PAST_LEN = 16384
N_GROUPS = 4
EXP_PER_GROUP = 8
N_EXPERTS = N_GROUPS * EXP_PER_GROUP
TOP_K = 2
MOE_ROWS = 128
MOE_SB_ROWS = 1024
FFN_ROWS = 256
FFN_GU_CHUNK = 1024
FFN_DN_CHUNK = 512
NEG = -1e30


def _cparams(sem, vmem=VMEM_LIMIT_BYTES):
    return pltpu.CompilerParams(dimension_semantics=sem, vmem_limit_bytes=vmem)


def _pick_tile(n, cap, mult):
    best = None
    for t in range(mult, min(n, cap) + 1, mult):
        if n % t == 0:
            best = t
    assert best is not None, (n, cap, mult)
    return best


def _skip_refs(fn, idxs):
    def wrapped(*refs):
        return fn(*[r for i, r in enumerate(refs) if i not in idxs])
    return wrapped


def _alias_args(n_in, aliased):
    arrs = [(a, o) for a, o in aliased if a is not None]
    idxs = list(range(n_in, n_in + len(arrs)))
    return ([pl.BlockSpec(memory_space=pl.ANY)] * len(arrs), [a for a, _ in arrs], idxs,
            {i: o for i, (_, o) in zip(idxs, arrs)})


def _dot(a, b):
    return jnp.dot(a, b, preferred_element_type=F32)


def _dot_nt(a, b):
    return lax.dot_general(a, b, (((1,), (1,)), ((), ())), preferred_element_type=F32)


def _dot_tn(a, b):
    return lax.dot_general(a, b, (((0,), (0,)), ((), ())), preferred_element_type=F32)


def _split3(x):
    hi = x.astype(BF16)
    r1 = x - hi.astype(F32)
    mid = r1.astype(BF16)
    lo = (r1 - mid.astype(F32)).astype(BF16)
    return hi, mid, lo


U32 = jnp.uint32


def _pack_halves(x):
    n = x.shape[1] // 2
    lo = lax.bitcast_convert_type(x[:, :n].astype(BF16).astype(F32), U32)
    hi = lax.bitcast_convert_type(x[:, n:].astype(BF16).astype(F32), U32)
    return (lo >> 16) | (hi & U32(0xFFFF0000))


def _unpack_half(p, half):
    bits = (p << 16) if half == 0 else (p & U32(0xFFFF0000))
    return lax.bitcast_convert_type(bits, F32)


def _dot_hi(a, b):
    return jnp.dot(a, b, preferred_element_type=F32, precision=lax.Precision.HIGHEST)


def _sigmoid(x):
    return 1.0 / (1.0 + jnp.exp(-x))


def _silu(x):
    return x * _sigmoid(x)


def _log_sigmoid(x):
    return jnp.minimum(x, 0.0) - jnp.log(1.0 + jnp.exp(-jnp.abs(x)))


def _softplus(x):
    return jnp.maximum(x, 0.0) + jnp.log(1.0 + jnp.exp(-jnp.abs(x)))


def _head_norm(o):
    mu = jnp.mean(o, axis=-1, keepdims=True)
    d = o - mu
    var = jnp.mean(d * d, axis=-1, keepdims=True)
    return d * lax.rsqrt(var + EPS)


def _blockdiag(x, w_ref, scale=None):
    outs = []
    for j in range(w_ref.shape[0]):
        o = _dot(x[:, j * LANES:(j + 1) * LANES].astype(BF16), w_ref[j])
        outs.append(o if scale is None else o * scale)
    return jnp.concatenate(outs, axis=-1)


ROW_DMA_UNROLL = 8


def _row_copies(n_rows, row_copy):
    def start(r, carry):
        for k in range(TOP_K):
            row_copy(r, k).start(priority=k % 2)
        return carry

    def wait(r, carry):
        for k in range(TOP_K):
            row_copy(r, k).wait()
        return carry

    lax.fori_loop(0, n_rows, start, 0, unroll=ROW_DMA_UNROLL)
    lax.fori_loop(0, n_rows, wait, 0, unroll=ROW_DMA_UNROLL)


def _norm_kernel(*refs, combine, want_x, router):
    refs = list(refs)
    if combine:
        dest_ref, wts_in_ref, yb_hbm = refs[:3]
        refs = refs[3:]
    x_ref, g_ref = refs[:2]
    refs = refs[2:]
    if router:
        wr_ref, br_ref = refs[:2]
        refs = refs[2:]
    n_out = int(want_x) + 1 + (3 if router else 0)
    outs, scratch = refs[:n_out], refs[n_out:]
    x = x_ref[...]
    tm = x.shape[0]
    if combine:
        gbuf, gsem = scratch
        _row_copies(tm, lambda r, k: pltpu.make_async_copy(
            yb_hbm.at[pl.ds(dest_ref[0, TOP_K * r + k], 1), :], gbuf.at[k, pl.ds(r, 1), :], gsem))
        for k in range(TOP_K):
            y = jnp.concatenate([_unpack_half(gbuf[k], 0), _unpack_half(gbuf[k], 1)], axis=-1)
            x = x + wts_in_ref[:, k:k + 1] * y
    k = 0
    if want_x:
        outs[k][...] = x
        k += 1
    ms = jnp.mean(x * x, axis=-1, keepdims=True)
    h = x * lax.rsqrt(ms + EPS) * g_ref[...]
    outs[k][...] = _pack_halves(h) if router else h.astype(outs[k].dtype)
    k += 1
    if router:
        ids_ref, wts_ref, cnt_ref = outs[k], outs[k + 1], outs[k + 2]
        logits = _dot_hi(h, wr_ref[...]) + br_ref[...]
        lane = lax.broadcasted_iota(jnp.int32, logits.shape, 1)
        big = jnp.int32(1 << 20)
        is_g = lane < N_GROUPS
        gl = jnp.where(is_g, logits, NEG)
        gmax = jnp.max(gl, axis=-1, keepdims=True)
        gidx = jnp.min(jnp.where(gl == gmax, lane, big), axis=-1, keepdims=True)
        gsum = jnp.sum(jnp.where(is_g, jnp.exp(gl - gmax), 0.0), axis=-1, keepdims=True)
        gprob = 1.0 / gsum
        lo = N_GROUPS + EXP_PER_GROUP * gidx
        em = jnp.where(lane >= lo, jnp.where(lane < lo + EXP_PER_GROUP, logits, NEG), NEG)
        e1v = jnp.max(em, axis=-1, keepdims=True)
        e1i = jnp.min(jnp.where(em == e1v, lane, big), axis=-1, keepdims=True)
        em2 = jnp.where(lane == e1i, NEG, em)
        e2v = jnp.max(em2, axis=-1, keepdims=True)
        e2i = jnp.min(jnp.where(em2 == e2v, lane, big), axis=-1, keepdims=True)
        t = jnp.exp(e2v - e1v)
        w1 = gprob / (1.0 + t)
        w2 = gprob * t / (1.0 + t)
        e1, e2 = e1i - N_GROUPS, e2i - N_GROUPS
        chosen = jnp.where(lane == e1, 1.0, jnp.where(lane == e2, 1.0, 0.0))
        ti = lax.broadcasted_iota(jnp.int32, (tm, tm), 0)
        si = lax.broadcasted_iota(jnp.int32, (tm, tm), 1)
        before = _dot(jnp.where(ti > si, 1.0, 0.0).astype(BF16), chosen.astype(BF16))
        rank1 = jnp.sum(jnp.where(lane == e1, before, 0.0), axis=-1, keepdims=True).astype(jnp.int32)
        rank2 = jnp.sum(jnp.where(lane == e2, before, 0.0), axis=-1, keepdims=True).astype(jnp.int32)
        ids_ref[...] = jnp.where(lane == 0, e1, jnp.where(lane == 1, e2,
                                 jnp.where(lane == 2, rank1, jnp.where(lane == 3, rank2, 0))))
        wts_ref[...] = jnp.where(lane == 0, w1, jnp.where(lane == 1, w2, 0.0))
        cnt_ref[...] = jnp.sum(chosen, axis=0, keepdims=True)


def _norm_call(x, g, *, tm, grid, row_tile, want_x, h_shape, h_dtype, h_spec, combine=None, router=None, name):
    r, d = x.shape
    x_spec = pl.BlockSpec((tm, d), lambda *gi: (row_tile(*gi), 0))
    in_specs, args, scratch = [], [], []
    if combine is not None:
        dest, wts, yb = combine
        in_specs += [pl.BlockSpec((None, 1, TOP_K * tm), lambda *gi: (row_tile(*gi), 0, 0), memory_space=pltpu.SMEM),
                     pl.BlockSpec((tm, LANES), lambda *gi: (row_tile(*gi), 0)),
                     pl.BlockSpec(memory_space=pl.ANY)]
        args += [dest.reshape(r // tm, 1, TOP_K * tm), wts, yb]
        assert yb.dtype == U32 and yb.shape[1] * 2 == d
        scratch = [pltpu.VMEM((TOP_K, tm, d // 2), U32), pltpu.SemaphoreType.DMA(())]
    in_specs += [x_spec, pl.BlockSpec((1, d), lambda *gi: (0, 0))]
    args += [x, g.reshape(1, d)]
    if router is not None:
        in_specs += [pl.BlockSpec((d, LANES), lambda *gi: (0, 0)), pl.BlockSpec((1, LANES), lambda *gi: (0, 0))]
        args += list(router)
    out_shape, out_specs = [], []
    if want_x:
        out_shape.append(jax.ShapeDtypeStruct((r, d), F32))
        out_specs.append(x_spec)
    out_shape.append(jax.ShapeDtypeStruct(h_shape, h_dtype))
    out_specs.append(h_spec)
    if router is not None:
        out_shape += [jax.ShapeDtypeStruct((r, LANES), jnp.int32), jax.ShapeDtypeStruct((r, LANES), F32),
                      jax.ShapeDtypeStruct((r // tm, 1, LANES), F32)]
        out_specs += [pl.BlockSpec((tm, LANES), lambda *gi: (row_tile(*gi), 0))] * 2
        out_specs += [pl.BlockSpec((None, 1, LANES), lambda *gi: (row_tile(*gi), 0, 0))]
    return pl.pallas_call(
        functools.partial(_norm_kernel, combine=combine is not None, want_x=want_x, router=router is not None),
        grid=grid, in_specs=in_specs, out_specs=out_specs, out_shape=out_shape, scratch_shapes=scratch,
        compiler_params=_cparams(("arbitrary",) * len(grid)), name=name)(*args)


def _embed_norm_kernel(xp_ref, meta_ref, xs_ref, g_ref, x_ref, h_ref, *, n_seq, n_prompt_tiles):
    i = pl.program_id(0)
    c = i // n_seq
    tm, d = x_ref.shape

    def finish(x):
        x_ref[...] = x
        ms = jnp.mean(x * x, axis=-1, keepdims=True)
        h_ref[...] = (x * lax.rsqrt(ms + EPS) * g_ref[...]).astype(h_ref.dtype)

    @pl.when(i >= n_prompt_tiles)
    def _():
        finish(xs_ref[...])

    @pl.when(jnp.logical_and(i < n_prompt_tiles, c == 0))
    def _():
        finish(jnp.concatenate([jnp.zeros((tm - meta_ref.shape[0], d), F32), meta_ref[...]], axis=0))

    @pl.when(jnp.logical_and(i < n_prompt_tiles, c > 0))
    def _():
        finish(xp_ref[...])


def _embed_norm_call(x_prompt, meta, x_sample, g, nch, *, name):
    bp, seq, d = x_prompt.shape
    bs = x_sample.shape[0]
    npt = bp * nch
    assert bs % CHUNK == 0 and meta.shape[0] % SUBLANES == 0
    r = npt * CHUNK + bs
    row_spec = pl.BlockSpec((CHUNK, d), lambda i: (i, 0))
    return pl.pallas_call(
        functools.partial(_embed_norm_kernel, n_seq=bp, n_prompt_tiles=npt), grid=(r // CHUNK,),
        in_specs=[pl.BlockSpec((None, CHUNK, d),
                               lambda i: (lax.rem(i, bp), jnp.clip(i // bp - 1, 0, nch - 2), 0)),
                  pl.BlockSpec(meta.shape, lambda i: (0, 0)),
                  pl.BlockSpec((CHUNK, d), lambda i: (jnp.maximum(i - npt, 0), 0)),
                  pl.BlockSpec((1, d), lambda i: (0, 0))],
        out_specs=[row_spec, row_spec],
        out_shape=[jax.ShapeDtypeStruct((r, d), F32), jax.ShapeDtypeStruct((r, d), BF16)],
        compiler_params=_cparams(("arbitrary",)), name=name)(x_prompt, meta, x_sample, g.reshape(1, d))


def _zero_blocks(zero_ref, out_hbm, first, last, sem):
    def blk_copy(b):
        return pltpu.make_async_copy(zero_ref, out_hbm.at[pl.ds(pl.multiple_of(b * MOE_ROWS, MOE_ROWS), MOE_ROWS), :], sem)

    def start(b, carry):
        blk_copy(b).start()
        return carry

    def wait(b, carry):
        blk_copy(b).wait()
        return carry

    lax.fori_loop(first, last, start, 0)
    lax.fori_loop(first, last, wait, 0)


def _dispatch_kernel(pe_ref, dest_ref, h_ref, xs_out, zero_ref, sem, zsem):
    tm = h_ref.shape[0]
    n_blocks = xs_out.shape[0] // MOE_ROWS

    @pl.when(pl.program_id(0) == 0)
    def _():
        zero_ref[...] = jnp.zeros_like(zero_ref)

        def last_block(e):
            b = jnp.maximum(pe_ref[e] // MOE_ROWS - 1, 0)
            return pltpu.make_async_copy(
                zero_ref, xs_out.at[pl.ds(pl.multiple_of(b * MOE_ROWS, MOE_ROWS), MOE_ROWS), :], zsem)

        for e in range(N_EXPERTS):
            last_block(e).start()
        for e in range(N_EXPERTS):
            last_block(e).wait()
        _zero_blocks(zero_ref, xs_out, pe_ref[N_EXPERTS - 1] // MOE_ROWS, n_blocks, zsem)

    _row_copies(tm, lambda r, k: pltpu.make_async_copy(
        h_ref.at[pl.ds(r, 1), :], xs_out.at[pl.ds(dest_ref[0, TOP_K * r + k], 1), :], sem))


def _dispatch_call(h2, dest, pad_end, n_slots, *, tm, name):
    r, d = h2.shape
    grid_spec = pltpu.PrefetchScalarGridSpec(
        num_scalar_prefetch=1, grid=(r // tm,),
        in_specs=[pl.BlockSpec((None, 1, TOP_K * tm), lambda i, pe: (i, 0, 0), memory_space=pltpu.SMEM),
                  pl.BlockSpec((tm, d), lambda i, pe: (i, 0))],
        out_specs=pl.BlockSpec(memory_space=pl.ANY),
        scratch_shapes=[pltpu.VMEM((MOE_ROWS, d), h2.dtype), pltpu.SemaphoreType.DMA(()),
                        pltpu.SemaphoreType.DMA(())])
    return pl.pallas_call(
        _dispatch_kernel, grid_spec=grid_spec, out_shape=jax.ShapeDtypeStruct((n_slots, d), h2.dtype),
        compiler_params=_cparams(("arbitrary",)), name=name)(pad_end, dest.reshape(r // tm, 1, TOP_K * tm), h2)


def _mm_kernel(*refs, has_res):
    if has_res:
        x_ref, w_ref, r_ref, o_ref, wb_ref = refs
    else:
        x_ref, w_ref, o_ref, wb_ref = refs

    @pl.when(pl.program_id(1) == 0)
    def _():
        wb_ref[...] = w_ref[...].astype(BF16)

    acc = _dot(x_ref[...], wb_ref[...])
    if has_res:
        acc = acc + r_ref[...]
    o_ref[...] = acc.astype(o_ref.dtype)


def _mm_call(x, w, l, res=None, *, out_dtype=F32, name):
    r, k = x.shape
    n = w.shape[2]
    tm = _pick_tile(r, 1152, BF16_SUBLANES)
    tn = _pick_tile(n, 1024, LANES)
    in_specs = [pl.BlockSpec((tm, k), lambda j, i: (i, 0)), pl.BlockSpec((None, k, tn), lambda j, i: (l, 0, j))]
    args = [x, w]
    if res is not None:
        in_specs.append(pl.BlockSpec((tm, tn), lambda j, i: (i, j)))
        args.append(res)
    return pl.pallas_call(
        functools.partial(_mm_kernel, has_res=res is not None),
        grid=(n // tn, r // tm), in_specs=in_specs,
        out_specs=pl.BlockSpec((tm, tn), lambda j, i: (i, j)),
        out_shape=jax.ShapeDtypeStruct((r, n), out_dtype),
        scratch_shapes=[pltpu.VMEM((k, tn), BF16)],
        compiler_params=_cparams(("arbitrary", "arbitrary")), name=name)(*args)


def _merge_kernel(*refs, n_branch):
    y_refs, g_refs = refs[:n_branch], refs[n_branch:2 * n_branch]
    w_ref, o_ref, wb_ref = refs[2 * n_branch:]

    @pl.when(pl.program_id(1) == 0)
    def _():
        wb_ref[...] = w_ref[...].astype(BF16)

    acc = _sigmoid(g_refs[0][...]) * _dot(y_refs[0][...], wb_ref[0])
    for n in range(1, n_branch):
        acc = acc + _sigmoid(g_refs[n][...]) * _dot(y_refs[n][...], wb_ref[n])
    o_ref[...] = acc.astype(o_ref.dtype)


def _merge_call(ys, z, w_branch, l, gate_col0, *, name):
    r = z.shape[0]
    db = ys[0].shape[1]
    _, nbr, _, d = w_branch.shape
    tn = _pick_tile(d, 1024, LANES)
    tm = _pick_tile(r, 384, BF16_SUBLANES)
    assert gate_col0 % tn == 0 and d % tn == 0 and all(y.shape[0] >= r for y in ys)
    g0 = gate_col0 // tn
    per = d // tn
    y_spec = pl.BlockSpec((tm, db), lambda j, i: (i, 0))
    g_specs = [pl.BlockSpec((tm, tn), lambda j, i, n=n: (i, g0 + per * n + j)) for n in range(nbr)]
    return pl.pallas_call(
        functools.partial(_merge_kernel, n_branch=nbr), grid=(d // tn, r // tm),
        in_specs=[y_spec] * nbr + g_specs + [pl.BlockSpec((None, nbr, db, tn), lambda j, i: (l, 0, 0, j))],
        out_specs=pl.BlockSpec((tm, tn), lambda j, i: (i, j)),
        out_shape=jax.ShapeDtypeStruct((r, d), BF16),
        scratch_shapes=[pltpu.VMEM((nbr, db, tn), BF16)],
        compiler_params=_cparams(("arbitrary", "arbitrary")), name=name)(*ys, *([z] * nbr), w_branch)


def _shift_rows(x, prev8, k, row):
    rolled = pltpu.roll(x, k, 0)
    head = jnp.where(row[:SUBLANES] < k, pltpu.roll(prev8, k, 0), rolled[:SUBLANES])
    return jnp.concatenate([head, rolled[SUBLANES:]], axis=0)


def _chunk_conv(x, prev8, w_ref, b_ref, row):
    y = b_ref[...] + w_ref[CONV_W - 1:CONV_W, :] * x
    for k in range(1, CONV_W):
        y = y + w_ref[CONV_W - 1 - k:CONV_W - k, :] * _shift_rows(x, prev8, k, row)
    return y


def _rglru_coeffs(xc, wa_ref, ba_ref, wx_ref, bx_ref, lam_ref):
    r = _sigmoid(_blockdiag(xc, wa_ref) + ba_ref[...])
    i = _sigmoid(_blockdiag(xc, wx_ref) + bx_ref[...])
    log_a = -RG_C * r * _softplus(-lam_ref[...])
    a = jnp.exp(log_a)
    mult = jnp.sqrt(1.0 - a * a)
    return a, mult * i * xc


def _per_sequence(body, nseq, n_rows_in, n_const, n_rows_out, n_state_out):
    def kernel(*refs):
        rows_in = refs[:n_rows_in]
        consts = refs[n_rows_in:n_rows_in + n_const]
        o = n_rows_in + n_const
        rows_out = refs[o:o + n_rows_out]
        state_out = refs[o + n_rows_out:o + n_rows_out + n_state_out]
        scratch = refs[o + n_rows_out + n_state_out:]
        L = rows_in[0].shape[0] // nseq

        @pl.when(pl.program_id(0) <= _last_chunk())
        def _():
            for b in range(nseq):
                rows = lambda r, b=b: r.at[pl.ds(b * L, L), :]
                body(*[rows(r) for r in rows_in], *consts, *[rows(r) for r in rows_out],
                     *[r.at[pl.ds(b, 1)] for r in state_out], *[r.at[b] for r in scratch])

        @pl.when(pl.program_id(0) > _last_chunk())
        def _():
            for r in rows_out:
                r[...] = jnp.zeros_like(r)
    return kernel


def _last_chunk():
    return pl.num_programs(0) - 2


def _pa_kernel(x_ref, cw_ref, cb_ref, wa_ref, ba_ref, wx_ref, bx_ref, lam_ref,
               y_ref, hl_ref, cv_ref, prev_ref, h_ref, *, pad):
    c = pl.program_id(0)
    last = _last_chunk()
    L = x_ref.shape[0]
    row = lax.broadcasted_iota(jnp.int32, (L, 1), 0)
    valid = jnp.logical_or(c > 0, row >= pad)

    @pl.when(c == 0)
    def _():
        prev_ref[...] = jnp.zeros_like(prev_ref)
        h_ref[...] = jnp.zeros_like(h_ref)

    x = jnp.where(valid, x_ref[...], 0.0)
    xc = _chunk_conv(x, prev_ref[...], cw_ref, cb_ref, row)
    prev_ref[...] = x[L - SUBLANES:]
    a, b = _rglru_coeffs(xc, wa_ref, ba_ref, wx_ref, bx_ref, lam_ref)
    a = jnp.where(valid, a, 1.0)
    b = jnp.where(valid, b, 0.0)
    hs = []
    for j in range(x.shape[1] // LANES):
        sl = slice(j * LANES, (j + 1) * LANES)
        aj, bj = a[:, sl], b[:, sl]
        d = 1
        while d < L:
            keep = row >= d
            a_s = jnp.where(keep, pltpu.roll(aj, d, 0), 1.0)
            b_s = jnp.where(keep, pltpu.roll(bj, d, 0), 0.0)
            bj = aj * b_s + bj
            aj = aj * a_s
            d *= 2
        hs.append(aj * h_ref[0:1, sl] + bj)
    h = jnp.concatenate(hs, axis=-1)
    h_ref[0:1, :] = h[L - 1:L]
    y_ref[...] = h.astype(y_ref.dtype)

    @pl.when(c == last)
    def _():
        hl_ref[0] = h[L - 1:L]
        cv_ref[0] = x[L - (CONV_W - 1):]


def _pa_call(z, nseq, nch, pad, p, *, name):
    rows = nseq * nch * CHUNK
    c = p['rg_conv_w'].shape[1]
    full = lambda a: pl.BlockSpec(a.shape, lambda k: (0,) * a.ndim)
    whole = lambda *shape: pl.BlockSpec(shape, lambda k: (0,) * len(shape))
    params = [p['rg_conv_w'], p['rg_conv_b'], p['rg_wa'], p['rg_ba'], p['rg_wx'], p['rg_bx'], p['rg_lambda']]
    return pl.pallas_call(
        _per_sequence(functools.partial(_pa_kernel, pad=pad), nseq, 1, len(params), 1, 2), grid=(nch + 1,),
        in_specs=[pl.BlockSpec((nseq * CHUNK, c), lambda k: (jnp.minimum(k, nch - 1), 0))]
        + [full(a) for a in params],
        out_specs=[pl.BlockSpec((nseq * CHUNK, c), lambda k: (k, 0)),
                   whole(nseq, 1, c), whole(nseq, CONV_W - 1, c)],
        out_shape=[jax.ShapeDtypeStruct((rows + nseq * CHUNK, c), BF16),
                   jax.ShapeDtypeStruct((nseq, 1, c), F32),
                   jax.ShapeDtypeStruct((nseq, CONV_W - 1, c), F32)],
        scratch_shapes=[pltpu.VMEM((nseq, SUBLANES, c), F32), pltpu.VMEM((nseq, SUBLANES, c), F32)],
        compiler_params=_cparams(("arbitrary",)), name=name)(z, *params)


def _step_conv(x, cs_ref, w_ref, b_ref):
    y = b_ref[...] + w_ref[CONV_W - 1:CONV_W, :] * x
    for k in range(CONV_W - 1):
        y = y + w_ref[k:k + 1, :] * cs_ref[k]
    return y


def _step_conv_state(x, cs_ref, out_ref):
    for k in range(CONV_W - 2):
        out_ref[k] = cs_ref[k + 1]
    out_ref[CONV_W - 2] = x


def _sa_kernel(x_ref, cs_ref, h0_ref, cw_ref, cb_ref, wa_ref, ba_ref, wx_ref, bx_ref, lam_ref,
               y_ref, hn_ref, cv_ref):
    x = x_ref[...]
    xc = _step_conv(x, cs_ref, cw_ref, cb_ref)
    a, b = _rglru_coeffs(xc, wa_ref, ba_ref, wx_ref, bx_ref, lam_ref)
    h = a * h0_ref[...] + b
    hn_ref[...] = h
    y_ref[...] = h.astype(y_ref.dtype)
    _step_conv_state(x, cs_ref, cv_ref)


def _sa_call(z, row0, bs, y_all, cs, h0, p, *, name):
    c = h0.shape[1]
    assert row0 % bs == 0
    full = lambda a: pl.BlockSpec(a.shape, lambda i: (0,) * a.ndim)
    params = [p['rg_conv_w'], p['rg_conv_b'], p['rg_wa'], p['rg_ba'], p['rg_wx'], p['rg_bx'], p['rg_lambda']]
    specs, extra, drop, aliases = _alias_args(3 + len(params), [(y_all, 0)])
    return pl.pallas_call(
        _skip_refs(_sa_kernel, drop), grid=(1,),
        in_specs=[pl.BlockSpec((bs, c), lambda i: (row0 // bs, 0)), full(cs), full(h0)] + [full(a) for a in params]
        + specs,
        out_specs=[pl.BlockSpec((bs, c), lambda i: (row0 // bs, 0)), pl.BlockSpec((bs, c), lambda i: (0, 0)),
                   pl.BlockSpec(cs.shape, lambda i: (0, 0, 0))],
        out_shape=[jax.ShapeDtypeStruct(y_all.shape, BF16), jax.ShapeDtypeStruct((bs, c), F32),
                   jax.ShapeDtypeStruct(cs.shape, F32)],
        input_output_aliases=aliases,
        compiler_params=_cparams(("arbitrary",)), name=name)(z, cs, h0, *params, *extra)


def _ret_log_g(h):
    return math.log1p(-(2.0 ** (-5.0 - h)))


def _rope(x, cos2, sin2):
    return x * cos2 + pltpu.roll(x, x.shape[1] // 2, 1) * sin2


def _pb_kernel(q_ref, k_ref, v_ref, g_ref, cos_ref, sin_ref, gn_ref, y_ref, so_ref, s_ref, *, pad, heads):
    c = pl.program_id(0)
    last = _last_chunk()
    L = q_ref.shape[0]
    dk = q_ref.shape[1] // heads
    dv = v_ref.shape[1] // heads
    row = lax.broadcasted_iota(jnp.int32, (L, 1), 0)
    valid = jnp.logical_or(c > 0, row >= pad)

    @pl.when(c == 0)
    def _():
        s_ref[...] = jnp.zeros_like(s_ref)

    t = row.astype(F32)
    rel = t - lax.broadcasted_iota(jnp.int32, (1, L), 1).astype(F32)
    cos2, sin2 = cos_ref[...], sin_ref[...]
    for h in range(heads):
        lg = _ret_log_g(h)
        qh = jnp.where(valid, _rope(q_ref[:, h * dk:(h + 1) * dk], cos2, sin2), 0.0)
        kh = jnp.where(valid, _rope(k_ref[:, h * dk:(h + 1) * dk], cos2, sin2), 0.0) * (dk ** -0.5)
        vh = jnp.where(valid, v_ref[:, h * dv:(h + 1) * dv], 0.0)
        decay = jnp.where(rel >= 0, jnp.exp(jnp.maximum(rel, 0.0) * lg), 0.0)
        qb, vb = qh.astype(BF16), vh.astype(BF16)
        scores = _dot_nt(qb, kh.astype(BF16)) * decay
        s_old = s_ref[h]
        o = _dot(scores.astype(BF16), vb) + _dot(qb, s_old.astype(BF16)) * jnp.exp((t + 1.0) * lg)
        k_dec = kh * jnp.exp((L - 1.0 - t) * lg)
        s_ref[h] = math.exp(L * lg) * s_old + _dot_tn(k_dec.astype(BF16), vb)
        sl = slice(h * dv, (h + 1) * dv)
        y_ref[:, sl] = (_head_norm(o) * gn_ref[:, sl] * _silu(g_ref[:, sl])).astype(y_ref.dtype)

    @pl.when(c == last)
    def _():
        so_ref[0] = s_ref[...]


def _pb_call(z, nseq, nch, pad, heads, dk, dv, cos2, sin2, ret_norm, *, name):
    rows = nseq * nch * CHUNK
    hk, hv = heads * dk, heads * dv
    assert hv % hk == 0
    q0 = hv // hk
    blk = nseq * CHUNK
    return pl.pallas_call(
        _per_sequence(functools.partial(_pb_kernel, pad=pad, heads=heads), nseq, 4, 3, 1, 1), grid=(nch + 1,),
        in_specs=[pl.BlockSpec((blk, hk), lambda k: (jnp.minimum(k, nch - 1), q0)),
                  pl.BlockSpec((blk, hk), lambda k: (jnp.minimum(k, nch - 1), q0 + 1)),
                  pl.BlockSpec((blk, hv), lambda k: (jnp.minimum(k, nch - 1), 2)),
                  pl.BlockSpec((blk, hv), lambda k: (jnp.minimum(k, nch - 1), 3)),
                  pl.BlockSpec((CHUNK, dk), lambda k: (jnp.minimum(k, nch - 1), 0)),
                  pl.BlockSpec((CHUNK, dk), lambda k: (jnp.minimum(k, nch - 1), 0)),
                  pl.BlockSpec((1, hv), lambda k: (0, 0))],
        out_specs=[pl.BlockSpec((blk, hv), lambda k: (k, 0)),
                   pl.BlockSpec((nseq, heads, dk, dv), lambda k: (0, 0, 0, 0))],
        out_shape=[jax.ShapeDtypeStruct((rows + blk, hv), BF16),
                   jax.ShapeDtypeStruct((nseq, heads, dk, dv), F32)],
        scratch_shapes=[pltpu.VMEM((nseq, heads, dk, dv), F32)],
        compiler_params=_cparams(("arbitrary",)), name=name)(z, z, z, z, cos2, sin2, ret_norm)


def _first_pass_only(body, state_out_ref):
    @pl.when(pl.program_id(0) == 0)
    def _():
        body()

    @pl.when(pl.program_id(0) > 0)
    def _():
        state_out_ref[...] = jnp.zeros_like(state_out_ref)


def _pass_maps(l, prev, nblk, depth):
    if prev is not None:
        return 1, (lambda g, i: i), (lambda g, i: l)
    assert l == 0
    return depth, (lambda g, i: jnp.where(g == 0, i, nblk - 1)), (lambda g, i: g)


def _sb_kernel(*refs, heads):
    _first_pass_only(functools.partial(_sb_body, *refs, heads=heads), refs[-1])


def _sb_body(q_ref, k_ref, v_ref, g_ref, s_ref, cos_ref, sin_ref, gn_ref, y_ref, so_ref, *, heads):
    nb = q_ref.shape[0]
    dk = q_ref.shape[1] // heads
    dv = v_ref.shape[1] // heads
    row = lax.broadcasted_iota(jnp.int32, (nb, 1), 0)
    cos2, sin2 = cos_ref[...], sin_ref[...]
    for h in range(heads):
        g = math.exp(_ret_log_g(h))
        qh = _rope(q_ref[:, h * dk:(h + 1) * dk], cos2, sin2)
        kh = _rope(k_ref[:, h * dk:(h + 1) * dk], cos2, sin2) * (dk ** -0.5)
        vh = v_ref[:, h * dv:(h + 1) * dv]
        qb, vb = qh.astype(BF16), vh.astype(BF16)
        cross = jnp.zeros((nb, dv), F32)
        for i in range(nb):
            s_old = s_ref[i, h]
            cross = jnp.where(row == i, _dot(qb, s_old.astype(BF16)), cross)
            so_ref[i, h] = g * s_old + _dot_tn(jnp.where(row == i, kh, 0.0), vh)
        qk = jnp.sum(qb.astype(F32) * kh.astype(BF16).astype(F32), axis=-1, keepdims=True)
        o = qk * vb.astype(F32) + g * cross
        sl = slice(h * dv, (h + 1) * dv)
        y_ref[:, sl] = (_head_norm(o) * gn_ref[:, sl] * _silu(g_ref[:, sl])).astype(y_ref.dtype)


def _sb_call(z, row0, bs, y_all, heads, dk, dv, state, l, prev, cos2, sin2, ret_norm, *, name):
    hk, hv = heads * dk, heads * dv
    nb = SUBLANES
    assert bs % nb == 0 and row0 % nb == 0
    r0 = row0 // nb
    q0 = hv // hk
    passes, blk, lay = _pass_maps(l, prev, bs // nb, state.shape[0])
    specs, extra, drop, aliases = _alias_args(8, [(y_all, 0), (prev, 1)])
    return pl.pallas_call(
        _skip_refs(functools.partial(_sb_kernel, heads=heads), drop),
        grid=(passes, bs // nb),
        in_specs=[pl.BlockSpec((nb, hk), lambda g, i: (r0 + blk(g, i), q0)),
                  pl.BlockSpec((nb, hk), lambda g, i: (r0 + blk(g, i), q0 + 1)),
                  pl.BlockSpec((nb, hv), lambda g, i: (r0 + blk(g, i), 2)),
                  pl.BlockSpec((nb, hv), lambda g, i: (r0 + blk(g, i), 3)),
                  pl.BlockSpec((None, nb, heads, dk, dv), lambda g, i: (l, blk(g, i), 0, 0, 0)),
                  pl.BlockSpec((1, dk), lambda g, i: (0, 0)),
                  pl.BlockSpec((1, dk), lambda g, i: (0, 0)),
                  pl.BlockSpec((1, hv), lambda g, i: (0, 0))] + specs,
        out_specs=[pl.BlockSpec((nb, hv), lambda g, i: (r0 + blk(g, i), 0)),
                   pl.BlockSpec((None, nb, heads, dk, dv), lambda g, i: (lay(g, i), i, 0, 0, 0))],
        out_shape=[jax.ShapeDtypeStruct(y_all.shape, BF16), jax.ShapeDtypeStruct(state.shape, F32)],
        input_output_aliases=aliases,
        compiler_params=_cparams(("arbitrary", "arbitrary")), name=name)(
            z, z, z, z, state, cos2, sin2, ret_norm, *extra)


def _mlstm_qkv_gates(x, xc, wq_ref, wk_ref, wv_ref, wif_ref, bif_ref, dh):
    q = _blockdiag(xc, wq_ref)
    k = _blockdiag(xc, wk_ref, scale=dh ** -0.5)
    v = _blockdiag(x, wv_ref)
    c = x.shape[1]
    gates = (_dot(q.astype(BF16), wif_ref[0:c, :]) + _dot(k.astype(BF16), wif_ref[c:2 * c, :])
             + _dot(v.astype(BF16), wif_ref[2 * c:3 * c, :]) + bif_ref[...])
    return q, k, v, gates


def _pc_kernel(x_ref, o_ref, cw_ref, cb_ref, wq_ref, wk_ref, wv_ref, wif_ref, bif_ref, gn_ref, sk_ref,
               y_ref, co_ref, no_ref, mo_ref, cv_ref, prev_ref, c_ref, n_ref, m_ref, *, pad, heads):
    c = pl.program_id(0)
    last = _last_chunk()
    L = x_ref.shape[0]
    dh = x_ref.shape[1] // heads
    row = lax.broadcasted_iota(jnp.int32, (L, 1), 0)
    valid = jnp.logical_or(c > 0, row >= pad)

    @pl.when(c == 0)
    def _():
        prev_ref[...] = jnp.zeros_like(prev_ref)
        c_ref[...] = jnp.zeros_like(c_ref)
        n_ref[...] = jnp.zeros_like(n_ref)
        m_ref[...] = jnp.zeros_like(m_ref)

    x = jnp.where(valid, x_ref[...], 0.0)
    xc = _silu(_chunk_conv(x, prev_ref[...], cw_ref, cb_ref, row))
    prev_ref[...] = x[L - SUBLANES:]
    q, k, v, gates = _mlstm_qkv_gates(x, xc, wq_ref, wk_ref, wv_ref, wif_ref, bif_ref, dh)
    lane = lax.broadcasted_iota(jnp.int32, gates.shape, 1)
    is_i = lane < heads
    ig_c = jnp.where(jnp.logical_and(valid, is_i), gates, NEG)
    lf_c = jnp.where(valid, _log_sigmoid(gates), 0.0)
    ig_r = ig_c.T
    lf_r = lf_c.T
    ti = lax.broadcasted_iota(jnp.int32, (L, L), 0)
    si = lax.broadcasted_iota(jnp.int32, (L, L), 1)
    causal = ti >= si
    ones_lower = jnp.where(causal, 1.0, 0.0).astype(BF16)
    ones_upper = jnp.where(si >= ti, 1.0, 0.0).astype(BF16)
    b_c = sum(_dot(ones_lower, part) for part in _split3(lf_c))
    b_r = sum(_dot(part, ones_upper) for part in _split3(lf_r))
    m_all = m_ref[...]
    m_lane = lax.broadcasted_iota(jnp.int32, m_all.shape, 1)
    m_new = jnp.zeros_like(m_all)
    for h in range(heads):
        sl = slice(h * dh, (h + 1) * dh)
        bc = b_c[:, heads + h:heads + h + 1]
        br = b_r[heads + h:heads + h + 1, :]
        igr = ig_r[h:h + 1, :]
        igc = ig_c[:, h:h + 1]
        m_prev = m_all[0:1, h:h + 1]
        log_d = jnp.where(causal, bc - br + igr, NEG)
        log_inter = bc + m_prev
        m_t = jnp.maximum(log_inter, jnp.max(log_d, axis=-1, keepdims=True))
        d_m = jnp.exp(log_d - m_t)
        w_inter = jnp.exp(log_inter - m_t)
        qh, kh, vh = q[:, sl], k[:, sl], v[:, sl]
        qb, kb, vb = qh.astype(BF16), kh.astype(BF16), vh.astype(BF16)
        scores = _dot_nt(qb, kb) * d_m
        c_old = c_ref[h]
        n_old = n_ref[h:h + 1, :]
        num = _dot(scores.astype(BF16), vb) + w_inter * _dot(qb, c_old.astype(BF16))
        den = jnp.sum(scores, axis=-1, keepdims=True) + w_inter * jnp.sum(qh * n_old, axis=-1, keepdims=True)
        hh = num / jnp.maximum(jnp.abs(den), jnp.exp(-m_t))
        m_end = m_t[L - 1:L]
        b_last = bc[L - 1:L]
        w_c = jnp.exp(b_last - bc + igc - m_end)
        decay_c = jnp.exp(b_last + m_prev - m_end)
        kw = kh * w_c
        c_ref[h] = decay_c * c_old + _dot_tn(kw.astype(BF16), vb)
        n_ref[h:h + 1, :] = decay_c * n_old + jnp.sum(kw, axis=0, keepdims=True)
        m_new = jnp.where(m_lane == h, m_end, m_new)
        hm = _sigmoid(o_ref[:, sl]) * hh
        y_ref[:, sl] = (_head_norm(hm) * gn_ref[:, sl] + sk_ref[:, sl] * xc[:, sl]).astype(y_ref.dtype)
    m_ref[...] = m_new

    @pl.when(c == last)
    def _():
        co_ref[0] = c_ref[...]
        no_ref[0] = n_ref[...]
        mo_ref[0] = m_ref[...]
        cv_ref[0] = x[L - (CONV_W - 1):]


def _pc_call(z, nseq, nch, pad, heads, xcol, p, *, name):
    rows = nseq * nch * CHUNK
    c = p['m_conv_w'].shape[1]
    dh = c // heads
    full = lambda a: pl.BlockSpec(a.shape, lambda k: (0,) * a.ndim)
    whole = lambda *shape: pl.BlockSpec(shape, lambda k: (0,) * len(shape))
    params = [p['m_conv_w'], p['m_conv_b'], p['m_wq'], p['m_wk'], p['m_wv'], p['m_w_if'], p['m_b_if'],
              p['m_norm'], p['m_skip']]
    blk = nseq * CHUNK
    return pl.pallas_call(
        _per_sequence(functools.partial(_pc_kernel, pad=pad, heads=heads), nseq, 2, len(params), 1, 4),
        grid=(nch + 1,),
        in_specs=[pl.BlockSpec((blk, c), lambda k: (jnp.minimum(k, nch - 1), xcol)),
                  pl.BlockSpec((blk, c), lambda k: (jnp.minimum(k, nch - 1), xcol + 1))] + [full(a) for a in params],
        out_specs=[pl.BlockSpec((blk, c), lambda k: (k, 0)),
                   whole(nseq, heads, dh, dh), whole(nseq, SUBLANES, dh), whole(nseq, SUBLANES, LANES),
                   whole(nseq, CONV_W - 1, c)],
        out_shape=[jax.ShapeDtypeStruct((rows + blk, c), BF16),
                   jax.ShapeDtypeStruct((nseq, heads, dh, dh), F32),
                   jax.ShapeDtypeStruct((nseq, SUBLANES, dh), F32),
                   jax.ShapeDtypeStruct((nseq, SUBLANES, LANES), F32),
                   jax.ShapeDtypeStruct((nseq, CONV_W - 1, c), F32)],
        scratch_shapes=[pltpu.VMEM((nseq, SUBLANES, c), F32), pltpu.VMEM((nseq, heads, dh, dh), F32),
                        pltpu.VMEM((nseq, SUBLANES, dh), F32), pltpu.VMEM((nseq, SUBLANES, LANES), F32)],
        compiler_params=_cparams(("arbitrary",)), name=name)(z, z, *params)


def _sc_kernel(*refs, heads):
    _first_pass_only(functools.partial(_sc_body, *refs, heads=heads), refs[-4])


def _sc_body(x_ref, o_ref, cs_ref, c_ref, n_ref, m_ref, cw_ref, cb_ref, wq_ref, wk_ref, wv_ref, wif_ref, bif_ref,
             gn_ref, sk_ref, y_ref, co_ref, no_ref, mo_ref, cv_ref, *, heads):
    nb = x_ref.shape[0]
    dh = x_ref.shape[1] // heads
    row = lax.broadcasted_iota(jnp.int32, (nb, 1), 0)
    x = x_ref[...]
    xc = _silu(_step_conv(x, cs_ref, cw_ref, cb_ref))
    q, k, v, gates = _mlstm_qkv_gates(x, xc, wq_ref, wk_ref, wv_ref, wif_ref, bif_ref, dh)
    lf = _log_sigmoid(gates)
    m_old = m_ref[...]
    lane = lax.broadcasted_iota(jnp.int32, m_old.shape, 1)
    m_new = jnp.zeros_like(m_old)
    for h in range(heads):
        sl = slice(h * dh, (h + 1) * dh)
        ig = gates[:, h:h + 1]
        log_inter = lf[:, heads + h:heads + h + 1] + m_old[:, h:h + 1]
        m_t = jnp.maximum(log_inter, ig)
        d_m = jnp.exp(ig - m_t)
        w_inter = jnp.exp(log_inter - m_t)
        qh, kh, vh = q[:, sl], k[:, sl], v[:, sl]
        qb, kb, vb = qh.astype(BF16), kh.astype(BF16), vh.astype(BF16)
        sc = jnp.sum(qb.astype(F32) * kb.astype(F32), axis=-1, keepdims=True) * d_m
        n_old = n_ref[:, sl]
        kw = kh * d_m
        qc = jnp.zeros((nb, dh), F32)
        for i in range(nb):
            c_old = c_ref[i, h]
            qc = jnp.where(row == i, _dot(qb, c_old.astype(BF16)), qc)
            co_ref[i, h] = (w_inter[i:i + 1] * c_old
                            + _dot_tn(jnp.where(row == i, kw, 0.0), vh))
        num = sc * vb.astype(F32) + w_inter * qc
        den = sc + w_inter * jnp.sum(qh * n_old, axis=-1, keepdims=True)
        hh = num / jnp.maximum(jnp.abs(den), jnp.exp(-m_t))
        no_ref[:, sl] = w_inter * n_old + kw
        m_new = jnp.where(lane == h, m_t, m_new)
        hm = _sigmoid(o_ref[:, sl]) * hh
        y_ref[:, sl] = (_head_norm(hm) * gn_ref[:, sl] + sk_ref[:, sl] * xc[:, sl]).astype(y_ref.dtype)
    mo_ref[...] = m_new
    _step_conv_state(x, cs_ref, cv_ref)


def _sc_call(z, row0, bs, y_all, heads, xcol, cs, c_state, l, prev, n_state, m_state, p, *, name):
    c = p['m_conv_w'].shape[1]
    dh = c // heads
    nb = SUBLANES
    assert bs % nb == 0 and row0 % nb == 0
    r0 = row0 // nb
    full = lambda a: pl.BlockSpec(a.shape, lambda g, i: (0,) * a.ndim)
    params = [p['m_conv_w'], p['m_conv_b'], p['m_wq'], p['m_wk'], p['m_wv'], p['m_w_if'], p['m_b_if'],
              p['m_norm'], p['m_skip']]
    passes, blk, lay = _pass_maps(l, prev, bs // nb, c_state.shape[0])
    cs_spec = pl.BlockSpec((CONV_W - 1, nb, c), lambda g, i: (0, blk(g, i), 0))
    row_spec = lambda w: pl.BlockSpec((nb, w), lambda g, i: (blk(g, i), 0))
    specs, extra, drop, aliases = _alias_args(6 + len(params), [(y_all, 0), (prev, 1)])
    return pl.pallas_call(
        _skip_refs(functools.partial(_sc_kernel, heads=heads), drop),
        grid=(passes, bs // nb),
        in_specs=[pl.BlockSpec((nb, c), lambda g, i: (r0 + blk(g, i), xcol)),
                  pl.BlockSpec((nb, c), lambda g, i: (r0 + blk(g, i), xcol + 1)),
                  cs_spec,
                  pl.BlockSpec((None, nb, heads, dh, dh), lambda g, i: (l, blk(g, i), 0, 0, 0)),
                  row_spec(c), row_spec(LANES)] + [full(a) for a in params] + specs,
        out_specs=[pl.BlockSpec((nb, c), lambda g, i: (r0 + blk(g, i), 0)),
                   pl.BlockSpec((None, nb, heads, dh, dh), lambda g, i: (lay(g, i), i, 0, 0, 0)),
                   row_spec(c), row_spec(LANES), cs_spec],
        out_shape=[jax.ShapeDtypeStruct(y_all.shape, BF16), jax.ShapeDtypeStruct(c_state.shape, F32),
                   jax.ShapeDtypeStruct((bs, c), F32), jax.ShapeDtypeStruct((bs, LANES), F32),
                   jax.ShapeDtypeStruct(cs.shape, F32)],
        input_output_aliases=aliases,
        compiler_params=_cparams(("arbitrary", "arbitrary")), name=name)(
            z, z, cs, c_state, n_state, m_state, *params, *extra)


def _ffn_kernel(sbe_ref, sbs_ref, sbn_ref, nv_ref, xs_hbm, wg_ref, wu_ref, wd_ref, ys_hbm,
                acc_ref, xp_ref, yp_ref, h_ref, wgb_ref, wub_ref, wdb_ref, zero_ref, isem, osem, *, nka, nkb):
    del sbe_ref
    j, s = pl.program_id(0), pl.program_id(1)
    nblk = sbn_ref[j]
    blk0 = sbs_ref[j]
    max_blk = acc_ref.shape[0] // MOE_ROWS
    f = wgb_ref.shape[1]
    dn = wdb_ref.shape[0]

    @pl.when(jnp.logical_and(j == 0, s == 0))
    def _():
        zero_ref[...] = jnp.zeros_like(zero_ref)
        xp_ref[...] = jnp.zeros_like(xp_ref)

    def hbm_rows(ref, p):
        return ref.at[pl.ds(pl.multiple_of((blk0 + p) * MOE_ROWS, MOE_ROWS), MOE_ROWS), :]

    def vmem_rows(ref, p):
        return ref.at[pl.ds(p * MOE_ROWS, MOE_ROWS), :]

    def for_blocks(fn, n):
        for p in range(max_blk):
            @pl.when(p < n)
            def _():
                fn(p)

    def block_loop(body):
        per = FFN_ROWS // MOE_ROWS

        def step(b, carry):
            body(pl.ds(pl.multiple_of(b * FFN_ROWS, FFN_ROWS), FFN_ROWS))
            return carry
        lax.fori_loop(0, (nblk + per - 1) // per, step, 0)

    @pl.when(nblk > 0)
    def _():
        @pl.when(s == 0)
        def _():
            x_copy = lambda p: pltpu.make_async_copy(hbm_rows(xs_hbm, p), vmem_rows(xp_ref, p), isem)
            for_blocks(lambda p: x_copy(p).start(), nblk)
            for_blocks(lambda p: x_copy(p).wait(), nblk)

        for c in range(nka):
            @pl.when(s == c)
            def _(c=c):
                wgb_ref[...] = wg_ref[...].astype(BF16)
                wub_ref[...] = wu_ref[...].astype(BF16)

                def body(rows):
                    x = _unpack_half(xp_ref[rows, :], c).astype(BF16)
                    g = _dot(x, wgb_ref[...])
                    u = _dot(x, wub_ref[...])
                    if c > 0:
                        g = g + acc_ref[rows, 0:f]
                        u = u + acc_ref[rows, f:2 * f]
                    if c < nka - 1:
                        acc_ref[rows, 0:f] = g
                        acc_ref[rows, f:2 * f] = u
                    else:
                        h_ref[rows, :] = (_silu(g) * u).astype(BF16)
                block_loop(body)

        for c in range(nkb):
            @pl.when(s == nka + c)
            def _(c=c):
                wdb_ref[...] = wd_ref[...].astype(BF16)

                def body(rows):
                    y = _dot(h_ref[rows, c * dn:(c + 1) * dn], wdb_ref[...])
                    if c > 0:
                        y = y + acc_ref[rows, :]
                    acc_ref[rows, :] = y
                block_loop(body)

                if c == nkb - 1:
                    def pack_block(p):
                        sl = slice(p * MOE_ROWS, (p + 1) * MOE_ROWS)
                        yp_ref[sl, :] = _pack_halves(acc_ref[sl, :])
                    for_blocks(pack_block, nblk)
                    y_copy = lambda p: pltpu.make_async_copy(vmem_rows(yp_ref, p), hbm_rows(ys_hbm, p), osem)
                    for_blocks(lambda p: y_copy(p).start(), nblk)
                    for_blocks(lambda p: y_copy(p).wait(), nblk)

    @pl.when(jnp.logical_and(j == pl.num_programs(0) - 1, s == pl.num_programs(1) - 1))
    def _():
        _zero_blocks(zero_ref, ys_hbm, nv_ref[0], ys_hbm.shape[0] // MOE_ROWS, osem)


def _ffn_call(xs, tables, wg, wu, wd, l, *, name):
    sb_e, sb_blk0, sb_nblk, n_valid = tables
    ns, dp = xs.shape
    d, f = wg.shape[2], wg.shape[3]
    nsb = sb_e.shape[0]
    nka, nkb = d // FFN_GU_CHUNK, f // FFN_DN_CHUNK
    assert xs.dtype == U32 and dp * 2 == d and nka == 2 and FFN_GU_CHUNK == dp
    assert f % FFN_DN_CHUNK == 0 and 2 * f <= d and MOE_SB_ROWS % FFN_ROWS == 0 and FFN_ROWS % MOE_ROWS == 0

    def gu_map(j, s, sbe, sbs, sbn, nv):
        return (l, sbe[j], jnp.where(j < nv[1], jnp.minimum(s, nka - 1), nka - 1), 0)

    def dn_map(j, s, sbe, sbs, sbn, nv):
        prev = jnp.maximum(j - 1, 0)
        e = jnp.where(s >= nka, sbe[j], sbe[prev])
        c = jnp.where(j < nv[1], jnp.where(s >= nka, s - nka, jnp.where(j == 0, 0, nkb - 1)), nkb - 1)
        return (l, e, c, 0)

    grid_spec = pltpu.PrefetchScalarGridSpec(
        num_scalar_prefetch=4, grid=(nsb, nka + nkb),
        in_specs=[pl.BlockSpec(memory_space=pl.ANY),
                  pl.BlockSpec((None, None, FFN_GU_CHUNK, f), gu_map),
                  pl.BlockSpec((None, None, FFN_GU_CHUNK, f), gu_map),
                  pl.BlockSpec((None, None, FFN_DN_CHUNK, d), dn_map)],
        out_specs=pl.BlockSpec(memory_space=pl.ANY),
        scratch_shapes=[pltpu.VMEM((MOE_SB_ROWS, d), F32), pltpu.VMEM((MOE_SB_ROWS, dp), U32),
                        pltpu.VMEM((MOE_SB_ROWS, dp), U32), pltpu.VMEM((MOE_SB_ROWS, f), BF16),
                        pltpu.VMEM((FFN_GU_CHUNK, f), BF16), pltpu.VMEM((FFN_GU_CHUNK, f), BF16),
                        pltpu.VMEM((FFN_DN_CHUNK, d), BF16), pltpu.VMEM((MOE_ROWS, dp), U32),
                        pltpu.SemaphoreType.DMA(()), pltpu.SemaphoreType.DMA(())])
    return pl.pallas_call(
        functools.partial(_ffn_kernel, nka=nka, nkb=nkb), grid_spec=grid_spec,
        out_shape=jax.ShapeDtypeStruct((ns, dp), U32),
        compiler_params=_cparams(("arbitrary", "arbitrary")), name=name)(
            sb_e, sb_blk0, sb_nblk, n_valid, xs, wg, wu, wd)


def _take(table, idx):
    return jnp.sum(jnp.where(idx[:, None] == jnp.arange(table.shape[0])[None, :], table[None, :], 0), axis=1)


def _route_tables(ids, tile_cnt, r, tm):
    eidx = jnp.arange(N_EXPERTS, dtype=jnp.int32)
    cnt_t = tile_cnt[:, 0, :N_EXPERTS].astype(jnp.int32)
    counts = jnp.sum(cnt_t, axis=0)
    tile_off = jnp.cumsum(cnt_t, axis=0) - cnt_t
    padded = (counts + MOE_ROWS - 1) // MOE_ROWS * MOE_ROWS
    pad_end = jnp.cumsum(padded)
    pad_start = pad_end - padded
    base = jnp.repeat(pad_start[None, :] + tile_off, tm, axis=0)
    e = ids[:, :TOP_K]
    rank = ids[:, TOP_K:2 * TOP_K]
    dest = rank + jnp.sum(jnp.where(e[:, :, None] == eidx, base[:, None, :], 0), axis=-1)
    n_slots = (r * TOP_K // MOE_ROWS + N_EXPERTS) * MOE_ROWS
    nsb_e = (padded + MOE_SB_ROWS - 1) // MOE_SB_ROWS
    sb_end = jnp.cumsum(nsb_e)
    n_sb = sb_end[-1]
    nsb = n_slots // MOE_SB_ROWS + N_EXPERTS
    j = jnp.arange(nsb, dtype=jnp.int32)
    valid = j < n_sb
    e_j = jnp.minimum(jnp.sum(sb_end[None, :] <= j[:, None], axis=1), N_EXPERTS - 1)
    t = j - _take(sb_end - nsb_e, e_j)
    sb_blk0 = jnp.where(valid, (_take(pad_start, e_j) + t * MOE_SB_ROWS) // MOE_ROWS, 0)
    sb_nblk = jnp.where(valid, jnp.clip((_take(padded, e_j) - t * MOE_SB_ROWS) // MOE_ROWS,
                                        0, MOE_SB_ROWS // MOE_ROWS), 0)
    e_last = jnp.sum(jnp.where(j == n_sb - 1, e_j, 0))
    sb_e = jnp.where(valid, e_j, e_last)
    n_valid = jnp.stack([pad_end[-1] // MOE_ROWS, n_sb])
    i32 = lambda a: a.astype(jnp.int32)
    return i32(dest), i32(pad_end), (i32(sb_e), i32(sb_blk0), i32(sb_nblk), i32(n_valid)), n_slots


def _expand_blockdiag(w):
    nb, bs, _ = w.shape
    if bs == LANES:
        return w.astype(BF16)
    per = LANES // bs
    eye = jnp.eye(per, dtype=w.dtype)
    wt = w.reshape(nb // per, per, bs, bs)
    out = jnp.einsum('tpbc,pq->tpbqc', wt, eye)
    return out.reshape(nb // per, LANES, LANES).astype(BF16)


def _rope_tables(pos, half):
    freq = ROPE_BASE ** (-jnp.arange(half, dtype=F32) / half)
    ang = pos[:, None] * freq[None, :]
    cos, sin = jnp.cos(ang), jnp.sin(ang)
    return jnp.concatenate([cos, cos], axis=-1), jnp.concatenate([-sin, sin], axis=-1)


def kernel(x_prompt, x_sample, state_rglru_h, state_rglru_conv, state_ret, state_mlstm_C, state_mlstm_n,
           state_mlstm_m, state_mlstm_conv, meta_tokens, norm_mix, norm_ffn, norm_final, w_in, rg_conv_w,
           rg_conv_b, rg_wa, rg_ba, rg_wx, rg_bx, rg_lambda, ret_norm, m_conv_w, m_conv_b, m_wq, m_wk, m_wv,
           m_w_if, m_b_if, m_norm, m_skip, w_branch, w_out, moe_w_group, moe_b_group, moe_w_expert,
           moe_b_expert, moe_w_gate, moe_w_up, moe_w_down):
    bp, seq, d = x_prompt.shape
    bs = x_sample.shape[0]
    n_meta = meta_tokens.shape[0]
    depth = w_in.shape[0]
    d_rnn = state_rglru_h.shape[2]
    _, _, r_heads, r_dk, r_dv = state_ret.shape
    m_heads, m_dh = state_mlstm_C.shape[2], state_mlstm_C.shape[3]
    d_m = m_heads * m_dh
    assert seq % CHUNK == 0 and n_meta <= CHUNK and x_sample.shape[1] == 1
    pad = CHUNK - n_meta
    nch = 1 + seq // CHUNK
    rp = bp * nch * CHUNK
    r = rp + bs
    assert d_rnn == r_heads * r_dv == d_m
    xm_col = (d_rnn + 2 * r_heads * r_dk + 2 * r_heads * r_dv) // d_m
    gate_col0 = d_rnn + 2 * r_heads * r_dk + 2 * r_heads * r_dv + 2 * d_m

    pos_p = jnp.arange(nch * CHUNK, dtype=F32) - pad
    cos_p, sin_p = _rope_tables(pos_p, r_dk // 2)
    pos_s = jnp.full((1,), float(PAST_LEN), F32)
    cos_s, sin_s = _rope_tables(pos_s, r_dk // 2)

    tm = _pick_tile(r, 384, BF16_SUBLANES)
    assert rp % CHUNK == 0 and bs % BF16_SUBLANES == 0 and bs <= bp * CHUNK
    prompt_states, sample_states = [], []
    combine = None
    ret_s = c_s = None
    for l in range(depth):
        p = {'rg_conv_w': rg_conv_w[l], 'rg_conv_b': rg_conv_b[l].reshape(1, -1),
             'rg_wa': _expand_blockdiag(rg_wa[l]), 'rg_ba': rg_ba[l].reshape(1, -1),
             'rg_wx': _expand_blockdiag(rg_wx[l]), 'rg_bx': rg_bx[l].reshape(1, -1),
             'rg_lambda': rg_lambda[l].reshape(1, -1),
             'm_conv_w': m_conv_w[l], 'm_conv_b': m_conv_b[l].reshape(1, -1),
             'm_wq': _expand_blockdiag(m_wq[l]), 'm_wk': _expand_blockdiag(m_wk[l]),
             'm_wv': _expand_blockdiag(m_wv[l]),
             'm_w_if': jnp.pad(m_w_if[l], ((0, 0), (0, LANES - 2 * m_heads))).astype(BF16),
             'm_b_if': jnp.pad(m_b_if[l], (0, LANES - 2 * m_heads)).reshape(1, LANES),
             'm_norm': m_norm[l].reshape(1, -1), 'm_skip': m_skip[l].reshape(1, -1)}
        gn_ret = ret_norm[l].reshape(1, -1)

        if l == 0:
            x, h = _embed_norm_call(x_prompt, meta_tokens, x_sample.reshape(bs, d), norm_mix[l], nch,
                                    name=f'norm_mix{l}')
        else:
            x, h = _norm_call(x, norm_mix[l], tm=tm, grid=(r // tm,), row_tile=lambda i: i, want_x=True,
                              h_shape=(r, d), h_dtype=BF16, h_spec=pl.BlockSpec((tm, d), lambda i: (i, 0)),
                              combine=combine, name=f'norm_mix{l}')
        z = _mm_call(h, w_in, l, name=f'in_proj{l}')

        ya_p, rgh_p, rgc_p = _pa_call(z, bp, nch, pad, p, name=f'rglru_p{l}')
        yb_p, ret_p = _pb_call(z, bp, nch, pad, r_heads, r_dk, r_dv, cos_p, sin_p, gn_ret, name=f'ret_p{l}')
        yc_p, c_p, n_p, m_p, mc_p = _pc_call(z, bp, nch, pad, m_heads, xm_col, p, name=f'mlstm_p{l}')

        rg_cs = jnp.transpose(state_rglru_conv[l], (1, 0, 2))
        ya, rgh_s, rgc_s = _sa_call(z, rp, bs, ya_p, rg_cs, state_rglru_h[l], p, name=f'rglru_s{l}')
        yb, ret_s = _sb_call(z, rp, bs, yb_p, r_heads, r_dk, r_dv, state_ret, l, ret_s, cos_s, sin_s, gn_ret,
                             name=f'ret_s{l}')
        m_cs = jnp.transpose(state_mlstm_conv[l], (1, 0, 2))
        m_in = jnp.pad(state_mlstm_m[l], ((0, 0), (0, LANES - m_heads)))
        yc, c_s, n_s, m_s, mc_s = _sc_call(z, rp, bs, yc_p, m_heads, xm_col, m_cs, state_mlstm_C, l, c_s,
                                           state_mlstm_n[l].reshape(bs, d_m), m_in, p, name=f'mlstm_s{l}')

        prompt_states.append((rgh_p[:, 0], rgc_p, ret_p, c_p, n_p[:, :m_heads], m_p[:, 0, :m_heads], mc_p))
        sample_states.append((rgh_s, jnp.transpose(rgc_s, (1, 0, 2)), None, None,
                              n_s.reshape(bs, m_heads, m_dh), m_s[:, :m_heads], jnp.transpose(mc_s, (1, 0, 2))))

        merged = _merge_call((ya, yb, yc), z, w_branch, l, gate_col0, name=f'merge{l}')
        x = _mm_call(merged, w_out, l, res=x, name=f'out_proj{l}')

        wr = jnp.pad(jnp.concatenate([moe_w_group[l], moe_w_expert[l]], axis=1),
                     ((0, 0), (0, LANES - N_GROUPS - N_EXPERTS)))
        br = jnp.pad(jnp.concatenate([moe_b_group[l], moe_b_expert[l]]), (0, LANES - N_GROUPS - N_EXPERTS))
        h2, ids, wts, tile_cnt = _norm_call(
            x, norm_ffn[l], tm=tm, grid=(r // tm,), row_tile=lambda i: i, want_x=False, h_shape=(r, d // 2),
            h_dtype=U32, h_spec=pl.BlockSpec((tm, d // 2), lambda i: (i, 0)), router=(wr, br.reshape(1, LANES)),
            name=f'norm_router{l}')
        dest, pad_end, sb_tables, n_slots = _route_tables(ids, tile_cnt, r, tm)
        xs = _dispatch_call(h2, dest, pad_end, n_slots, tm=tm, name=f'moe_dispatch{l}')
        ys = _ffn_call(xs, sb_tables, moe_w_gate, moe_w_up, moe_w_down, l, name=f'moe_ffn{l}')
        combine = (dest, wts, ys)

    (y_prompt,) = _norm_call(
        x, norm_final, tm=CHUNK, grid=(bp, seq // CHUNK), row_tile=lambda b, c: (1 + c) * bp + b, want_x=False,
        h_shape=(bp, seq, d), h_dtype=F32, h_spec=pl.BlockSpec((None, CHUNK, d), lambda b, c: (b, c, 0)),
        combine=combine, name='norm_final_p')
    ts = _pick_tile(bs, CHUNK, SUBLANES)
    assert rp % ts == 0
    (y_sample,) = _norm_call(
        x, norm_final, tm=ts, grid=(bs // ts,), row_tile=lambda i: rp // ts + i, want_x=False,
        h_shape=(bs, d), h_dtype=F32, h_spec=pl.BlockSpec((ts, d), lambda i: (i, 0)),
        combine=combine, name='norm_final_s')
    pn = [jnp.stack([s[i] for s in prompt_states], axis=0) for i in range(7)]
    sn = [None if sample_states[0][i] is None else jnp.stack([s[i] for s in sample_states], axis=0)
          for i in range(7)]
    sn[2], sn[3] = ret_s, c_s
    return (y_prompt, y_sample.reshape(bs, 1, d), *pn, *sn)
```

```python
import functools
import math

import jax
import jax.numpy as jnp
import numpy as np
from jax import lax
from jax.experimental import pallas as pl
from jax.experimental.pallas import tpu as pltpu

F32 = jnp.float32
BF16 = jnp.bfloat16

LANES = 128
SUBLANES = 8
BF16_SUBLANES = 16
VMEM_LIMIT_BYTES = 56 * 1024 * 1024

CHUNK = 128
CONV_W = 4
EPS = 1e-6
RG_C = 8.0
ROPE_BASE = 10000.0
PAST_LEN = 16384
N_GROUPS = 4
EXP_PER_GROUP = 8
N_EXPERTS = N_GROUPS * EXP_PER_GROUP
TOP_K = 2
MOE_ROWS = 128
MOE_SB_ROWS = 1024
FFN_ROWS = 256
FFN_GU_CHUNK = 1024
FFN_DN_CHUNK = 512
NEG = -1e30


def _cparams(sem, vmem=VMEM_LIMIT_BYTES):
    return pltpu.CompilerParams(dimension_semantics=sem, vmem_limit_bytes=vmem)


def _pick_tile(n, cap, mult):
    best = None
    for t in range(mult, min(n, cap) + 1, mult):
        if n % t == 0:
            best = t
    assert best is not None, (n, cap, mult)
    return best


def _skip_refs(fn, idxs):
    def wrapped(*refs):
        return fn(*[r for i, r in enumerate(refs) if i not in idxs])
    return wrapped


def _alias_args(n_in, aliased):
    arrs = [(a, o) for a, o in aliased if a is not None]
    idxs = list(range(n_in, n_in + len(arrs)))
    return ([pl.BlockSpec(memory_space=pl.ANY)] * len(arrs), [a for a, _ in arrs], idxs,
            {i: o for i, (_, o) in zip(idxs, arrs)})


def _dot(a, b):
    return jnp.dot(a, b, preferred_element_type=F32)


def _dot_nt(a, b):
    return lax.dot_general(a, b, (((1,), (1,)), ((), ())), preferred_element_type=F32)


def _dot_tn(a, b):
    return lax.dot_general(a, b, (((0,), (0,)), ((), ())), preferred_element_type=F32)


def _split3(x):
    hi = x.astype(BF16)
    r1 = x - hi.astype(F32)
    mid = r1.astype(BF16)
    lo = (r1 - mid.astype(F32)).astype(BF16)
    return hi, mid, lo


U32 = jnp.uint32


def _pack_halves(x):
    n = x.shape[1] // 2
    lo = lax.bitcast_convert_type(x[:, :n].astype(BF16).astype(F32), U32)
    hi = lax.bitcast_convert_type(x[:, n:].astype(BF16).astype(F32), U32)
    return (lo >> 16) | (hi & U32(0xFFFF0000))


def _unpack_half(p, half):
    bits = (p << 16) if half == 0 else (p & U32(0xFFFF0000))
    return lax.bitcast_convert_type(bits, F32)


def _dot_hi(a, b):
    return jnp.dot(a, b, preferred_element_type=F32, precision=lax.Precision.HIGHEST)


def _sigmoid(x):
    return 1.0 / (1.0 + jnp.exp(-x))


def _silu(x):
    return x * _sigmoid(x)


def _log_sigmoid(x):
    return jnp.minimum(x, 0.0) - jnp.log(1.0 + jnp.exp(-jnp.abs(x)))


def _softplus(x):
    return jnp.maximum(x, 0.0) + jnp.log(1.0 + jnp.exp(-jnp.abs(x)))


def _head_norm(o):
    mu = jnp.mean(o, axis=-1, keepdims=True)
    d = o - mu
    var = jnp.mean(d * d, axis=-1, keepdims=True)
    return d * lax.rsqrt(var + EPS)


def _blockdiag(x, w_ref, scale=None):
    outs = []
    for j in range(w_ref.shape[0]):
        o = _dot(x[:, j * LANES:(j + 1) * LANES].astype(BF16), w_ref[j])
        outs.append(o if scale is None else o * scale)
    return jnp.concatenate(outs, axis=-1)


ROW_DMA_UNROLL = 8


def _row_copies(n_rows, row_copy):
    def start(r, carry):
        for k in range(TOP_K):
            row_copy(r, k).start(priority=k % 2)
        return carry

    def wait(r, carry):
        for k in range(TOP_K):
            row_copy(r, k).wait()
        return carry

    lax.fori_loop(0, n_rows, start, 0, unroll=ROW_DMA_UNROLL)
    lax.fori_loop(0, n_rows, wait, 0, unroll=ROW_DMA_UNROLL)


def _norm_kernel(*refs, combine, want_x, router):
    refs = list(refs)
    if combine:
        dest_ref, wts_in_ref, yb_hbm = refs[:3]
        refs = refs[3:]
    x_ref, g_ref = refs[:2]
    refs = refs[2:]
    if router:
        wr_ref, br_ref = refs[:2]
        refs = refs[2:]
    n_out = int(want_x) + 1 + (3 if router else 0)
    outs, scratch = refs[:n_out], refs[n_out:]
    x = x_ref[...]
    tm = x.shape[0]
    if combine:
        gbuf, gsem = scratch
        _row_copies(tm, lambda r, k: pltpu.make_async_copy(
            yb_hbm.at[pl.ds(dest_ref[0, TOP_K * r + k], 1), :], gbuf.at[k, pl.ds(r, 1), :], gsem))
        for k in range(TOP_K):
            y = jnp.concatenate([_unpack_half(gbuf[k], 0), _unpack_half(gbuf[k], 1)], axis=-1)
            x = x + wts_in_ref[:, k:k + 1] * y
    k = 0
    if want_x:
        outs[k][...] = x
        k += 1
    ms = jnp.mean(x * x, axis=-1, keepdims=True)
    h = x * lax.rsqrt(ms + EPS) * g_ref[...]
    outs[k][...] = _pack_halves(h) if router else h.astype(outs[k].dtype)
    k += 1
    if router:
        ids_ref, wts_ref, cnt_ref = outs[k], outs[k + 1], outs[k + 2]
        logits = _dot_hi(h, wr_ref[...]) + br_ref[...]
        lane = lax.broadcasted_iota(jnp.int32, logits.shape, 1)
        big = jnp.int32(1 << 20)
        is_g = lane < N_GROUPS
        gl = jnp.where(is_g, logits, NEG)
        gmax = jnp.max(gl, axis=-1, keepdims=True)
        gidx = jnp.min(jnp.where(gl == gmax, lane, big), axis=-1, keepdims=True)
        gsum = jnp.sum(jnp.where(is_g, jnp.exp(gl - gmax), 0.0), axis=-1, keepdims=True)
        gprob = 1.0 / gsum
        lo = N_GROUPS + EXP_PER_GROUP * gidx
        em = jnp.where(lane >= lo, jnp.where(lane < lo + EXP_PER_GROUP, logits, NEG), NEG)
        e1v = jnp.max(em, axis=-1, keepdims=True)
        e1i = jnp.min(jnp.where(em == e1v, lane, big), axis=-1, keepdims=True)
        em2 = jnp.where(lane == e1i, NEG, em)
        e2v = jnp.max(em2, axis=-1, keepdims=True)
        e2i = jnp.min(jnp.where(em2 == e2v, lane, big), axis=-1, keepdims=True)
        t = jnp.exp(e2v - e1v)
        w1 = gprob / (1.0 + t)
        w2 = gprob * t / (1.0 + t)
        e1, e2 = e1i - N_GROUPS, e2i - N_GROUPS
        chosen = jnp.where(lane == e1, 1.0, jnp.where(lane == e2, 1.0, 0.0))
        ti = lax.broadcasted_iota(jnp.int32, (tm, tm), 0)
        si = lax.broadcasted_iota(jnp.int32, (tm, tm), 1)
        before = _dot(jnp.where(ti > si, 1.0, 0.0).astype(BF16), chosen.astype(BF16))
        rank1 = jnp.sum(jnp.where(lane == e1, before, 0.0), axis=-1, keepdims=True).astype(jnp.int32)
        rank2 = jnp.sum(jnp.where(lane == e2, before, 0.0), axis=-1, keepdims=True).astype(jnp.int32)
        ids_ref[...] = jnp.where(lane == 0, e1, jnp.where(lane == 1, e2,
                                 jnp.where(lane == 2, rank1, jnp.where(lane == 3, rank2, 0))))
        wts_ref[...] = jnp.where(lane == 0, w1, jnp.where(lane == 1, w2, 0.0))
        cnt_ref[...] = jnp.sum(chosen, axis=0, keepdims=True)


def _norm_call(x, g, *, tm, grid, row_tile, want_x, h_shape, h_dtype, h_spec, combine=None, router=None, name):
    r, d = x.shape
    x_spec = pl.BlockSpec((tm, d), lambda *gi: (row_tile(*gi), 0))
    in_specs, args, scratch = [], [], []
    if combine is not None:
        dest, wts, yb = combine
        in_specs += [pl.BlockSpec((None, 1, TOP_K * tm), lambda *gi: (row_tile(*gi), 0, 0), memory_space=pltpu.SMEM),
                     pl.BlockSpec((tm, LANES), lambda *gi: (row_tile(*gi), 0)),
                     pl.BlockSpec(memory_space=pl.ANY)]
        args += [dest.reshape(r // tm, 1, TOP_K * tm), wts, yb]
        assert yb.dtype == U32 and yb.shape[1] * 2 == d
        scratch = [pltpu.VMEM((TOP_K, tm, d // 2), U32), pltpu.SemaphoreType.DMA(())]
    in_specs += [x_spec, pl.BlockSpec((1, d), lambda *gi: (0, 0))]
    args += [x, g.reshape(1, d)]
    if router is not None:
        in_specs += [pl.BlockSpec((d, LANES), lambda *gi: (0, 0)), pl.BlockSpec((1, LANES), lambda *gi: (0, 0))]
        args += list(router)
    out_shape, out_specs = [], []
    if want_x:
        out_shape.append(jax.ShapeDtypeStruct((r, d), F32))
        out_specs.append(x_spec)
    out_shape.append(jax.ShapeDtypeStruct(h_shape, h_dtype))
    out_specs.append(h_spec)
    if router is not None:
        out_shape += [jax.ShapeDtypeStruct((r, LANES), jnp.int32), jax.ShapeDtypeStruct((r, LANES), F32),
                      jax.ShapeDtypeStruct((r // tm, 1, LANES), F32)]
        out_specs += [pl.BlockSpec((tm, LANES), lambda *gi: (row_tile(*gi), 0))] * 2
        out_specs += [pl.BlockSpec((None, 1, LANES), lambda *gi: (row_tile(*gi), 0, 0))]
    return pl.pallas_call(
        functools.partial(_norm_kernel, combine=combine is not None, want_x=want_x, router=router is not None),
        grid=grid, in_specs=in_specs, out_specs=out_specs, out_shape=out_shape, scratch_shapes=scratch,
        compiler_params=_cparams(("arbitrary",) * len(grid)), name=name)(*args)


def _embed_norm_kernel(xp_ref, meta_ref, xs_ref, g_ref, x_ref, h_ref, *, n_seq, n_prompt_tiles):
    i = pl.program_id(0)
    c = i // n_seq
    tm, d = x_ref.shape

    def finish(x):
        x_ref[...] = x
        ms = jnp.mean(x * x, axis=-1, keepdims=True)
        h_ref[...] = (x * lax.rsqrt(ms + EPS) * g_ref[...]).astype(h_ref.dtype)

    @pl.when(i >= n_prompt_tiles)
    def _():
        finish(xs_ref[...])

    @pl.when(jnp.logical_and(i < n_prompt_tiles, c == 0))
    def _():
        finish(jnp.concatenate([jnp.zeros((tm - meta_ref.shape[0], d), F32), meta_ref[...]], axis=0))

    @pl.when(jnp.logical_and(i < n_prompt_tiles, c > 0))
    def _():
        finish(xp_ref[...])


def _embed_norm_call(x_prompt, meta, x_sample, g, nch, *, name):
    bp, seq, d = x_prompt.shape
    bs = x_sample.shape[0]
    npt = bp * nch
    assert bs % CHUNK == 0 and meta.shape[0] % SUBLANES == 0
    r = npt * CHUNK + bs
    row_spec = pl.BlockSpec((CHUNK, d), lambda i: (i, 0))
    return pl.pallas_call(
        functools.partial(_embed_norm_kernel, n_seq=bp, n_prompt_tiles=npt), grid=(r // CHUNK,),
        in_specs=[pl.BlockSpec((None, CHUNK, d),
                               lambda i: (lax.rem(i, bp), jnp.clip(i // bp - 1, 0, nch - 2), 0)),
                  pl.BlockSpec(meta.shape, lambda i: (0, 0)),
                  pl.BlockSpec((CHUNK, d), lambda i: (jnp.maximum(i - npt, 0), 0)),
                  pl.BlockSpec((1, d), lambda i: (0, 0))],
        out_specs=[row_spec, row_spec],
        out_shape=[jax.ShapeDtypeStruct((r, d), F32), jax.ShapeDtypeStruct((r, d), BF16)],
        compiler_params=_cparams(("arbitrary",)), name=name)(x_prompt, meta, x_sample, g.reshape(1, d))


def _zero_blocks(zero_ref, out_hbm, first, last, sem):
    def blk_copy(b):
        return pltpu.make_async_copy(zero_ref, out_hbm.at[pl.ds(pl.multiple_of(b * MOE_ROWS, MOE_ROWS), MOE_ROWS), :], sem)

    def start(b, carry):
        blk_copy(b).start()
        return carry

    def wait(b, carry):
        blk_copy(b).wait()
        return carry

    lax.fori_loop(first, last, start, 0)
    lax.fori_loop(first, last, wait, 0)


def _dispatch_kernel(pe_ref, dest_ref, h_ref, xs_out, zero_ref, sem, zsem):
    tm = h_ref.shape[0]
    n_blocks = xs_out.shape[0] // MOE_ROWS

    @pl.when(pl.program_id(0) == 0)
    def _():
        zero_ref[...] = jnp.zeros_like(zero_ref)

        def last_block(e):
            b = jnp.maximum(pe_ref[e] // MOE_ROWS - 1, 0)
            return pltpu.make_async_copy(
                zero_ref, xs_out.at[pl.ds(pl.multiple_of(b * MOE_ROWS, MOE_ROWS), MOE_ROWS), :], zsem)

        for e in range(N_EXPERTS):
            last_block(e).start()
        for e in range(N_EXPERTS):
            last_block(e).wait()
        _zero_blocks(zero_ref, xs_out, pe_ref[N_EXPERTS - 1] // MOE_ROWS, n_blocks, zsem)

    _row_copies(tm, lambda r, k: pltpu.make_async_copy(
        h_ref.at[pl.ds(r, 1), :], xs_out.at[pl.ds(dest_ref[0, TOP_K * r + k], 1), :], sem))


def _dispatch_call(h2, dest, pad_end, n_slots, *, tm, name):
    r, d = h2.shape
    grid_spec = pltpu.PrefetchScalarGridSpec(
        num_scalar_prefetch=1, grid=(r // tm,),
        in_specs=[pl.BlockSpec((None, 1, TOP_K * tm), lambda i, pe: (i, 0, 0), memory_space=pltpu.SMEM),
                  pl.BlockSpec((tm, d), lambda i, pe: (i, 0))],
        out_specs=pl.BlockSpec(memory_space=pl.ANY),
        scratch_shapes=[pltpu.VMEM((MOE_ROWS, d), h2.dtype), pltpu.SemaphoreType.DMA(()),
                        pltpu.SemaphoreType.DMA(())])
    return pl.pallas_call(
        _dispatch_kernel, grid_spec=grid_spec, out_shape=jax.ShapeDtypeStruct((n_slots, d), h2.dtype),
        compiler_params=_cparams(("arbitrary",)), name=name)(pad_end, dest.reshape(r // tm, 1, TOP_K * tm), h2)


def _mm_kernel(*refs, has_res):
    if has_res:
        x_ref, w_ref, r_ref, o_ref, wb_ref = refs
    else:
        x_ref, w_ref, o_ref, wb_ref = refs

    @pl.when(pl.program_id(1) == 0)
    def _():
        wb_ref[...] = w_ref[...].astype(BF16)

    acc = _dot(x_ref[...], wb_ref[...])
    if has_res:
        acc = acc + r_ref[...]
    o_ref[...] = acc.astype(o_ref.dtype)


def _mm_call(x, w, l, res=None, *, out_dtype=F32, name):
    r, k = x.shape
    n = w.shape[2]
    tm = _pick_tile(r, 1152, BF16_SUBLANES)
    tn = _pick_tile(n, 1024, LANES)
    in_specs = [pl.BlockSpec((tm, k), lambda j, i: (i, 0)), pl.BlockSpec((None, k, tn), lambda j, i: (l, 0, j))]
    args = [x, w]
    if res is not None:
        in_specs.append(pl.BlockSpec((tm, tn), lambda j, i: (i, j)))
        args.append(res)
    return pl.pallas_call(
        functools.partial(_mm_kernel, has_res=res is not None),
        grid=(n // tn, r // tm), in_specs=in_specs,
        out_specs=pl.BlockSpec((tm, tn), lambda j, i: (i, j)),
        out_shape=jax.ShapeDtypeStruct((r, n), out_dtype),
        scratch_shapes=[pltpu.VMEM((k, tn), BF16)],
        compiler_params=_cparams(("arbitrary", "arbitrary")), name=name)(*args)


def _merge_kernel(*refs, n_branch):
    y_refs, g_refs = refs[:n_branch], refs[n_branch:2 * n_branch]
    w_ref, o_ref, wb_ref = refs[2 * n_branch:]

    @pl.when(pl.program_id(1) == 0)
    def _():
        wb_ref[...] = w_ref[...].astype(BF16)

    acc = _sigmoid(g_refs[0][...]) * _dot(y_refs[0][...], wb_ref[0])
    for n in range(1, n_branch):
        acc = acc + _sigmoid(g_refs[n][...]) * _dot(y_refs[n][...], wb_ref[n])
    o_ref[...] = acc.astype(o_ref.dtype)


def _merge_call(ys, z, w_branch, l, gate_col0, *, name):
    r = z.shape[0]
    db = ys[0].shape[1]
    _, nbr, _, d = w_branch.shape
    tn = _pick_tile(d, 1024, LANES)
    tm = _pick_tile(r, 384, BF16_SUBLANES)
    assert gate_col0 % tn == 0 and d % tn == 0 and all(y.shape[0] >= r for y in ys)
    g0 = gate_col0 // tn
    per = d // tn
    y_spec = pl.BlockSpec((tm, db), lambda j, i: (i, 0))
    g_specs = [pl.BlockSpec((tm, tn), lambda j, i, n=n: (i, g0 + per * n + j)) for n in range(nbr)]
    return pl.pallas_call(
        functools.partial(_merge_kernel, n_branch=nbr), grid=(d // tn, r // tm),
        in_specs=[y_spec] * nbr + g_specs + [pl.BlockSpec((None, nbr, db, tn), lambda j, i: (l, 0, 0, j))],
        out_specs=pl.BlockSpec((tm, tn), lambda j, i: (i, j)),
        out_shape=jax.ShapeDtypeStruct((r, d), BF16),
        scratch_shapes=[pltpu.VMEM((nbr, db, tn), BF16)],
        compiler_params=_cparams(("arbitrary", "arbitrary")), name=name)(*ys, *([z] * nbr), w_branch)


def _shift_rows(x, prev8, k, row):
    rolled = pltpu.roll(x, k, 0)
    head = jnp.where(row[:SUBLANES] < k, pltpu.roll(prev8, k, 0), rolled[:SUBLANES])
    return jnp.concatenate([head, rolled[SUBLANES:]], axis=0)


def _chunk_conv(x, prev8, w_ref, b_ref, row):
    y = b_ref[...] + w_ref[CONV_W - 1:CONV_W, :] * x
    for k in range(1, CONV_W):
        y = y + w_ref[CONV_W - 1 - k:CONV_W - k, :] * _shift_rows(x, prev8, k, row)
    return y


def _rglru_coeffs(xc, wa_ref, ba_ref, wx_ref, bx_ref, lam_ref):
    r = _sigmoid(_blockdiag(xc, wa_ref) + ba_ref[...])
    i = _sigmoid(_blockdiag(xc, wx_ref) + bx_ref[...])
    log_a = -RG_C * r * _softplus(-lam_ref[...])
    a = jnp.exp(log_a)
    mult = jnp.sqrt(1.0 - a * a)
    return a, mult * i * xc


def _per_sequence(body, nseq, n_rows_in, n_const, n_rows_out, n_state_out):
    def kernel(*refs):
        rows_in = refs[:n_rows_in]
        consts = refs[n_rows_in:n_rows_in + n_const]
        o = n_rows_in + n_const
        rows_out = refs[o:o + n_rows_out]
        state_out = refs[o + n_rows_out:o + n_rows_out + n_state_out]
        scratch = refs[o + n_rows_out + n_state_out:]
        L = rows_in[0].shape[0] // nseq

        @pl.when(pl.program_id(0) <= _last_chunk())
        def _():
            for b in range(nseq):
                rows = lambda r, b=b: r.at[pl.ds(b * L, L), :]
                body(*[rows(r) for r in rows_in], *consts, *[rows(r) for r in rows_out],
                     *[r.at[pl.ds(b, 1)] for r in state_out], *[r.at[b] for r in scratch])

        @pl.when(pl.program_id(0) > _last_chunk())
        def _():
            for r in rows_out:
                r[...] = jnp.zeros_like(r)
    return kernel


def _last_chunk():
    return pl.num_programs(0) - 2


def _pa_kernel(x_ref, cw_ref, cb_ref, wa_ref, ba_ref, wx_ref, bx_ref, lam_ref,
               y_ref, hl_ref, cv_ref, prev_ref, h_ref, *, pad):
    c = pl.program_id(0)
    last = _last_chunk()
    L = x_ref.shape[0]
    row = lax.broadcasted_iota(jnp.int32, (L, 1), 0)
    valid = jnp.logical_or(c > 0, row >= pad)

    @pl.when(c == 0)
    def _():
        prev_ref[...] = jnp.zeros_like(prev_ref)
        h_ref[...] = jnp.zeros_like(h_ref)

    x = jnp.where(valid, x_ref[...], 0.0)
    xc = _chunk_conv(x, prev_ref[...], cw_ref, cb_ref, row)
    prev_ref[...] = x[L - SUBLANES:]
    a, b = _rglru_coeffs(xc, wa_ref, ba_ref, wx_ref, bx_ref, lam_ref)
    a = jnp.where(valid, a, 1.0)
    b = jnp.where(valid, b, 0.0)
    hs = []
    for j in range(x.shape[1] // LANES):
        sl = slice(j * LANES, (j + 1) * LANES)
        aj, bj = a[:, sl], b[:, sl]
        d = 1
        while d < L:
            keep = row >= d
            a_s = jnp.where(keep, pltpu.roll(aj, d, 0), 1.0)
            b_s = jnp.where(keep, pltpu.roll(bj, d, 0), 0.0)
            bj = aj * b_s + bj
            aj = aj * a_s
            d *= 2
        hs.append(aj * h_ref[0:1, sl] + bj)
    h = jnp.concatenate(hs, axis=-1)
    h_ref[0:1, :] = h[L - 1:L]
    y_ref[...] = h.astype(y_ref.dtype)

    @pl.when(c == last)
    def _():
        hl_ref[0] = h[L - 1:L]
        cv_ref[0] = x[L - (CONV_W - 1):]


def _pa_call(z, nseq, nch, pad, p, *, name):
    rows = nseq * nch * CHUNK
    c = p['rg_conv_w'].shape[1]
    full = lambda a: pl.BlockSpec(a.shape, lambda k: (0,) * a.ndim)
    whole = lambda *shape: pl.BlockSpec(shape, lambda k: (0,) * len(shape))
    params = [p['rg_conv_w'], p['rg_conv_b'], p['rg_wa'], p['rg_ba'], p['rg_wx'], p['rg_bx'], p['rg_lambda']]
    return pl.pallas_call(
        _per_sequence(functools.partial(_pa_kernel, pad=pad), nseq, 1, len(params), 1, 2), grid=(nch + 1,),
        in_specs=[pl.BlockSpec((nseq * CHUNK, c), lambda k: (jnp.minimum(k, nch - 1), 0))]
        + [full(a) for a in params],
        out_specs=[pl.BlockSpec((nseq * CHUNK, c), lambda k: (k, 0)),
                   whole(nseq, 1, c), whole(nseq, CONV_W - 1, c)],
        out_shape=[jax.ShapeDtypeStruct((rows + nseq * CHUNK, c), BF16),
                   jax.ShapeDtypeStruct((nseq, 1, c), F32),
                   jax.ShapeDtypeStruct((nseq, CONV_W - 1, c), F32)],
        scratch_shapes=[pltpu.VMEM((nseq, SUBLANES, c), F32), pltpu.VMEM((nseq, SUBLANES, c), F32)],
        compiler_params=_cparams(("arbitrary",)), name=name)(z, *params)


def _step_conv(x, cs_ref, w_ref, b_ref):
    y = b_ref[...] + w_ref[CONV_W - 1:CONV_W, :] * x
    for k in range(CONV_W - 1):
        y = y + w_ref[k:k + 1, :] * cs_ref[k]
    return y


def _step_conv_state(x, cs_ref, out_ref):
    for k in range(CONV_W - 2):
        out_ref[k] = cs_ref[k + 1]
    out_ref[CONV_W - 2] = x


def _sa_kernel(x_ref, cs_ref, h0_ref, cw_ref, cb_ref, wa_ref, ba_ref, wx_ref, bx_ref, lam_ref,
               y_ref, hn_ref, cv_ref):
    x = x_ref[...]
    xc = _step_conv(x, cs_ref, cw_ref, cb_ref)
    a, b = _rglru_coeffs(xc, wa_ref, ba_ref, wx_ref, bx_ref, lam_ref)
    h = a * h0_ref[...] + b
    hn_ref[...] = h
    y_ref[...] = h.astype(y_ref.dtype)
    _step_conv_state(x, cs_ref, cv_ref)


def _sa_call(z, row0, bs, y_all, cs, h0, p, *, name):
    c = h0.shape[1]
    assert row0 % bs == 0
    full = lambda a: pl.BlockSpec(a.shape, lambda i: (0,) * a.ndim)
    params = [p['rg_conv_w'], p['rg_conv_b'], p['rg_wa'], p['rg_ba'], p['rg_wx'], p['rg_bx'], p['rg_lambda']]
    specs, extra, drop, aliases = _alias_args(3 + len(params), [(y_all, 0)])
    return pl.pallas_call(
        _skip_refs(_sa_kernel, drop), grid=(1,),
        in_specs=[pl.BlockSpec((bs, c), lambda i: (row0 // bs, 0)), full(cs), full(h0)] + [full(a) for a in params]
        + specs,
        out_specs=[pl.BlockSpec((bs, c), lambda i: (row0 // bs, 0)), pl.BlockSpec((bs, c), lambda i: (0, 0)),
                   pl.BlockSpec(cs.shape, lambda i: (0, 0, 0))],
        out_shape=[jax.ShapeDtypeStruct(y_all.shape, BF16), jax.ShapeDtypeStruct((bs, c), F32),
                   jax.ShapeDtypeStruct(cs.shape, F32)],
        input_output_aliases=aliases,
        compiler_params=_cparams(("arbitrary",)), name=name)(z, cs, h0, *params, *extra)


def _ret_log_g(h):
    return math.log1p(-(2.0 ** (-5.0 - h)))


def _rope(x, cos2, sin2):
    return x * cos2 + pltpu.roll(x, x.shape[1] // 2, 1) * sin2


def _pb_kernel(q_ref, k_ref, v_ref, g_ref, cos_ref, sin_ref, gn_ref, y_ref, so_ref, s_ref, *, pad, heads):
    c = pl.program_id(0)
    last = _last_chunk()
    L = q_ref.shape[0]
    dk = q_ref.shape[1] // heads
    dv = v_ref.shape[1] // heads
    row = lax.broadcasted_iota(jnp.int32, (L, 1), 0)
    valid = jnp.logical_or(c > 0, row >= pad)

    @pl.when(c == 0)
    def _():
        s_ref[...] = jnp.zeros_like(s_ref)

    t = row.astype(F32)
    rel = t - lax.broadcasted_iota(jnp.int32, (1, L), 1).astype(F32)
    cos2, sin2 = cos_ref[...], sin_ref[...]
    for h in range(heads):
        lg = _ret_log_g(h)
        qh = jnp.where(valid, _rope(q_ref[:, h * dk:(h + 1) * dk], cos2, sin2), 0.0)
        kh = jnp.where(valid, _rope(k_ref[:, h * dk:(h + 1) * dk], cos2, sin2), 0.0) * (dk ** -0.5)
        vh = jnp.where(valid, v_ref[:, h * dv:(h + 1) * dv], 0.0)
        decay = jnp.where(rel >= 0, jnp.exp(jnp.maximum(rel, 0.0) * lg), 0.0)
        qb, vb = qh.astype(BF16), vh.astype(BF16)
        scores = _dot_nt(qb, kh.astype(BF16)) * decay
        s_old = s_ref[h]
        o = _dot(scores.astype(BF16), vb) + _dot(qb, s_old.astype(BF16)) * jnp.exp((t + 1.0) * lg)
        k_dec = kh * jnp.exp((L - 1.0 - t) * lg)
        s_ref[h] = math.exp(L * lg) * s_old + _dot_tn(k_dec.astype(BF16), vb)
        sl = slice(h * dv, (h + 1) * dv)
        y_ref[:, sl] = (_head_norm(o) * gn_ref[:, sl] * _silu(g_ref[:, sl])).astype(y_ref.dtype)

    @pl.when(c == last)
    def _():
        so_ref[0] = s_ref[...]


def _pb_call(z, nseq, nch, pad, heads, dk, dv, cos2, sin2, ret_norm, *, name):
    rows = nseq * nch * CHUNK
    hk, hv = heads * dk, heads * dv
    assert hv % hk == 0
    q0 = hv // hk
    blk = nseq * CHUNK
    return pl.pallas_call(
        _per_sequence(functools.partial(_pb_kernel, pad=pad, heads=heads), nseq, 4, 3, 1, 1), grid=(nch + 1,),
        in_specs=[pl.BlockSpec((blk, hk), lambda k: (jnp.minimum(k, nch - 1), q0)),
                  pl.BlockSpec((blk, hk), lambda k: (jnp.minimum(k, nch - 1), q0 + 1)),
                  pl.BlockSpec((blk, hv), lambda k: (jnp.minimum(k, nch - 1), 2)),
                  pl.BlockSpec((blk, hv), lambda k: (jnp.minimum(k, nch - 1), 3)),
                  pl.BlockSpec((CHUNK, dk), lambda k: (jnp.minimum(k, nch - 1), 0)),
                  pl.BlockSpec((CHUNK, dk), lambda k: (jnp.minimum(k, nch - 1), 0)),
                  pl.BlockSpec((1, hv), lambda k: (0, 0))],
        out_specs=[pl.BlockSpec((blk, hv), lambda k: (k, 0)),
                   pl.BlockSpec((nseq, heads, dk, dv), lambda k: (0, 0, 0, 0))],
        out_shape=[jax.ShapeDtypeStruct((rows + blk, hv), BF16),
                   jax.ShapeDtypeStruct((nseq, heads, dk, dv), F32)],
        scratch_shapes=[pltpu.VMEM((nseq, heads, dk, dv), F32)],
        compiler_params=_cparams(("arbitrary",)), name=name)(z, z, z, z, cos2, sin2, ret_norm)


def _first_pass_only(body, state_out_ref):
    @pl.when(pl.program_id(0) == 0)
    def _():
        body()

    @pl.when(pl.program_id(0) > 0)
    def _():
        state_out_ref[...] = jnp.zeros_like(state_out_ref)


def _pass_maps(l, prev, nblk, depth):
    if prev is not None:
        return 1, (lambda g, i: i), (lambda g, i: l)
    assert l == 0
    return depth, (lambda g, i: jnp.where(g == 0, i, nblk - 1)), (lambda g, i: g)


def _sb_kernel(*refs, heads):
    _first_pass_only(functools.partial(_sb_body, *refs, heads=heads), refs[-1])


def _sb_body(q_ref, k_ref, v_ref, g_ref, s_ref, cos_ref, sin_ref, gn_ref, y_ref, so_ref, *, heads):
    nb = q_ref.shape[0]
    dk = q_ref.shape[1] // heads
    dv = v_ref.shape[1] // heads
    row = lax.broadcasted_iota(jnp.int32, (nb, 1), 0)
    cos2, sin2 = cos_ref[...], sin_ref[...]
    for h in range(heads):
        g = math.exp(_ret_log_g(h))
        qh = _rope(q_ref[:, h * dk:(h + 1) * dk], cos2, sin2)
        kh = _rope(k_ref[:, h * dk:(h + 1) * dk], cos2, sin2) * (dk ** -0.5)
        vh = v_ref[:, h * dv:(h + 1) * dv]
        qb, vb = qh.astype(BF16), vh.astype(BF16)
        cross = jnp.zeros((nb, dv), F32)
        for i in range(nb):
            s_old = s_ref[i, h]
            cross = jnp.where(row == i, _dot(qb, s_old.astype(BF16)), cross)
            so_ref[i, h] = g * s_old + _dot_tn(jnp.where(row == i, kh, 0.0), vh)
        qk = jnp.sum(qb.astype(F32) * kh.astype(BF16).astype(F32), axis=-1, keepdims=True)
        o = qk * vb.astype(F32) + g * cross
        sl = slice(h * dv, (h + 1) * dv)
        y_ref[:, sl] = (_head_norm(o) * gn_ref[:, sl] * _silu(g_ref[:, sl])).astype(y_ref.dtype)


def _sb_call(z, row0, bs, y_all, heads, dk, dv, state, l, prev, cos2, sin2, ret_norm, *, name):
    hk, hv = heads * dk, heads * dv
    nb = SUBLANES
    assert bs % nb == 0 and row0 % nb == 0
    r0 = row0 // nb
    q0 = hv // hk
    passes, blk, lay = _pass_maps(l, prev, bs // nb, state.shape[0])
    specs, extra, drop, aliases = _alias_args(8, [(y_all, 0), (prev, 1)])
    return pl.pallas_call(
        _skip_refs(functools.partial(_sb_kernel, heads=heads), drop),
        grid=(passes, bs // nb),
        in_specs=[pl.BlockSpec((nb, hk), lambda g, i: (r0 + blk(g, i), q0)),
                  pl.BlockSpec((nb, hk), lambda g, i: (r0 + blk(g, i), q0 + 1)),
                  pl.BlockSpec((nb, hv), lambda g, i: (r0 + blk(g, i), 2)),
                  pl.BlockSpec((nb, hv), lambda g, i: (r0 + blk(g, i), 3)),
                  pl.BlockSpec((None, nb, heads, dk, dv), lambda g, i: (l, blk(g, i), 0, 0, 0)),
                  pl.BlockSpec((1, dk), lambda g, i: (0, 0)),
                  pl.BlockSpec((1, dk), lambda g, i: (0, 0)),
                  pl.BlockSpec((1, hv), lambda g, i: (0, 0))] + specs,
        out_specs=[pl.BlockSpec((nb, hv), lambda g, i: (r0 + blk(g, i), 0)),
                   pl.BlockSpec((None, nb, heads, dk, dv), lambda g, i: (lay(g, i), i, 0, 0, 0))],
        out_shape=[jax.ShapeDtypeStruct(y_all.shape, BF16), jax.ShapeDtypeStruct(state.shape, F32)],
        input_output_aliases=aliases,
        compiler_params=_cparams(("arbitrary", "arbitrary")), name=name)(
            z, z, z, z, state, cos2, sin2, ret_norm, *extra)


def _mlstm_qkv_gates(x, xc, wq_ref, wk_ref, wv_ref, wif_ref, bif_ref, dh):
    q = _blockdiag(xc, wq_ref)
    k = _blockdiag(xc, wk_ref, scale=dh ** -0.5)
    v = _blockdiag(x, wv_ref)
    c = x.shape[1]
    gates = (_dot(q.astype(BF16), wif_ref[0:c, :]) + _dot(k.astype(BF16), wif_ref[c:2 * c, :])
             + _dot(v.astype(BF16), wif_ref[2 * c:3 * c, :]) + bif_ref[...])
    return q, k, v, gates


def _pc_kernel(x_ref, o_ref, cw_ref, cb_ref, wq_ref, wk_ref, wv_ref, wif_ref, bif_ref, gn_ref, sk_ref,
               y_ref, co_ref, no_ref, mo_ref, cv_ref, prev_ref, c_ref, n_ref, m_ref, *, pad, heads):
    c = pl.program_id(0)
    last = _last_chunk()
    L = x_ref.shape[0]
    dh = x_ref.shape[1] // heads
    row = lax.broadcasted_iota(jnp.int32, (L, 1), 0)
    valid = jnp.logical_or(c > 0, row >= pad)

    @pl.when(c == 0)
    def _():
        prev_ref[...] = jnp.zeros_like(prev_ref)
        c_ref[...] = jnp.zeros_like(c_ref)
        n_ref[...] = jnp.zeros_like(n_ref)
        m_ref[...] = jnp.zeros_like(m_ref)

    x = jnp.where(valid, x_ref[...], 0.0)
    xc = _silu(_chunk_conv(x, prev_ref[...], cw_ref, cb_ref, row))
    prev_ref[...] = x[L - SUBLANES:]
    q, k, v, gates = _mlstm_qkv_gates(x, xc, wq_ref, wk_ref, wv_ref, wif_ref, bif_ref, dh)
    lane = lax.broadcasted_iota(jnp.int32, gates.shape, 1)
    is_i = lane < heads
    ig_c = jnp.where(jnp.logical_and(valid, is_i), gates, NEG)
    lf_c = jnp.where(valid, _log_sigmoid(gates), 0.0)
    ig_r = ig_c.T
    lf_r = lf_c.T
    ti = lax.broadcasted_iota(jnp.int32, (L, L), 0)
    si = lax.broadcasted_iota(jnp.int32, (L, L), 1)
    causal = ti >= si
    ones_lower = jnp.where(causal, 1.0, 0.0).astype(BF16)
    ones_upper = jnp.where(si >= ti, 1.0, 0.0).astype(BF16)
    b_c = sum(_dot(ones_lower, part) for part in _split3(lf_c))
    b_r = sum(_dot(part, ones_upper) for part in _split3(lf_r))
    m_all = m_ref[...]
    m_lane = lax.broadcasted_iota(jnp.int32, m_all.shape, 1)
    m_new = jnp.zeros_like(m_all)
    for h in range(heads):
        sl = slice(h * dh, (h + 1) * dh)
        bc = b_c[:, heads + h:heads + h + 1]
        br = b_r[heads + h:heads + h + 1, :]
        igr = ig_r[h:h + 1, :]
        igc = ig_c[:, h:h + 1]
        m_prev = m_all[0:1, h:h + 1]
        log_d = jnp.where(causal, bc - br + igr, NEG)
        log_inter = bc + m_prev
        m_t = jnp.maximum(log_inter, jnp.max(log_d, axis=-1, keepdims=True))
        d_m = jnp.exp(log_d - m_t)
        w_inter = jnp.exp(log_inter - m_t)
        qh, kh, vh = q[:, sl], k[:, sl], v[:, sl]
        qb, kb, vb = qh.astype(BF16), kh.astype(BF16), vh.astype(BF16)
        scores = _dot_nt(qb, kb) * d_m
        c_old = c_ref[h]
        n_old = n_ref[h:h + 1, :]
        num = _dot(scores.astype(BF16), vb) + w_inter * _dot(qb, c_old.astype(BF16))
        den = jnp.sum(scores, axis=-1, keepdims=True) + w_inter * jnp.sum(qh * n_old, axis=-1, keepdims=True)
        hh = num / jnp.maximum(jnp.abs(den), jnp.exp(-m_t))
        m_end = m_t[L - 1:L]
        b_last = bc[L - 1:L]
        w_c = jnp.exp(b_last - bc + igc - m_end)
        decay_c = jnp.exp(b_last + m_prev - m_end)
        kw = kh * w_c
        c_ref[h] = decay_c * c_old + _dot_tn(kw.astype(BF16), vb)
        n_ref[h:h + 1, :] = decay_c * n_old + jnp.sum(kw, axis=0, keepdims=True)
        m_new = jnp.where(m_lane == h, m_end, m_new)
        hm = _sigmoid(o_ref[:, sl]) * hh
        y_ref[:, sl] = (_head_norm(hm) * gn_ref[:, sl] + sk_ref[:, sl] * xc[:, sl]).astype(y_ref.dtype)
    m_ref[...] = m_new

    @pl.when(c == last)
    def _():
        co_ref[0] = c_ref[...]
        no_ref[0] = n_ref[...]
        mo_ref[0] = m_ref[...]
        cv_ref[0] = x[L - (CONV_W - 1):]


def _pc_call(z, nseq, nch, pad, heads, xcol, p, *, name):
    rows = nseq * nch * CHUNK
    c = p['m_conv_w'].shape[1]
    dh = c // heads
    full = lambda a: pl.BlockSpec(a.shape, lambda k: (0,) * a.ndim)
    whole = lambda *shape: pl.BlockSpec(shape, lambda k: (0,) * len(shape))
    params = [p['m_conv_w'], p['m_conv_b'], p['m_wq'], p['m_wk'], p['m_wv'], p['m_w_if'], p['m_b_if'],
              p['m_norm'], p['m_skip']]
    blk = nseq * CHUNK
    return pl.pallas_call(
        _per_sequence(functools.partial(_pc_kernel, pad=pad, heads=heads), nseq, 2, len(params), 1, 4),
        grid=(nch + 1,),
        in_specs=[pl.BlockSpec((blk, c), lambda k: (jnp.minimum(k, nch - 1), xcol)),
                  pl.BlockSpec((blk, c), lambda k: (jnp.minimum(k, nch - 1), xcol + 1))] + [full(a) for a in params],
        out_specs=[pl.BlockSpec((blk, c), lambda k: (k, 0)),
                   whole(nseq, heads, dh, dh), whole(nseq, SUBLANES, dh), whole(nseq, SUBLANES, LANES),
                   whole(nseq, CONV_W - 1, c)],
        out_shape=[jax.ShapeDtypeStruct((rows + blk, c), BF16),
                   jax.ShapeDtypeStruct((nseq, heads, dh, dh), F32),
                   jax.ShapeDtypeStruct((nseq, SUBLANES, dh), F32),
                   jax.ShapeDtypeStruct((nseq, SUBLANES, LANES), F32),
                   jax.ShapeDtypeStruct((nseq, CONV_W - 1, c), F32)],
        scratch_shapes=[pltpu.VMEM((nseq, SUBLANES, c), F32), pltpu.VMEM((nseq, heads, dh, dh), F32),
                        pltpu.VMEM((nseq, SUBLANES, dh), F32), pltpu.VMEM((nseq, SUBLANES, LANES), F32)],
        compiler_params=_cparams(("arbitrary",)), name=name)(z, z, *params)


def _sc_kernel(*refs, heads):
    _first_pass_only(functools.partial(_sc_body, *refs, heads=heads), refs[-4])


def _sc_body(x_ref, o_ref, cs_ref, c_ref, n_ref, m_ref, cw_ref, cb_ref, wq_ref, wk_ref, wv_ref, wif_ref, bif_ref,
             gn_ref, sk_ref, y_ref, co_ref, no_ref, mo_ref, cv_ref, *, heads):
    nb = x_ref.shape[0]
    dh = x_ref.shape[1] // heads
    row = lax.broadcasted_iota(jnp.int32, (nb, 1), 0)
    x = x_ref[...]
    xc = _silu(_step_conv(x, cs_ref, cw_ref, cb_ref))
    q, k, v, gates = _mlstm_qkv_gates(x, xc, wq_ref, wk_ref, wv_ref, wif_ref, bif_ref, dh)
    lf = _log_sigmoid(gates)
    m_old = m_ref[...]
    lane = lax.broadcasted_iota(jnp.int32, m_old.shape, 1)
    m_new = jnp.zeros_like(m_old)
    for h in range(heads):
        sl = slice(h * dh, (h + 1) * dh)
        ig = gates[:, h:h + 1]
        log_inter = lf[:, heads + h:heads + h + 1] + m_old[:, h:h + 1]
        m_t = jnp.maximum(log_inter, ig)
        d_m = jnp.exp(ig - m_t)
        w_inter = jnp.exp(log_inter - m_t)
        qh, kh, vh = q[:, sl], k[:, sl], v[:, sl]
        qb, kb, vb = qh.astype(BF16), kh.astype(BF16), vh.astype(BF16)
        sc = jnp.sum(qb.astype(F32) * kb.astype(F32), axis=-1, keepdims=True) * d_m
        n_old = n_ref[:, sl]
        kw = kh * d_m
        qc = jnp.zeros((nb, dh), F32)
        for i in range(nb):
            c_old = c_ref[i, h]
            qc = jnp.where(row == i, _dot(qb, c_old.astype(BF16)), qc)
            co_ref[i, h] = (w_inter[i:i + 1] * c_old
                            + _dot_tn(jnp.where(row == i, kw, 0.0), vh))
        num = sc * vb.astype(F32) + w_inter * qc
        den = sc + w_inter * jnp.sum(qh * n_old, axis=-1, keepdims=True)
        hh = num / jnp.maximum(jnp.abs(den), jnp.exp(-m_t))
        no_ref[:, sl] = w_inter * n_old + kw
        m_new = jnp.where(lane == h, m_t, m_new)
        hm = _sigmoid(o_ref[:, sl]) * hh
        y_ref[:, sl] = (_head_norm(hm) * gn_ref[:, sl] + sk_ref[:, sl] * xc[:, sl]).astype(y_ref.dtype)
    mo_ref[...] = m_new
    _step_conv_state(x, cs_ref, cv_ref)


def _sc_call(z, row0, bs, y_all, heads, xcol, cs, c_state, l, prev, n_state, m_state, p, *, name):
    c = p['m_conv_w'].shape[1]
    dh = c // heads
    nb = SUBLANES
    assert bs % nb == 0 and row0 % nb == 0
    r0 = row0 // nb
    full = lambda a: pl.BlockSpec(a.shape, lambda g, i: (0,) * a.ndim)
    params = [p['m_conv_w'], p['m_conv_b'], p['m_wq'], p['m_wk'], p['m_wv'], p['m_w_if'], p['m_b_if'],
              p['m_norm'], p['m_skip']]
    passes, blk, lay = _pass_maps(l, prev, bs // nb, c_state.shape[0])
    cs_spec = pl.BlockSpec((CONV_W - 1, nb, c), lambda g, i: (0, blk(g, i), 0))
    row_spec = lambda w: pl.BlockSpec((nb, w), lambda g, i: (blk(g, i), 0))
    specs, extra, drop, aliases = _alias_args(6 + len(params), [(y_all, 0), (prev, 1)])
    return pl.pallas_call(
        _skip_refs(functools.partial(_sc_kernel, heads=heads), drop),
        grid=(passes, bs // nb),
        in_specs=[pl.BlockSpec((nb, c), lambda g, i: (r0 + blk(g, i), xcol)),
                  pl.BlockSpec((nb, c), lambda g, i: (r0 + blk(g, i), xcol + 1)),
                  cs_spec,
                  pl.BlockSpec((None, nb, heads, dh, dh), lambda g, i: (l, blk(g, i), 0, 0, 0)),
                  row_spec(c), row_spec(LANES)] + [full(a) for a in params] + specs,
        out_specs=[pl.BlockSpec((nb, c), lambda g, i: (r0 + blk(g, i), 0)),
                   pl.BlockSpec((None, nb, heads, dh, dh), lambda g, i: (lay(g, i), i, 0, 0, 0)),
                   row_spec(c), row_spec(LANES), cs_spec],
        out_shape=[jax.ShapeDtypeStruct(y_all.shape, BF16), jax.ShapeDtypeStruct(c_state.shape, F32),
                   jax.ShapeDtypeStruct((bs, c), F32), jax.ShapeDtypeStruct((bs, LANES), F32),
                   jax.ShapeDtypeStruct(cs.shape, F32)],
        input_output_aliases=aliases,
        compiler_params=_cparams(("arbitrary", "arbitrary")), name=name)(
            z, z, cs, c_state, n_state, m_state, *params, *extra)


def _ffn_kernel(sbe_ref, sbs_ref, sbn_ref, nv_ref, xs_hbm, wg_ref, wu_ref, wd_ref, ys_hbm,
                acc_ref, xp_ref, yp_ref, h_ref, wgb_ref, wub_ref, wdb_ref, zero_ref, isem, osem, zsem, *, nka, nkb):
    del sbe_ref
    j, s = pl.program_id(0), pl.program_id(1)
    nblk = sbn_ref[j]
    max_blk = acc_ref.shape[0] // MOE_ROWS
    f = wgb_ref.shape[1]
    dn = wdb_ref.shape[0]

    @pl.when(jnp.logical_and(j == 0, s == 0))
    def _():
        zero_ref[...] = jnp.zeros_like(zero_ref)
        xp_ref[...] = jnp.zeros_like(xp_ref)

    n_sb = pl.num_programs(0)

    def hbm_rows(ref, jj, p):
        return ref.at[pl.ds(pl.multiple_of((sbs_ref[jj] + p) * MOE_ROWS, MOE_ROWS), MOE_ROWS), :]

    def vmem_rows(ref, p):
        return ref.at[pl.ds(p * MOE_ROWS, MOE_ROWS), :]

    def x_copy(jj):
        return lambda p: pltpu.make_async_copy(hbm_rows(xs_hbm, jj, p), vmem_rows(xp_ref, p), isem)

    def y_copy(jj):
        return lambda p: pltpu.make_async_copy(vmem_rows(yp_ref, p), hbm_rows(ys_hbm, jj, p), osem)

    def for_blocks(fn, n):
        for p in range(max_blk):
            @pl.when(p < n)
            def _():
                fn(p)

    def block_loop(body):
        per = FFN_ROWS // MOE_ROWS

        def step(b, carry):
            body(pl.ds(pl.multiple_of(b * FFN_ROWS, FFN_ROWS), FFN_ROWS))
            return carry
        lax.fori_loop(0, (nblk + per - 1) // per, step, 0)

    @pl.when(nblk > 0)
    def _():
        @pl.when(jnp.logical_and(s == 0, j == 0))
        def _():
            for_blocks(lambda p: x_copy(j)(p).start(), nblk)

        @pl.when(s == 0)
        def _():
            for_blocks(lambda p: x_copy(j)(p).wait(), nblk)

        @pl.when(s == nka)
        def _():
            jn = jnp.minimum(j + 1, n_sb - 1)
            n_next = jnp.where(j + 1 < n_sb, sbn_ref[jn], 0)
            for_blocks(lambda p: x_copy(jn)(p).start(), n_next)

        for c in range(nka):
            @pl.when(s == c)
            def _(c=c):
                wgb_ref[...] = wg_ref[...].astype(BF16)
                wub_ref[...] = wu_ref[...].astype(BF16)

                def body(rows):
                    x = _unpack_half(xp_ref[rows, :], c).astype(BF16)
                    g = _dot(x, wgb_ref[...])
                    u = _dot(x, wub_ref[...])
                    if c > 0:
                        g = g + acc_ref[rows, 0:f]
                        u = u + acc_ref[rows, f:2 * f]
                    if c < nka - 1:
                        acc_ref[rows, 0:f] = g
                        acc_ref[rows, f:2 * f] = u
                    else:
                        h_ref[rows, :] = (_silu(g) * u).astype(BF16)
                block_loop(body)

        for c in range(nkb):
            @pl.when(s == nka + c)
            def _(c=c):
                wdb_ref[...] = wd_ref[...].astype(BF16)

                def body(rows):
                    y = _dot(h_ref[rows, c * dn:(c + 1) * dn], wdb_ref[...])
                    if c > 0:
                        y = y + acc_ref[rows, :]
                    acc_ref[rows, :] = y
                block_loop(body)

                if c == nkb - 1:
                    @pl.when(j > 0)
                    def _():
                        jp = jnp.maximum(j - 1, 0)
                        for_blocks(lambda p: y_copy(jp)(p).wait(), sbn_ref[jp])

                    def pack_block(p):
                        sl = slice(p * MOE_ROWS, (p + 1) * MOE_ROWS)
                        yp_ref[sl, :] = _pack_halves(acc_ref[sl, :])
                    for_blocks(pack_block, nblk)
                    for_blocks(lambda p: y_copy(j)(p).start(), nblk)

    @pl.when(jnp.logical_and(j == n_sb - 1, s == pl.num_programs(1) - 1))
    def _():
        jl = jnp.maximum(nv_ref[1] - 1, 0)
        for_blocks(lambda p: y_copy(jl)(p).wait(), jnp.where(nv_ref[1] > 0, sbn_ref[jl], 0))
        _zero_blocks(zero_ref, ys_hbm, nv_ref[0], ys_hbm.shape[0] // MOE_ROWS, zsem)


def _ffn_call(xs, tables, wg, wu, wd, l, *, name):
    sb_e, sb_blk0, sb_nblk, n_valid = tables
    ns, dp = xs.shape
    d, f = wg.shape[2], wg.shape[3]
    nsb = sb_e.shape[0]
    nka, nkb = d // FFN_GU_CHUNK, f // FFN_DN_CHUNK
    assert xs.dtype == U32 and dp * 2 == d and nka == 2 and FFN_GU_CHUNK == dp
    assert f % FFN_DN_CHUNK == 0 and 2 * f <= d and MOE_SB_ROWS % FFN_ROWS == 0 and FFN_ROWS % MOE_ROWS == 0

    def gu_map(j, s, sbe, sbs, sbn, nv):
        return (l, sbe[j], jnp.where(j < nv[1], jnp.minimum(s, nka - 1), nka - 1), 0)

    def dn_map(j, s, sbe, sbs, sbn, nv):
        prev = jnp.maximum(j - 1, 0)
        e = jnp.where(s >= nka, sbe[j], sbe[prev])
        c = jnp.where(j < nv[1], jnp.where(s >= nka, s - nka, jnp.where(j == 0, 0, nkb - 1)), nkb - 1)
        return (l, e, c, 0)

    grid_spec = pltpu.PrefetchScalarGridSpec(
        num_scalar_prefetch=4, grid=(nsb, nka + nkb),
        in_specs=[pl.BlockSpec(memory_space=pl.ANY),
                  pl.BlockSpec((None, None, FFN_GU_CHUNK, f), gu_map),
                  pl.BlockSpec((None, None, FFN_GU_CHUNK, f), gu_map),
                  pl.BlockSpec((None, None, FFN_DN_CHUNK, d), dn_map)],
        out_specs=pl.BlockSpec(memory_space=pl.ANY),
        scratch_shapes=[pltpu.VMEM((MOE_SB_ROWS, d), F32), pltpu.VMEM((MOE_SB_ROWS, dp), U32),
                        pltpu.VMEM((MOE_SB_ROWS, dp), U32), pltpu.VMEM((MOE_SB_ROWS, f), BF16),
                        pltpu.VMEM((FFN_GU_CHUNK, f), BF16), pltpu.VMEM((FFN_GU_CHUNK, f), BF16),
                        pltpu.VMEM((FFN_DN_CHUNK, d), BF16), pltpu.VMEM((MOE_ROWS, dp), U32),
                        pltpu.SemaphoreType.DMA(()), pltpu.SemaphoreType.DMA(()), pltpu.SemaphoreType.DMA(())])
    return pl.pallas_call(
        functools.partial(_ffn_kernel, nka=nka, nkb=nkb), grid_spec=grid_spec,
        out_shape=jax.ShapeDtypeStruct((ns, dp), U32),
        compiler_params=_cparams(("arbitrary", "arbitrary")), name=name)(
            sb_e, sb_blk0, sb_nblk, n_valid, xs, wg, wu, wd)


def _take(table, idx):
    return jnp.sum(jnp.where(idx[:, None] == jnp.arange(table.shape[0])[None, :], table[None, :], 0), axis=1)


def _route_tables(ids, tile_cnt, r, tm):
    eidx = jnp.arange(N_EXPERTS, dtype=jnp.int32)
    cnt_t = tile_cnt[:, 0, :N_EXPERTS].astype(jnp.int32)
    counts = jnp.sum(cnt_t, axis=0)
    tile_off = jnp.cumsum(cnt_t, axis=0) - cnt_t
    padded = (counts + MOE_ROWS - 1) // MOE_ROWS * MOE_ROWS
    pad_end = jnp.cumsum(padded)
    pad_start = pad_end - padded
    base = jnp.repeat(pad_start[None, :] + tile_off, tm, axis=0)
    e = ids[:, :TOP_K]
    rank = ids[:, TOP_K:2 * TOP_K]
    dest = rank + jnp.sum(jnp.where(e[:, :, None] == eidx, base[:, None, :], 0), axis=-1)
    n_slots = (r * TOP_K // MOE_ROWS + N_EXPERTS) * MOE_ROWS
    nsb_e = (padded + MOE_SB_ROWS - 1) // MOE_SB_ROWS
    sb_end = jnp.cumsum(nsb_e)
    n_sb = sb_end[-1]
    nsb = n_slots // MOE_SB_ROWS + N_EXPERTS
    j = jnp.arange(nsb, dtype=jnp.int32)
    valid = j < n_sb
    e_j = jnp.minimum(jnp.sum(sb_end[None, :] <= j[:, None], axis=1), N_EXPERTS - 1)
    t = j - _take(sb_end - nsb_e, e_j)
    sb_blk0 = jnp.where(valid, (_take(pad_start, e_j) + t * MOE_SB_ROWS) // MOE_ROWS, 0)
    sb_nblk = jnp.where(valid, jnp.clip((_take(padded, e_j) - t * MOE_SB_ROWS) // MOE_ROWS,
                                        0, MOE_SB_ROWS // MOE_ROWS), 0)
    e_last = jnp.sum(jnp.where(j == n_sb - 1, e_j, 0))
    sb_e = jnp.where(valid, e_j, e_last)
    n_valid = jnp.stack([pad_end[-1] // MOE_ROWS, n_sb])
    i32 = lambda a: a.astype(jnp.int32)
    return i32(dest), i32(pad_end), (i32(sb_e), i32(sb_blk0), i32(sb_nblk), i32(n_valid)), n_slots


def _expand_blockdiag(w):
    nb, bs, _ = w.shape
    if bs == LANES:
        return w.astype(BF16)
    per = LANES // bs
    eye = jnp.eye(per, dtype=w.dtype)
    wt = w.reshape(nb // per, per, bs, bs)
    out = jnp.einsum('tpbc,pq->tpbqc', wt, eye)
    return out.reshape(nb // per, LANES, LANES).astype(BF16)


def _rope_tables(pos, half):
    freq = ROPE_BASE ** (-jnp.arange(half, dtype=F32) / half)
    ang = pos[:, None] * freq[None, :]
    cos, sin = jnp.cos(ang), jnp.sin(ang)
    return jnp.concatenate([cos, cos], axis=-1), jnp.concatenate([-sin, sin], axis=-1)


def kernel(x_prompt, x_sample, state_rglru_h, state_rglru_conv, state_ret, state_mlstm_C, state_mlstm_n,
           state_mlstm_m, state_mlstm_conv, meta_tokens, norm_mix, norm_ffn, norm_final, w_in, rg_conv_w,
           rg_conv_b, rg_wa, rg_ba, rg_wx, rg_bx, rg_lambda, ret_norm, m_conv_w, m_conv_b, m_wq, m_wk, m_wv,
           m_w_if, m_b_if, m_norm, m_skip, w_branch, w_out, moe_w_group, moe_b_group, moe_w_expert,
           moe_b_expert, moe_w_gate, moe_w_up, moe_w_down):
    bp, seq, d = x_prompt.shape
    bs = x_sample.shape[0]
    n_meta = meta_tokens.shape[0]
    depth = w_in.shape[0]
    d_rnn = state_rglru_h.shape[2]
    _, _, r_heads, r_dk, r_dv = state_ret.shape
    m_heads, m_dh = state_mlstm_C.shape[2], state_mlstm_C.shape[3]
    d_m = m_heads * m_dh
    assert seq % CHUNK == 0 and n_meta <= CHUNK and x_sample.shape[1] == 1
    pad = CHUNK - n_meta
    nch = 1 + seq // CHUNK
    rp = bp * nch * CHUNK
    r = rp + bs
    assert d_rnn == r_heads * r_dv == d_m
    xm_col = (d_rnn + 2 * r_heads * r_dk + 2 * r_heads * r_dv) // d_m
    gate_col0 = d_rnn + 2 * r_heads * r_dk + 2 * r_heads * r_dv + 2 * d_m

    pos_p = jnp.arange(nch * CHUNK, dtype=F32) - pad
    cos_p, sin_p = _rope_tables(pos_p, r_dk // 2)
    pos_s = jnp.full((1,), float(PAST_LEN), F32)
    cos_s, sin_s = _rope_tables(pos_s, r_dk // 2)

    tm = _pick_tile(r, 384, BF16_SUBLANES)
    assert rp % CHUNK == 0 and bs % BF16_SUBLANES == 0 and bs <= bp * CHUNK
    prompt_states, sample_states = [], []
    combine = None
    ret_s = c_s = None
    for l in range(depth):
        p = {'rg_conv_w': rg_conv_w[l], 'rg_conv_b': rg_conv_b[l].reshape(1, -1),
             'rg_wa': _expand_blockdiag(rg_wa[l]), 'rg_ba': rg_ba[l].reshape(1, -1),
             'rg_wx': _expand_blockdiag(rg_wx[l]), 'rg_bx': rg_bx[l].reshape(1, -1),
             'rg_lambda': rg_lambda[l].reshape(1, -1),
             'm_conv_w': m_conv_w[l], 'm_conv_b': m_conv_b[l].reshape(1, -1),
             'm_wq': _expand_blockdiag(m_wq[l]), 'm_wk': _expand_blockdiag(m_wk[l]),
             'm_wv': _expand_blockdiag(m_wv[l]),
             'm_w_if': jnp.pad(m_w_if[l], ((0, 0), (0, LANES - 2 * m_heads))).astype(BF16),
             'm_b_if': jnp.pad(m_b_if[l], (0, LANES - 2 * m_heads)).reshape(1, LANES),
             'm_norm': m_norm[l].reshape(1, -1), 'm_skip': m_skip[l].reshape(1, -1)}
        gn_ret = ret_norm[l].reshape(1, -1)

        if l == 0:
            x, h = _embed_norm_call(x_prompt, meta_tokens, x_sample.reshape(bs, d), norm_mix[l], nch,
                                    name=f'norm_mix{l}')
        else:
            x, h = _norm_call(x, norm_mix[l], tm=tm, grid=(r // tm,), row_tile=lambda i: i, want_x=True,
                              h_shape=(r, d), h_dtype=BF16, h_spec=pl.BlockSpec((tm, d), lambda i: (i, 0)),
                              combine=combine, name=f'norm_mix{l}')
        z = _mm_call(h, w_in, l, name=f'in_proj{l}')

        ya_p, rgh_p, rgc_p = _pa_call(z, bp, nch, pad, p, name=f'rglru_p{l}')
        yb_p, ret_p = _pb_call(z, bp, nch, pad, r_heads, r_dk, r_dv, cos_p, sin_p, gn_ret, name=f'ret_p{l}')
        yc_p, c_p, n_p, m_p, mc_p = _pc_call(z, bp, nch, pad, m_heads, xm_col, p, name=f'mlstm_p{l}')

        rg_cs = jnp.transpose(state_rglru_conv[l], (1, 0, 2))
        ya, rgh_s, rgc_s = _sa_call(z, rp, bs, ya_p, rg_cs, state_rglru_h[l], p, name=f'rglru_s{l}')
        yb, ret_s = _sb_call(z, rp, bs, yb_p, r_heads, r_dk, r_dv, state_ret, l, ret_s, cos_s, sin_s, gn_ret,
                             name=f'ret_s{l}')
        m_cs = jnp.transpose(state_mlstm_conv[l], (1, 0, 2))
        m_in = jnp.pad(state_mlstm_m[l], ((0, 0), (0, LANES - m_heads)))
        yc, c_s, n_s, m_s, mc_s = _sc_call(z, rp, bs, yc_p, m_heads, xm_col, m_cs, state_mlstm_C, l, c_s,
                                           state_mlstm_n[l].reshape(bs, d_m), m_in, p, name=f'mlstm_s{l}')

        prompt_states.append((rgh_p[:, 0], rgc_p, ret_p, c_p, n_p[:, :m_heads], m_p[:, 0, :m_heads], mc_p))
        sample_states.append((rgh_s, jnp.transpose(rgc_s, (1, 0, 2)), None, None,
                              n_s.reshape(bs, m_heads, m_dh), m_s[:, :m_heads], jnp.transpose(mc_s, (1, 0, 2))))

        merged = _merge_call((ya, yb, yc), z, w_branch, l, gate_col0, name=f'merge{l}')
        x = _mm_call(merged, w_out, l, res=x, name=f'out_proj{l}')

        wr = jnp.pad(jnp.concatenate([moe_w_group[l], moe_w_expert[l]], axis=1),
                     ((0, 0), (0, LANES - N_GROUPS - N_EXPERTS)))
        br = jnp.pad(jnp.concatenate([moe_b_group[l], moe_b_expert[l]]), (0, LANES - N_GROUPS - N_EXPERTS))
        h2, ids, wts, tile_cnt = _norm_call(
            x, norm_ffn[l], tm=tm, grid=(r // tm,), row_tile=lambda i: i, want_x=False, h_shape=(r, d // 2),
            h_dtype=U32, h_spec=pl.BlockSpec((tm, d // 2), lambda i: (i, 0)), router=(wr, br.reshape(1, LANES)),
            name=f'norm_router{l}')
        dest, pad_end, sb_tables, n_slots = _route_tables(ids, tile_cnt, r, tm)
        xs = _dispatch_call(h2, dest, pad_end, n_slots, tm=tm, name=f'moe_dispatch{l}')
        ys = _ffn_call(xs, sb_tables, moe_w_gate, moe_w_up, moe_w_down, l, name=f'moe_ffn{l}')
        combine = (dest, wts, ys)

    (y_prompt,) = _norm_call(
        x, norm_final, tm=CHUNK, grid=(bp, seq // CHUNK), row_tile=lambda b, c: (1 + c) * bp + b, want_x=False,
        h_shape=(bp, seq, d), h_dtype=F32, h_spec=pl.BlockSpec((None, CHUNK, d), lambda b, c: (b, c, 0)),
        combine=combine, name='norm_final_p')
    ts = _pick_tile(bs, CHUNK, SUBLANES)
    assert rp % ts == 0
    (y_sample,) = _norm_call(
        x, norm_final, tm=ts, grid=(bs // ts,), row_tile=lambda i: rp // ts + i, want_x=False,
        h_shape=(bs, d), h_dtype=F32, h_spec=pl.BlockSpec((ts, d), lambda i: (i, 0)),
        combine=combine, name='norm_final_s')
    pn = [jnp.stack([s[i] for s in prompt_states], axis=0) for i in range(7)]
    sn = [None if sample_states[0][i] is None else jnp.stack([s[i] for s in sample_states], axis=0)
          for i in range(7)]
    sn[2], sn[3] = ret_s, c_s
    return (y_prompt, y_sample.reshape(bs, 1, d), *pn, *sn)
```

```python
import functools
import math

import jax
import jax.numpy as jnp
import numpy as np
from jax import lax
from jax.experimental import pallas as pl
from jax.experimental.pallas import tpu as pltpu

F32 = jnp.float32
BF16 = jnp.bfloat16

LANES = 128
SUBLANES = 8
BF16_SUBLANES = 16
VMEM_LIMIT_BYTES = 56 * 1024 * 1024

CHUNK = 128
CONV_W = 4
EPS = 1e-6
RG_C = 8.0
ROPE_BASE = 10000.0
PAST_LEN = 16384
N_GROUPS = 4
EXP_PER_GROUP = 8
N_EXPERTS = N_GROUPS * EXP_PER_GROUP
TOP_K = 2
MOE_ROWS = 128
MOE_SB_ROWS = 1024
FFN_ROWS = 256
FFN_GU_CHUNK = 1024
FFN_DN_CHUNK = 512
NEG = -1e30


def _cparams(sem, vmem=VMEM_LIMIT_BYTES):
    return pltpu.CompilerParams(dimension_semantics=sem, vmem_limit_bytes=vmem)


def _pick_tile(n, cap, mult):
    best = None
    for t in range(mult, min(n, cap) + 1, mult):
        if n % t == 0:
            best = t
    assert best is not None, (n, cap, mult)
    return best


def _skip_refs(fn, idxs):
    def wrapped(*refs):
        return fn(*[r for i, r in enumerate(refs) if i not in idxs])
    return wrapped


def _alias_args(n_in, aliased):
    arrs = [(a, o) for a, o in aliased if a is not None]
    idxs = list(range(n_in, n_in + len(arrs)))
    return ([pl.BlockSpec(memory_space=pl.ANY)] * len(arrs), [a for a, _ in arrs], idxs,
            {i: o for i, (_, o) in zip(idxs, arrs)})


def _dot(a, b):
    return jnp.dot(a, b, preferred_element_type=F32)


def _dot_nt(a, b):
    return lax.dot_general(a, b, (((1,), (1,)), ((), ())), preferred_element_type=F32)


def _dot_tn(a, b):
    return lax.dot_general(a, b, (((0,), (0,)), ((), ())), preferred_element_type=F32)


def _split3(x):
    hi = x.astype(BF16)
    r1 = x - hi.astype(F32)
    mid = r1.astype(BF16)
    lo = (r1 - mid.astype(F32)).astype(BF16)
    return hi, mid, lo


U32 = jnp.uint32


def _pack_halves(x):
    n = x.shape[1] // 2
    lo = lax.bitcast_convert_type(x[:, :n].astype(BF16).astype(F32), U32)
    hi = lax.bitcast_convert_type(x[:, n:].astype(BF16).astype(F32), U32)
    return (lo >> 16) | (hi & U32(0xFFFF0000))


def _unpack_half(p, half):
    bits = (p << 16) if half == 0 else (p & U32(0xFFFF0000))
    return lax.bitcast_convert_type(bits, F32)


def _dot_hi(a, b):
    return jnp.dot(a, b, preferred_element_type=F32, precision=lax.Precision.HIGHEST)


def _sigmoid(x):
    return 1.0 / (1.0 + jnp.exp(-x))


def _silu(x):
    return x * _sigmoid(x)


def _log_sigmoid(x):
    return jnp.minimum(x, 0.0) - jnp.log(1.0 + jnp.exp(-jnp.abs(x)))


def _softplus(x):
    return jnp.maximum(x, 0.0) + jnp.log(1.0 + jnp.exp(-jnp.abs(x)))


def _head_norm(o):
    mu = jnp.mean(o, axis=-1, keepdims=True)
    d = o - mu
    var = jnp.mean(d * d, axis=-1, keepdims=True)
    return d * lax.rsqrt(var + EPS)


def _blockdiag(x, w_ref, scale=None):
    outs = []
    for j in range(w_ref.shape[0]):
        o = _dot(x[:, j * LANES:(j + 1) * LANES].astype(BF16), w_ref[j])
        outs.append(o if scale is None else o * scale)
    return jnp.concatenate(outs, axis=-1)


ROW_DMA_UNROLL = 8


def _row_copies(n_rows, row_copy):
    def start(r, carry):
        for k in range(TOP_K):
            row_copy(r, k).start(priority=k % 2)
        return carry

    def wait(r, carry):
        for k in range(TOP_K):
            row_copy(r, k).wait()
        return carry

    lax.fori_loop(0, n_rows, start, 0, unroll=ROW_DMA_UNROLL)
    lax.fori_loop(0, n_rows, wait, 0, unroll=ROW_DMA_UNROLL)


def _norm_kernel(*refs, combine, want_x, router):
    refs = list(refs)
    if combine:
        dest_ref, wts_in_ref, yb_hbm = refs[:3]
        refs = refs[3:]
    x_ref, g_ref = refs[:2]
    refs = refs[2:]
    if router:
        wr_ref, br_ref = refs[:2]
        refs = refs[2:]
    n_out = int(want_x) + 1 + (3 if router else 0)
    outs, scratch = refs[:n_out], refs[n_out:]
    x = x_ref[...]
    tm = x.shape[0]
    if combine:
        gbuf, gsem = scratch
        _row_copies(tm, lambda r, k: pltpu.make_async_copy(
            yb_hbm.at[pl.ds(dest_ref[0, TOP_K * r + k], 1), :], gbuf.at[k, pl.ds(r, 1), :], gsem))
        for k in range(TOP_K):
            y = jnp.concatenate([_unpack_half(gbuf[k], 0), _unpack_half(gbuf[k], 1)], axis=-1)
            x = x + wts_in_ref[:, k:k + 1] * y
    k = 0
    if want_x:
        outs[k][...] = x
        k += 1
    ms = jnp.mean(x * x, axis=-1, keepdims=True)
    h = x * lax.rsqrt(ms + EPS) * g_ref[...]
    outs[k][...] = _pack_halves(h) if router else h.astype(outs[k].dtype)
    k += 1
    if router:
        ids_ref, wts_ref, cnt_ref = outs[k], outs[k + 1], outs[k + 2]
        logits = _dot(h.astype(BF16), wr_ref[...]) + br_ref[...]
        lane = lax.broadcasted_iota(jnp.int32, logits.shape, 1)
        big = jnp.int32(1 << 20)
        is_g = lane < N_GROUPS
        gl = jnp.where(is_g, logits, NEG)
        gmax = jnp.max(gl, axis=-1, keepdims=True)
        gidx = jnp.min(jnp.where(gl == gmax, lane, big), axis=-1, keepdims=True)
        gsum = jnp.sum(jnp.where(is_g, jnp.exp(gl - gmax), 0.0), axis=-1, keepdims=True)
        gprob = 1.0 / gsum
        lo = N_GROUPS + EXP_PER_GROUP * gidx
        em = jnp.where(lane >= lo, jnp.where(lane < lo + EXP_PER_GROUP, logits, NEG), NEG)
        e1v = jnp.max(em, axis=-1, keepdims=True)
        e1i = jnp.min(jnp.where(em == e1v, lane, big), axis=-1, keepdims=True)
        em2 = jnp.where(lane == e1i, NEG, em)
        e2v = jnp.max(em2, axis=-1, keepdims=True)
        e2i = jnp.min(jnp.where(em2 == e2v, lane, big), axis=-1, keepdims=True)
        t = jnp.exp(e2v - e1v)
        w1 = gprob / (1.0 + t)
        w2 = gprob * t / (1.0 + t)
        e1, e2 = e1i - N_GROUPS, e2i - N_GROUPS
        chosen = jnp.where(lane == e1, 1.0, jnp.where(lane == e2, 1.0, 0.0))
        ti = lax.broadcasted_iota(jnp.int32, (tm, tm), 0)
        si = lax.broadcasted_iota(jnp.int32, (tm, tm), 1)
        before = _dot(jnp.where(ti > si, 1.0, 0.0).astype(BF16), chosen.astype(BF16))
        rank1 = jnp.sum(jnp.where(lane == e1, before, 0.0), axis=-1, keepdims=True).astype(jnp.int32)
        rank2 = jnp.sum(jnp.where(lane == e2, before, 0.0), axis=-1, keepdims=True).astype(jnp.int32)
        ids_ref[...] = jnp.where(lane == 0, e1, jnp.where(lane == 1, e2,
                                 jnp.where(lane == 2, rank1, jnp.where(lane == 3, rank2, 0))))
        wts_ref[...] = jnp.where(lane == 0, w1, jnp.where(lane == 1, w2, 0.0))
        cnt_ref[...] = jnp.sum(chosen, axis=0, keepdims=True)


def _norm_call(x, g, *, tm, grid, row_tile, want_x, h_shape, h_dtype, h_spec, combine=None, router=None, name):
    r, d = x.shape
    x_spec = pl.BlockSpec((tm, d), lambda *gi: (row_tile(*gi), 0))
    in_specs, args, scratch = [], [], []
    if combine is not None:
        dest, wts, yb = combine
        in_specs += [pl.BlockSpec((None, 1, TOP_K * tm), lambda *gi: (row_tile(*gi), 0, 0), memory_space=pltpu.SMEM),
                     pl.BlockSpec((tm, LANES), lambda *gi: (row_tile(*gi), 0)),
                     pl.BlockSpec(memory_space=pl.ANY)]
        args += [dest.reshape(r // tm, 1, TOP_K * tm), wts, yb]
        assert yb.dtype == U32 and yb.shape[1] * 2 == d
        scratch = [pltpu.VMEM((TOP_K, tm, d // 2), U32), pltpu.SemaphoreType.DMA(())]
    in_specs += [x_spec, pl.BlockSpec((1, d), lambda *gi: (0, 0))]
    args += [x, g.reshape(1, d)]
    if router is not None:
        in_specs += [pl.BlockSpec((d, LANES), lambda *gi: (0, 0)), pl.BlockSpec((1, LANES), lambda *gi: (0, 0))]
        args += list(router)
    out_shape, out_specs = [], []
    if want_x:
        out_shape.append(jax.ShapeDtypeStruct((r, d), F32))
        out_specs.append(x_spec)
    out_shape.append(jax.ShapeDtypeStruct(h_shape, h_dtype))
    out_specs.append(h_spec)
    if router is not None:
        out_shape += [jax.ShapeDtypeStruct((r, LANES), jnp.int32), jax.ShapeDtypeStruct((r, LANES), F32),
                      jax.ShapeDtypeStruct((r // tm, 1, LANES), F32)]
        out_specs += [pl.BlockSpec((tm, LANES), lambda *gi: (row_tile(*gi), 0))] * 2
        out_specs += [pl.BlockSpec((None, 1, LANES), lambda *gi: (row_tile(*gi), 0, 0))]
    return pl.pallas_call(
        functools.partial(_norm_kernel, combine=combine is not None, want_x=want_x, router=router is not None),
        grid=grid, in_specs=in_specs, out_specs=out_specs, out_shape=out_shape, scratch_shapes=scratch,
        compiler_params=_cparams(("arbitrary",) * len(grid)), name=name)(*args)


def _embed_norm_kernel(xp_ref, meta_ref, xs_ref, g_ref, x_ref, h_ref, *, n_seq, n_prompt_tiles):
    i = pl.program_id(0)
    c = i // n_seq
    tm, d = x_ref.shape

    def finish(x):
        x_ref[...] = x
        ms = jnp.mean(x * x, axis=-1, keepdims=True)
        h_ref[...] = (x * lax.rsqrt(ms + EPS) * g_ref[...]).astype(h_ref.dtype)

    @pl.when(i >= n_prompt_tiles)
    def _():
        finish(xs_ref[...])

    @pl.when(jnp.logical_and(i < n_prompt_tiles, c == 0))
    def _():
        finish(jnp.concatenate([jnp.zeros((tm - meta_ref.shape[0], d), F32), meta_ref[...]], axis=0))

    @pl.when(jnp.logical_and(i < n_prompt_tiles, c > 0))
    def _():
        finish(xp_ref[...])


def _embed_norm_call(x_prompt, meta, x_sample, g, nch, *, name):
    bp, seq, d = x_prompt.shape
    bs = x_sample.shape[0]
    npt = bp * nch
    assert bs % CHUNK == 0 and meta.shape[0] % SUBLANES == 0
    r = npt * CHUNK + bs
    row_spec = pl.BlockSpec((CHUNK, d), lambda i: (i, 0))
    return pl.pallas_call(
        functools.partial(_embed_norm_kernel, n_seq=bp, n_prompt_tiles=npt), grid=(r // CHUNK,),
        in_specs=[pl.BlockSpec((None, CHUNK, d),
                               lambda i: (lax.rem(i, bp), jnp.clip(i // bp - 1, 0, nch - 2), 0)),
                  pl.BlockSpec(meta.shape, lambda i: (0, 0)),
                  pl.BlockSpec((CHUNK, d), lambda i: (jnp.maximum(i - npt, 0), 0)),
                  pl.BlockSpec((1, d), lambda i: (0, 0))],
        out_specs=[row_spec, row_spec],
        out_shape=[jax.ShapeDtypeStruct((r, d), F32), jax.ShapeDtypeStruct((r, d), BF16)],
        compiler_params=_cparams(("arbitrary",)), name=name)(x_prompt, meta, x_sample, g.reshape(1, d))


def _zero_blocks(zero_ref, out_hbm, first, last, sem):
    def blk_copy(b):
        return pltpu.make_async_copy(zero_ref, out_hbm.at[pl.ds(pl.multiple_of(b * MOE_ROWS, MOE_ROWS), MOE_ROWS), :], sem)

    def start(b, carry):
        blk_copy(b).start()
        return carry

    def wait(b, carry):
        blk_copy(b).wait()
        return carry

    lax.fori_loop(first, last, start, 0)
    lax.fori_loop(first, last, wait, 0)


def _dispatch_kernel(pe_ref, dest_ref, h_ref, xs_out, zero_ref, sem, zsem):
    tm = h_ref.shape[0]
    n_blocks = xs_out.shape[0] // MOE_ROWS

    @pl.when(pl.program_id(0) == 0)
    def _():
        zero_ref[...] = jnp.zeros_like(zero_ref)

        def last_block(e):
            b = jnp.maximum(pe_ref[e] // MOE_ROWS - 1, 0)
            return pltpu.make_async_copy(
                zero_ref, xs_out.at[pl.ds(pl.multiple_of(b * MOE_ROWS, MOE_ROWS), MOE_ROWS), :], zsem)

        for e in range(N_EXPERTS):
            last_block(e).start()
        for e in range(N_EXPERTS):
            last_block(e).wait()
        _zero_blocks(zero_ref, xs_out, pe_ref[N_EXPERTS - 1] // MOE_ROWS, n_blocks, zsem)

    _row_copies(tm, lambda r, k: pltpu.make_async_copy(
        h_ref.at[pl.ds(r, 1), :], xs_out.at[pl.ds(dest_ref[0, TOP_K * r + k], 1), :], sem))


def _dispatch_call(h2, dest, pad_end, n_slots, *, tm, name):
    r, d = h2.shape
    grid_spec = pltpu.PrefetchScalarGridSpec(
        num_scalar_prefetch=1, grid=(r // tm,),
        in_specs=[pl.BlockSpec((None, 1, TOP_K * tm), lambda i, pe: (i, 0, 0), memory_space=pltpu.SMEM),
                  pl.BlockSpec((tm, d), lambda i, pe: (i, 0))],
        out_specs=pl.BlockSpec(memory_space=pl.ANY),
        scratch_shapes=[pltpu.VMEM((MOE_ROWS, d), h2.dtype), pltpu.SemaphoreType.DMA(()),
                        pltpu.SemaphoreType.DMA(())])
    return pl.pallas_call(
        _dispatch_kernel, grid_spec=grid_spec, out_shape=jax.ShapeDtypeStruct((n_slots, d), h2.dtype),
        compiler_params=_cparams(("arbitrary",)), name=name)(pad_end, dest.reshape(r // tm, 1, TOP_K * tm), h2)


def _mm_kernel(*refs, has_res):
    if has_res:
        x_ref, w_ref, r_ref, o_ref, wb_ref = refs
    else:
        x_ref, w_ref, o_ref, wb_ref = refs

    @pl.when(pl.program_id(1) == 0)
    def _():
        wb_ref[...] = w_ref[...].astype(BF16)

    acc = _dot(x_ref[...], wb_ref[...])
    if has_res:
        acc = acc + r_ref[...]
    o_ref[...] = acc.astype(o_ref.dtype)


def _mm_call(x, w, l, res=None, *, out_dtype=F32, name):
    r, k = x.shape
    n = w.shape[2]
    tm = _pick_tile(r, 1152, BF16_SUBLANES)
    tn = _pick_tile(n, 1024, LANES)
    in_specs = [pl.BlockSpec((tm, k), lambda j, i: (i, 0)), pl.BlockSpec((None, k, tn), lambda j, i: (l, 0, j))]
    args = [x, w]
    if res is not None:
        in_specs.append(pl.BlockSpec((tm, tn), lambda j, i: (i, j)))
        args.append(res)
    return pl.pallas_call(
        functools.partial(_mm_kernel, has_res=res is not None),
        grid=(n // tn, r // tm), in_specs=in_specs,
        out_specs=pl.BlockSpec((tm, tn), lambda j, i: (i, j)),
        out_shape=jax.ShapeDtypeStruct((r, n), out_dtype),
        scratch_shapes=[pltpu.VMEM((k, tn), BF16)],
        compiler_params=_cparams(("arbitrary", "arbitrary")), name=name)(*args)


def _merge_kernel(*refs, n_branch):
    y_refs, g_refs = refs[:n_branch], refs[n_branch:2 * n_branch]
    w_ref, o_ref, wb_ref = refs[2 * n_branch:]

    @pl.when(pl.program_id(1) == 0)
    def _():
        wb_ref[...] = w_ref[...].astype(BF16)

    acc = _sigmoid(g_refs[0][...]) * _dot(y_refs[0][...], wb_ref[0])
    for n in range(1, n_branch):
        acc = acc + _sigmoid(g_refs[n][...]) * _dot(y_refs[n][...], wb_ref[n])
    o_ref[...] = acc.astype(o_ref.dtype)


def _merge_call(ys, z, w_branch, l, gate_col0, *, name):
    r = z.shape[0]
    db = ys[0].shape[1]
    _, nbr, _, d = w_branch.shape
    tn = _pick_tile(d, 1024, LANES)
    tm = _pick_tile(r, 384, BF16_SUBLANES)
    assert gate_col0 % tn == 0 and d % tn == 0 and all(y.shape[0] >= r for y in ys)
    g0 = gate_col0 // tn
    per = d // tn
    y_spec = pl.BlockSpec((tm, db), lambda j, i: (i, 0))
    g_specs = [pl.BlockSpec((tm, tn), lambda j, i, n=n: (i, g0 + per * n + j)) for n in range(nbr)]
    return pl.pallas_call(
        functools.partial(_merge_kernel, n_branch=nbr), grid=(d // tn, r // tm),
        in_specs=[y_spec] * nbr + g_specs + [pl.BlockSpec((None, nbr, db, tn), lambda j, i: (l, 0, 0, j))],
        out_specs=pl.BlockSpec((tm, tn), lambda j, i: (i, j)),
        out_shape=jax.ShapeDtypeStruct((r, d), BF16),
        scratch_shapes=[pltpu.VMEM((nbr, db, tn), BF16)],
        compiler_params=_cparams(("arbitrary", "arbitrary")), name=name)(*ys, *([z] * nbr), w_branch)


def _shift_rows(x, prev8, k, row):
    rolled = pltpu.roll(x, k, 0)
    head = jnp.where(row[:SUBLANES] < k, pltpu.roll(prev8, k, 0), rolled[:SUBLANES])
    return jnp.concatenate([head, rolled[SUBLANES:]], axis=0)


def _chunk_conv(x, prev8, w_ref, b_ref, row):
    y = b_ref[...] + w_ref[CONV_W - 1:CONV_W, :] * x
    for k in range(1, CONV_W):
        y = y + w_ref[CONV_W - 1 - k:CONV_W - k, :] * _shift_rows(x, prev8, k, row)
    return y


def _rglru_coeffs(xc, wa_ref, ba_ref, wx_ref, bx_ref, lam_ref):
    r = _sigmoid(_blockdiag(xc, wa_ref) + ba_ref[...])
    i = _sigmoid(_blockdiag(xc, wx_ref) + bx_ref[...])
    log_a = -RG_C * r * _softplus(-lam_ref[...])
    a = jnp.exp(log_a)
    mult = jnp.sqrt(1.0 - a * a)
    return a, mult * i * xc


def _per_sequence(body, nseq, n_rows_in, n_const, n_rows_out, n_state_out):
    def kernel(*refs):
        rows_in = refs[:n_rows_in]
        consts = refs[n_rows_in:n_rows_in + n_const]
        o = n_rows_in + n_const
        rows_out = refs[o:o + n_rows_out]
        state_out = refs[o + n_rows_out:o + n_rows_out + n_state_out]
        scratch = refs[o + n_rows_out + n_state_out:]
        L = rows_in[0].shape[0] // nseq

        @pl.when(pl.program_id(0) <= _last_chunk())
        def _():
            for b in range(nseq):
                rows = lambda r, b=b: r.at[pl.ds(b * L, L), :]
                body(*[rows(r) for r in rows_in], *consts, *[rows(r) for r in rows_out],
                     *[r.at[pl.ds(b, 1)] for r in state_out], *[r.at[b] for r in scratch])

        @pl.when(pl.program_id(0) > _last_chunk())
        def _():
            for r in rows_out:
                r[...] = jnp.zeros_like(r)
    return kernel


def _last_chunk():
    return pl.num_programs(0) - 2


def _pa_kernel(x_ref, cw_ref, cb_ref, wa_ref, ba_ref, wx_ref, bx_ref, lam_ref,
               y_ref, hl_ref, cv_ref, prev_ref, h_ref, *, pad):
    c = pl.program_id(0)
    last = _last_chunk()
    L = x_ref.shape[0]
    row = lax.broadcasted_iota(jnp.int32, (L, 1), 0)
    valid = jnp.logical_or(c > 0, row >= pad)

    @pl.when(c == 0)
    def _():
        prev_ref[...] = jnp.zeros_like(prev_ref)
        h_ref[...] = jnp.zeros_like(h_ref)

    x = jnp.where(valid, x_ref[...], 0.0)
    xc = _chunk_conv(x, prev_ref[...], cw_ref, cb_ref, row)
    prev_ref[...] = x[L - SUBLANES:]
    a, b = _rglru_coeffs(xc, wa_ref, ba_ref, wx_ref, bx_ref, lam_ref)
    a = jnp.where(valid, a, 1.0)
    b = jnp.where(valid, b, 0.0)
    hs = []
    for j in range(x.shape[1] // LANES):
        sl = slice(j * LANES, (j + 1) * LANES)
        aj, bj = a[:, sl], b[:, sl]
        d = 1
        while d < L:
            keep = row >= d
            a_s = jnp.where(keep, pltpu.roll(aj, d, 0), 1.0)
            b_s = jnp.where(keep, pltpu.roll(bj, d, 0), 0.0)
            bj = aj * b_s + bj
            aj = aj * a_s
            d *= 2
        hs.append(aj * h_ref[0:1, sl] + bj)
    h = jnp.concatenate(hs, axis=-1)
    h_ref[0:1, :] = h[L - 1:L]
    y_ref[...] = h.astype(y_ref.dtype)

    @pl.when(c == last)
    def _():
        hl_ref[0] = h[L - 1:L]
        cv_ref[0] = x[L - (CONV_W - 1):]


def _pa_call(z, nseq, nch, pad, p, *, name):
    rows = nseq * nch * CHUNK
    c = p['rg_conv_w'].shape[1]
    full = lambda a: pl.BlockSpec(a.shape, lambda k: (0,) * a.ndim)
    whole = lambda *shape: pl.BlockSpec(shape, lambda k: (0,) * len(shape))
    params = [p['rg_conv_w'], p['rg_conv_b'], p['rg_wa'], p['rg_ba'], p['rg_wx'], p['rg_bx'], p['rg_lambda']]
    return pl.pallas_call(
        _per_sequence(functools.partial(_pa_kernel, pad=pad), nseq, 1, len(params), 1, 2), grid=(nch + 1,),
        in_specs=[pl.BlockSpec((nseq * CHUNK, c), lambda k: (jnp.minimum(k, nch - 1), 0))]
        + [full(a) for a in params],
        out_specs=[pl.BlockSpec((nseq * CHUNK, c), lambda k: (k, 0)),
                   whole(nseq, 1, c), whole(nseq, CONV_W - 1, c)],
        out_shape=[jax.ShapeDtypeStruct((rows + nseq * CHUNK, c), BF16),
                   jax.ShapeDtypeStruct((nseq, 1, c), F32),
                   jax.ShapeDtypeStruct((nseq, CONV_W - 1, c), F32)],
        scratch_shapes=[pltpu.VMEM((nseq, SUBLANES, c), F32), pltpu.VMEM((nseq, SUBLANES, c), F32)],
        compiler_params=_cparams(("arbitrary",)), name=name)(z, *params)


def _step_conv(x, cs_ref, w_ref, b_ref):
    y = b_ref[...] + w_ref[CONV_W - 1:CONV_W, :] * x
    for k in range(CONV_W - 1):
        y = y + w_ref[k:k + 1, :] * cs_ref[k]
    return y


def _step_conv_state(x, cs_ref, out_ref):
    for k in range(CONV_W - 2):
        out_ref[k] = cs_ref[k + 1]
    out_ref[CONV_W - 2] = x


def _sa_kernel(x_ref, cs_ref, h0_ref, cw_ref, cb_ref, wa_ref, ba_ref, wx_ref, bx_ref, lam_ref,
               y_ref, hn_ref, cv_ref):
    x = x_ref[...]
    xc = _step_conv(x, cs_ref, cw_ref, cb_ref)
    a, b = _rglru_coeffs(xc, wa_ref, ba_ref, wx_ref, bx_ref, lam_ref)
    h = a * h0_ref[...] + b
    hn_ref[...] = h
    y_ref[...] = h.astype(y_ref.dtype)
    _step_conv_state(x, cs_ref, cv_ref)


def _sa_call(z, row0, bs, y_all, cs, h0, p, *, name):
    c = h0.shape[1]
    assert row0 % bs == 0
    full = lambda a: pl.BlockSpec(a.shape, lambda i: (0,) * a.ndim)
    params = [p['rg_conv_w'], p['rg_conv_b'], p['rg_wa'], p['rg_ba'], p['rg_wx'], p['rg_bx'], p['rg_lambda']]
    specs, extra, drop, aliases = _alias_args(3 + len(params), [(y_all, 0)])
    return pl.pallas_call(
        _skip_refs(_sa_kernel, drop), grid=(1,),
        in_specs=[pl.BlockSpec((bs, c), lambda i: (row0 // bs, 0)), full(cs), full(h0)] + [full(a) for a in params]
        + specs,
        out_specs=[pl.BlockSpec((bs, c), lambda i: (row0 // bs, 0)), pl.BlockSpec((bs, c), lambda i: (0, 0)),
                   pl.BlockSpec(cs.shape, lambda i: (0, 0, 0))],
        out_shape=[jax.ShapeDtypeStruct(y_all.shape, BF16), jax.ShapeDtypeStruct((bs, c), F32),
                   jax.ShapeDtypeStruct(cs.shape, F32)],
        input_output_aliases=aliases,
        compiler_params=_cparams(("arbitrary",)), name=name)(z, cs, h0, *params, *extra)


def _ret_log_g(h):
    return math.log1p(-(2.0 ** (-5.0 - h)))


def _rope(x, cos2, sin2):
    return x * cos2 + pltpu.roll(x, x.shape[1] // 2, 1) * sin2


def _pb_kernel(q_ref, k_ref, v_ref, g_ref, cos_ref, sin_ref, gn_ref, y_ref, so_ref, s_ref, *, pad, heads):
    c = pl.program_id(0)
    last = _last_chunk()
    L = q_ref.shape[0]
    dk = q_ref.shape[1] // heads
    dv = v_ref.shape[1] // heads
    row = lax.broadcasted_iota(jnp.int32, (L, 1), 0)
    valid = jnp.logical_or(c > 0, row >= pad)

    @pl.when(c == 0)
    def _():
        s_ref[...] = jnp.zeros_like(s_ref)

    t = row.astype(F32)
    rel = t - lax.broadcasted_iota(jnp.int32, (1, L), 1).astype(F32)
    cos2, sin2 = cos_ref[...], sin_ref[...]
    for h in range(heads):
        lg = _ret_log_g(h)
        qh = jnp.where(valid, _rope(q_ref[:, h * dk:(h + 1) * dk], cos2, sin2), 0.0)
        kh = jnp.where(valid, _rope(k_ref[:, h * dk:(h + 1) * dk], cos2, sin2), 0.0) * (dk ** -0.5)
        vh = jnp.where(valid, v_ref[:, h * dv:(h + 1) * dv], 0.0)
        decay = jnp.where(rel >= 0, jnp.exp(jnp.maximum(rel, 0.0) * lg), 0.0)
        qb, vb = qh.astype(BF16), vh.astype(BF16)
        scores = _dot_nt(qb, kh.astype(BF16)) * decay
        s_old = s_ref[h]
        o = _dot(scores.astype(BF16), vb) + _dot(qb, s_old.astype(BF16)) * jnp.exp((t + 1.0) * lg)
        k_dec = kh * jnp.exp((L - 1.0 - t) * lg)
        s_ref[h] = math.exp(L * lg) * s_old + _dot_tn(k_dec.astype(BF16), vb)
        sl = slice(h * dv, (h + 1) * dv)
        y_ref[:, sl] = (_head_norm(o) * gn_ref[:, sl] * _silu(g_ref[:, sl])).astype(y_ref.dtype)

    @pl.when(c == last)
    def _():
        so_ref[0] = s_ref[...]


def _pb_call(z, nseq, nch, pad, heads, dk, dv, cos2, sin2, ret_norm, *, name):
    rows = nseq * nch * CHUNK
    hk, hv = heads * dk, heads * dv
    assert hv % hk == 0
    q0 = hv // hk
    blk = nseq * CHUNK
    return pl.pallas_call(
        _per_sequence(functools.partial(_pb_kernel, pad=pad, heads=heads), nseq, 4, 3, 1, 1), grid=(nch + 1,),
        in_specs=[pl.BlockSpec((blk, hk), lambda k: (jnp.minimum(k, nch - 1), q0)),
                  pl.BlockSpec((blk, hk), lambda k: (jnp.minimum(k, nch - 1), q0 + 1)),
                  pl.BlockSpec((blk, hv), lambda k: (jnp.minimum(k, nch - 1), 2)),
                  pl.BlockSpec((blk, hv), lambda k: (jnp.minimum(k, nch - 1), 3)),
                  pl.BlockSpec((CHUNK, dk), lambda k: (jnp.minimum(k, nch - 1), 0)),
                  pl.BlockSpec((CHUNK, dk), lambda k: (jnp.minimum(k, nch - 1), 0)),
                  pl.BlockSpec((1, hv), lambda k: (0, 0))],
        out_specs=[pl.BlockSpec((blk, hv), lambda k: (k, 0)),
                   pl.BlockSpec((nseq, heads, dk, dv), lambda k: (0, 0, 0, 0))],
        out_shape=[jax.ShapeDtypeStruct((rows + blk, hv), BF16),
                   jax.ShapeDtypeStruct((nseq, heads, dk, dv), F32)],
        scratch_shapes=[pltpu.VMEM((nseq, heads, dk, dv), F32)],
        compiler_params=_cparams(("arbitrary",)), name=name)(z, z, z, z, cos2, sin2, ret_norm)


def _first_pass_only(body, state_out_ref):
    @pl.when(pl.program_id(0) == 0)
    def _():
        body()

    @pl.when(pl.program_id(0) > 0)
    def _():
        state_out_ref[...] = jnp.zeros_like(state_out_ref)


def _pass_maps(l, prev, nblk, depth):
    if prev is not None:
        return 1, (lambda g, i: i), (lambda g, i: l)
    assert l == 0
    return depth, (lambda g, i: jnp.where(g == 0, i, nblk - 1)), (lambda g, i: g)


def _sb_kernel(*refs, heads):
    _first_pass_only(functools.partial(_sb_body, *refs, heads=heads), refs[-1])


def _sb_body(q_ref, k_ref, v_ref, g_ref, s_ref, cos_ref, sin_ref, gn_ref, y_ref, so_ref, *, heads):
    nb = q_ref.shape[0]
    dk = q_ref.shape[1] // heads
    dv = v_ref.shape[1] // heads
    row = lax.broadcasted_iota(jnp.int32, (nb, 1), 0)
    cos2, sin2 = cos_ref[...], sin_ref[...]
    for h in range(heads):
        g = math.exp(_ret_log_g(h))
        qh = _rope(q_ref[:, h * dk:(h + 1) * dk], cos2, sin2)
        kh = _rope(k_ref[:, h * dk:(h + 1) * dk], cos2, sin2) * (dk ** -0.5)
        vh = v_ref[:, h * dv:(h + 1) * dv]
        qb, vb = qh.astype(BF16), vh.astype(BF16)
        cross = jnp.zeros((nb, dv), F32)
        for i in range(nb):
            s_old = s_ref[i, h]
            cross = jnp.where(row == i, _dot(qb, s_old.astype(BF16)), cross)
            so_ref[i, h] = g * s_old + _dot_tn(jnp.where(row == i, kh, 0.0), vh)
        qk = jnp.sum(qb.astype(F32) * kh.astype(BF16).astype(F32), axis=-1, keepdims=True)
        o = qk * vb.astype(F32) + g * cross
        sl = slice(h * dv, (h + 1) * dv)
        y_ref[:, sl] = (_head_norm(o) * gn_ref[:, sl] * _silu(g_ref[:, sl])).astype(y_ref.dtype)


def _sb_call(z, row0, bs, y_all, heads, dk, dv, state, l, prev, cos2, sin2, ret_norm, *, name):
    hk, hv = heads * dk, heads * dv
    nb = SUBLANES
    assert bs % nb == 0 and row0 % nb == 0
    r0 = row0 // nb
    q0 = hv // hk
    passes, blk, lay = _pass_maps(l, prev, bs // nb, state.shape[0])
    specs, extra, drop, aliases = _alias_args(8, [(y_all, 0), (prev, 1)])
    return pl.pallas_call(
        _skip_refs(functools.partial(_sb_kernel, heads=heads), drop),
        grid=(passes, bs // nb),
        in_specs=[pl.BlockSpec((nb, hk), lambda g, i: (r0 + blk(g, i), q0)),
                  pl.BlockSpec((nb, hk), lambda g, i: (r0 + blk(g, i), q0 + 1)),
                  pl.BlockSpec((nb, hv), lambda g, i: (r0 + blk(g, i), 2)),
                  pl.BlockSpec((nb, hv), lambda g, i: (r0 + blk(g, i), 3)),
                  pl.BlockSpec((None, nb, heads, dk, dv), lambda g, i: (l, blk(g, i), 0, 0, 0)),
                  pl.BlockSpec((1, dk), lambda g, i: (0, 0)),
                  pl.BlockSpec((1, dk), lambda g, i: (0, 0)),
                  pl.BlockSpec((1, hv), lambda g, i: (0, 0))] + specs,
        out_specs=[pl.BlockSpec((nb, hv), lambda g, i: (r0 + blk(g, i), 0)),
                   pl.BlockSpec((None, nb, heads, dk, dv), lambda g, i: (lay(g, i), i, 0, 0, 0))],
        out_shape=[jax.ShapeDtypeStruct(y_all.shape, BF16), jax.ShapeDtypeStruct(state.shape, F32)],
        input_output_aliases=aliases,
        compiler_params=_cparams(("arbitrary", "arbitrary")), name=name)(
            z, z, z, z, state, cos2, sin2, ret_norm, *extra)


def _mlstm_qkv_gates(x, xc, wq_ref, wk_ref, wv_ref, wif_ref, bif_ref, dh):
    q = _blockdiag(xc, wq_ref)
    k = _blockdiag(xc, wk_ref, scale=dh ** -0.5)
    v = _blockdiag(x, wv_ref)
    c = x.shape[1]
    gates = (_dot(q.astype(BF16), wif_ref[0:c, :]) + _dot(k.astype(BF16), wif_ref[c:2 * c, :])
             + _dot(v.astype(BF16), wif_ref[2 * c:3 * c, :]) + bif_ref[...])
    return q, k, v, gates


def _pc_kernel(x_ref, o_ref, cw_ref, cb_ref, wq_ref, wk_ref, wv_ref, wif_ref, bif_ref, gn_ref, sk_ref,
               y_ref, co_ref, no_ref, mo_ref, cv_ref, prev_ref, c_ref, n_ref, m_ref, *, pad, heads):
    c = pl.program_id(0)
    last = _last_chunk()
    L = x_ref.shape[0]
    dh = x_ref.shape[1] // heads
    row = lax.broadcasted_iota(jnp.int32, (L, 1), 0)
    valid = jnp.logical_or(c > 0, row >= pad)

    @pl.when(c == 0)
    def _():
        prev_ref[...] = jnp.zeros_like(prev_ref)
        c_ref[...] = jnp.zeros_like(c_ref)
        n_ref[...] = jnp.zeros_like(n_ref)
        m_ref[...] = jnp.zeros_like(m_ref)

    x = jnp.where(valid, x_ref[...], 0.0)
    xc = _silu(_chunk_conv(x, prev_ref[...], cw_ref, cb_ref, row))
    prev_ref[...] = x[L - SUBLANES:]
    q, k, v, gates = _mlstm_qkv_gates(x, xc, wq_ref, wk_ref, wv_ref, wif_ref, bif_ref, dh)
    lane = lax.broadcasted_iota(jnp.int32, gates.shape, 1)
    is_i = lane < heads
    ig_c = jnp.where(jnp.logical_and(valid, is_i), gates, NEG)
    lf_c = jnp.where(valid, _log_sigmoid(gates), 0.0)
    ig_r = ig_c.T
    lf_r = lf_c.T
    ti = lax.broadcasted_iota(jnp.int32, (L, L), 0)
    si = lax.broadcasted_iota(jnp.int32, (L, L), 1)
    causal = ti >= si
    ones_lower = jnp.where(causal, 1.0, 0.0).astype(BF16)
    ones_upper = jnp.where(si >= ti, 1.0, 0.0).astype(BF16)
    b_c = sum(_dot(ones_lower, part) for part in _split3(lf_c))
    b_r = sum(_dot(part, ones_upper) for part in _split3(lf_r))
    m_all = m_ref[...]
    m_lane = lax.broadcasted_iota(jnp.int32, m_all.shape, 1)
    m_new = jnp.zeros_like(m_all)
    for h in range(heads):
        sl = slice(h * dh, (h + 1) * dh)
        bc = b_c[:, heads + h:heads + h + 1]
        br = b_r[heads + h:heads + h + 1, :]
        igr = ig_r[h:h + 1, :]
        igc = ig_c[:, h:h + 1]
        m_prev = m_all[0:1, h:h + 1]
        log_d = jnp.where(causal, bc - br + igr, NEG)
        log_inter = bc + m_prev
        m_t = jnp.maximum(log_inter, jnp.max(log_d, axis=-1, keepdims=True))
        d_m = jnp.exp(log_d - m_t)
        w_inter = jnp.exp(log_inter - m_t)
        qh, kh, vh = q[:, sl], k[:, sl], v[:, sl]
        qb, kb, vb = qh.astype(BF16), kh.astype(BF16), vh.astype(BF16)
        scores = _dot_nt(qb, kb) * d_m
        c_old = c_ref[h]
        n_old = n_ref[h:h + 1, :]
        num = _dot(scores.astype(BF16), vb) + w_inter * _dot(qb, c_old.astype(BF16))
        den = jnp.sum(scores, axis=-1, keepdims=True) + w_inter * jnp.sum(qh * n_old, axis=-1, keepdims=True)
        hh = num / jnp.maximum(jnp.abs(den), jnp.exp(-m_t))
        m_end = m_t[L - 1:L]
        b_last = bc[L - 1:L]
        w_c = jnp.exp(b_last - bc + igc - m_end)
        decay_c = jnp.exp(b_last + m_prev - m_end)
        kw = kh * w_c
        c_ref[h] = decay_c * c_old + _dot_tn(kw.astype(BF16), vb)
        n_ref[h:h + 1, :] = decay_c * n_old + jnp.sum(kw, axis=0, keepdims=True)
        m_new = jnp.where(m_lane == h, m_end, m_new)
        hm = _sigmoid(o_ref[:, sl]) * hh
        y_ref[:, sl] = (_head_norm(hm) * gn_ref[:, sl] + sk_ref[:, sl] * xc[:, sl]).astype(y_ref.dtype)
    m_ref[...] = m_new

    @pl.when(c == last)
    def _():
        co_ref[0] = c_ref[...]
        no_ref[0] = n_ref[...]
        mo_ref[0] = m_ref[...]
        cv_ref[0] = x[L - (CONV_W - 1):]


def _pc_call(z, nseq, nch, pad, heads, xcol, p, *, name):
    rows = nseq * nch * CHUNK
    c = p['m_conv_w'].shape[1]
    dh = c // heads
    full = lambda a: pl.BlockSpec(a.shape, lambda k: (0,) * a.ndim)
    whole = lambda *shape: pl.BlockSpec(shape, lambda k: (0,) * len(shape))
    params = [p['m_conv_w'], p['m_conv_b'], p['m_wq'], p['m_wk'], p['m_wv'], p['m_w_if'], p['m_b_if'],
              p['m_norm'], p['m_skip']]
    blk = nseq * CHUNK
    return pl.pallas_call(
        _per_sequence(functools.partial(_pc_kernel, pad=pad, heads=heads), nseq, 2, len(params), 1, 4),
        grid=(nch + 1,),
        in_specs=[pl.BlockSpec((blk, c), lambda k: (jnp.minimum(k, nch - 1), xcol)),
                  pl.BlockSpec((blk, c), lambda k: (jnp.minimum(k, nch - 1), xcol + 1))] + [full(a) for a in params],
        out_specs=[pl.BlockSpec((blk, c), lambda k: (k, 0)),
                   whole(nseq, heads, dh, dh), whole(nseq, SUBLANES, dh), whole(nseq, SUBLANES, LANES),
                   whole(nseq, CONV_W - 1, c)],
        out_shape=[jax.ShapeDtypeStruct((rows + blk, c), BF16),
                   jax.ShapeDtypeStruct((nseq, heads, dh, dh), F32),
                   jax.ShapeDtypeStruct((nseq, SUBLANES, dh), F32),
                   jax.ShapeDtypeStruct((nseq, SUBLANES, LANES), F32),
                   jax.ShapeDtypeStruct((nseq, CONV_W - 1, c), F32)],
        scratch_shapes=[pltpu.VMEM((nseq, SUBLANES, c), F32), pltpu.VMEM((nseq, heads, dh, dh), F32),
                        pltpu.VMEM((nseq, SUBLANES, dh), F32), pltpu.VMEM((nseq, SUBLANES, LANES), F32)],
        compiler_params=_cparams(("arbitrary",)), name=name)(z, z, *params)


def _sc_kernel(*refs, heads):
    _first_pass_only(functools.partial(_sc_body, *refs, heads=heads), refs[-4])


def _sc_body(x_ref, o_ref, cs_ref, c_ref, n_ref, m_ref, cw_ref, cb_ref, wq_ref, wk_ref, wv_ref, wif_ref, bif_ref,
             gn_ref, sk_ref, y_ref, co_ref, no_ref, mo_ref, cv_ref, *, heads):
    nb = x_ref.shape[0]
    dh = x_ref.shape[1] // heads
    row = lax.broadcasted_iota(jnp.int32, (nb, 1), 0)
    x = x_ref[...]
    xc = _silu(_step_conv(x, cs_ref, cw_ref, cb_ref))
    q, k, v, gates = _mlstm_qkv_gates(x, xc, wq_ref, wk_ref, wv_ref, wif_ref, bif_ref, dh)
    lf = _log_sigmoid(gates)
    m_old = m_ref[...]
    lane = lax.broadcasted_iota(jnp.int32, m_old.shape, 1)
    m_new = jnp.zeros_like(m_old)
    for h in range(heads):
        sl = slice(h * dh, (h + 1) * dh)
        ig = gates[:, h:h + 1]
        log_inter = lf[:, heads + h:heads + h + 1] + m_old[:, h:h + 1]
        m_t = jnp.maximum(log_inter, ig)
        d_m = jnp.exp(ig - m_t)
        w_inter = jnp.exp(log_inter - m_t)
        qh, kh, vh = q[:, sl], k[:, sl], v[:, sl]
        qb, kb, vb = qh.astype(BF16), kh.astype(BF16), vh.astype(BF16)
        sc = jnp.sum(qb.astype(F32) * kb.astype(F32), axis=-1, keepdims=True) * d_m
        n_old = n_ref[:, sl]
        kw = kh * d_m
        qc = jnp.zeros((nb, dh), F32)
        for i in range(nb):
            c_old = c_ref[i, h]
            qc = jnp.where(row == i, _dot(qb, c_old.astype(BF16)), qc)
            co_ref[i, h] = (w_inter[i:i + 1] * c_old
                            + _dot_tn(jnp.where(row == i, kw, 0.0), vh))
        num = sc * vb.astype(F32) + w_inter * qc
        den = sc + w_inter * jnp.sum(qh * n_old, axis=-1, keepdims=True)
        hh = num / jnp.maximum(jnp.abs(den), jnp.exp(-m_t))
        no_ref[:, sl] = w_inter * n_old + kw
        m_new = jnp.where(lane == h, m_t, m_new)
        hm = _sigmoid(o_ref[:, sl]) * hh
        y_ref[:, sl] = (_head_norm(hm) * gn_ref[:, sl] + sk_ref[:, sl] * xc[:, sl]).astype(y_ref.dtype)
    mo_ref[...] = m_new
    _step_conv_state(x, cs_ref, cv_ref)


def _sc_call(z, row0, bs, y_all, heads, xcol, cs, c_state, l, prev, n_state, m_state, p, *, name):
    c = p['m_conv_w'].shape[1]
    dh = c // heads
    nb = SUBLANES
    assert bs % nb == 0 and row0 % nb == 0
    r0 = row0 // nb
    full = lambda a: pl.BlockSpec(a.shape, lambda g, i: (0,) * a.ndim)
    params = [p['m_conv_w'], p['m_conv_b'], p['m_wq'], p['m_wk'], p['m_wv'], p['m_w_if'], p['m_b_if'],
              p['m_norm'], p['m_skip']]
    passes, blk, lay = _pass_maps(l, prev, bs // nb, c_state.shape[0])
    cs_spec = pl.BlockSpec((CONV_W - 1, nb, c), lambda g, i: (0, blk(g, i), 0))
    row_spec = lambda w: pl.BlockSpec((nb, w), lambda g, i: (blk(g, i), 0))
    specs, extra, drop, aliases = _alias_args(6 + len(params), [(y_all, 0), (prev, 1)])
    return pl.pallas_call(
        _skip_refs(functools.partial(_sc_kernel, heads=heads), drop),
        grid=(passes, bs // nb),
        in_specs=[pl.BlockSpec((nb, c), lambda g, i: (r0 + blk(g, i), xcol)),
                  pl.BlockSpec((nb, c), lambda g, i: (r0 + blk(g, i), xcol + 1)),
                  cs_spec,
                  pl.BlockSpec((None, nb, heads, dh, dh), lambda g, i: (l, blk(g, i), 0, 0, 0)),
                  row_spec(c), row_spec(LANES)] + [full(a) for a in params] + specs,
        out_specs=[pl.BlockSpec((nb, c), lambda g, i: (r0 + blk(g, i), 0)),
                   pl.BlockSpec((None, nb, heads, dh, dh), lambda g, i: (lay(g, i), i, 0, 0, 0)),
                   row_spec(c), row_spec(LANES), cs_spec],
        out_shape=[jax.ShapeDtypeStruct(y_all.shape, BF16), jax.ShapeDtypeStruct(c_state.shape, F32),
                   jax.ShapeDtypeStruct((bs, c), F32), jax.ShapeDtypeStruct((bs, LANES), F32),
                   jax.ShapeDtypeStruct(cs.shape, F32)],
        input_output_aliases=aliases,
        compiler_params=_cparams(("arbitrary", "arbitrary")), name=name)(
            z, z, cs, c_state, n_state, m_state, *params, *extra)


def _ffn_kernel(sbe_ref, sbs_ref, sbn_ref, nv_ref, xs_hbm, wg_ref, wu_ref, wd_ref, ys_hbm,
                acc_ref, xp_ref, yp_ref, h_ref, wgb_ref, wub_ref, wdb_ref, zero_ref, isem, osem, zsem, *, nka, nkb):
    del sbe_ref
    j, s = pl.program_id(0), pl.program_id(1)
    nblk = sbn_ref[j]
    max_blk = acc_ref.shape[0] // MOE_ROWS
    f = wgb_ref.shape[1]
    dn = wdb_ref.shape[0]

    @pl.when(jnp.logical_and(j == 0, s == 0))
    def _():
        zero_ref[...] = jnp.zeros_like(zero_ref)
        xp_ref[...] = jnp.zeros_like(xp_ref)

    n_sb = pl.num_programs(0)

    def hbm_rows(ref, jj, p):
        return ref.at[pl.ds(pl.multiple_of((sbs_ref[jj] + p) * MOE_ROWS, MOE_ROWS), MOE_ROWS), :]

    def vmem_rows(ref, p):
        return ref.at[pl.ds(p * MOE_ROWS, MOE_ROWS), :]

    def x_copy(jj):
        return lambda p: pltpu.make_async_copy(hbm_rows(xs_hbm, jj, p), vmem_rows(xp_ref, p), isem)

    def y_copy(jj):
        return lambda p: pltpu.make_async_copy(vmem_rows(yp_ref, p), hbm_rows(ys_hbm, jj, p), osem)

    def for_blocks(fn, n):
        for p in range(max_blk):
            @pl.when(p < n)
            def _():
                fn(p)

    def block_loop(body):
        per = FFN_ROWS // MOE_ROWS

        def step(b, carry):
            body(pl.ds(pl.multiple_of(b * FFN_ROWS, FFN_ROWS), FFN_ROWS))
            return carry
        lax.fori_loop(0, (nblk + per - 1) // per, step, 0)

    @pl.when(nblk > 0)
    def _():
        @pl.when(jnp.logical_and(s == 0, j == 0))
        def _():
            for_blocks(lambda p: x_copy(j)(p).start(), nblk)

        @pl.when(s == 0)
        def _():
            for_blocks(lambda p: x_copy(j)(p).wait(), nblk)

        @pl.when(s == nka)
        def _():
            jn = jnp.minimum(j + 1, n_sb - 1)
            n_next = jnp.where(j + 1 < n_sb, sbn_ref[jn], 0)
            for_blocks(lambda p: x_copy(jn)(p).start(), n_next)

        for c in range(nka):
            @pl.when(s == c)
            def _(c=c):
                wgb_ref[...] = wg_ref[...].astype(BF16)
                wub_ref[...] = wu_ref[...].astype(BF16)

                def body(rows):
                    x = _unpack_half(xp_ref[rows, :], c).astype(BF16)
                    g = _dot(x, wgb_ref[...])
                    u = _dot(x, wub_ref[...])
                    if c > 0:
                        g = g + acc_ref[rows, 0:f]
                        u = u + acc_ref[rows, f:2 * f]
                    if c < nka - 1:
                        acc_ref[rows, 0:f] = g
                        acc_ref[rows, f:2 * f] = u
                    else:
                        h_ref[rows, :] = (_silu(g) * u).astype(BF16)
                block_loop(body)

        for c in range(nkb):
            @pl.when(s == nka + c)
            def _(c=c):
                wdb_ref[...] = wd_ref[...].astype(BF16)

                def body(rows):
                    y = _dot(h_ref[rows, c * dn:(c + 1) * dn], wdb_ref[...])
                    if c > 0:
                        y = y + acc_ref[rows, :]
                    acc_ref[rows, :] = y
                block_loop(body)

                if c == nkb - 1:
                    @pl.when(j > 0)
                    def _():
                        jp = jnp.maximum(j - 1, 0)
                        for_blocks(lambda p: y_copy(jp)(p).wait(), sbn_ref[jp])

                    def pack_block(p):
                        sl = slice(p * MOE_ROWS, (p + 1) * MOE_ROWS)
                        yp_ref[sl, :] = _pack_halves(acc_ref[sl, :])
                    for_blocks(pack_block, nblk)
                    for_blocks(lambda p: y_copy(j)(p).start(), nblk)

    @pl.when(jnp.logical_and(j == n_sb - 1, s == pl.num_programs(1) - 1))
    def _():
        jl = jnp.maximum(nv_ref[1] - 1, 0)
        for_blocks(lambda p: y_copy(jl)(p).wait(), jnp.where(nv_ref[1] > 0, sbn_ref[jl], 0))
        _zero_blocks(zero_ref, ys_hbm, nv_ref[0], ys_hbm.shape[0] // MOE_ROWS, zsem)


def _ffn_call(xs, tables, wg, wu, wd, l, *, name):
    sb_e, sb_blk0, sb_nblk, n_valid = tables
    ns, dp = xs.shape
    d, f = wg.shape[2], wg.shape[3]
    nsb = sb_e.shape[0]
    nka, nkb = d // FFN_GU_CHUNK, f // FFN_DN_CHUNK
    assert xs.dtype == U32 and dp * 2 == d and nka == 2 and FFN_GU_CHUNK == dp
    assert f % FFN_DN_CHUNK == 0 and 2 * f <= d and MOE_SB_ROWS % FFN_ROWS == 0 and FFN_ROWS % MOE_ROWS == 0

    def gu_map(j, s, sbe, sbs, sbn, nv):
        return (l, sbe[j], jnp.where(j < nv[1], jnp.minimum(s, nka - 1), nka - 1), 0)

    def dn_map(j, s, sbe, sbs, sbn, nv):
        prev = jnp.maximum(j - 1, 0)
        e = jnp.where(s >= nka, sbe[j], sbe[prev])
        c = jnp.where(j < nv[1], jnp.where(s >= nka, s - nka, jnp.where(j == 0, 0, nkb - 1)), nkb - 1)
        return (l, e, c, 0)

    grid_spec = pltpu.PrefetchScalarGridSpec(
        num_scalar_prefetch=4, grid=(nsb, nka + nkb),
        in_specs=[pl.BlockSpec(memory_space=pl.ANY),
                  pl.BlockSpec((None, None, FFN_GU_CHUNK, f), gu_map),
                  pl.BlockSpec((None, None, FFN_GU_CHUNK, f), gu_map),
                  pl.BlockSpec((None, None, FFN_DN_CHUNK, d), dn_map)],
        out_specs=pl.BlockSpec(memory_space=pl.ANY),
        scratch_shapes=[pltpu.VMEM((MOE_SB_ROWS, d), F32), pltpu.VMEM((MOE_SB_ROWS, dp), U32),
                        pltpu.VMEM((MOE_SB_ROWS, dp), U32), pltpu.VMEM((MOE_SB_ROWS, f), BF16),
                        pltpu.VMEM((FFN_GU_CHUNK, f), BF16), pltpu.VMEM((FFN_GU_CHUNK, f), BF16),
                        pltpu.VMEM((FFN_DN_CHUNK, d), BF16), pltpu.VMEM((MOE_ROWS, dp), U32),
                        pltpu.SemaphoreType.DMA(()), pltpu.SemaphoreType.DMA(()), pltpu.SemaphoreType.DMA(())])
    return pl.pallas_call(
        functools.partial(_ffn_kernel, nka=nka, nkb=nkb), grid_spec=grid_spec,
        out_shape=jax.ShapeDtypeStruct((ns, dp), U32),
        compiler_params=_cparams(("arbitrary", "arbitrary")), name=name)(
            sb_e, sb_blk0, sb_nblk, n_valid, xs, wg, wu, wd)


def _take(table, idx):
    return jnp.sum(jnp.where(idx[:, None] == jnp.arange(table.shape[0])[None, :], table[None, :], 0), axis=1)


def _route_tables(ids, tile_cnt, r, tm):
    eidx = jnp.arange(N_EXPERTS, dtype=jnp.int32)
    cnt_t = tile_cnt[:, 0, :N_EXPERTS].astype(jnp.int32)
    counts = jnp.sum(cnt_t, axis=0)
    tile_off = jnp.cumsum(cnt_t, axis=0) - cnt_t
    padded = (counts + MOE_ROWS - 1) // MOE_ROWS * MOE_ROWS
    pad_end = jnp.cumsum(padded)
    pad_start = pad_end - padded
    base = jnp.repeat(pad_start[None, :] + tile_off, tm, axis=0)
    e = ids[:, :TOP_K]
    rank = ids[:, TOP_K:2 * TOP_K]
    dest = rank + jnp.sum(jnp.where(e[:, :, None] == eidx, base[:, None, :], 0), axis=-1)
    n_slots = (r * TOP_K // MOE_ROWS + N_EXPERTS) * MOE_ROWS
    nsb_e = (padded + MOE_SB_ROWS - 1) // MOE_SB_ROWS
    sb_end = jnp.cumsum(nsb_e)
    n_sb = sb_end[-1]
    nsb = n_slots // MOE_SB_ROWS + N_EXPERTS
    j = jnp.arange(nsb, dtype=jnp.int32)
    valid = j < n_sb
    e_j = jnp.minimum(jnp.sum(sb_end[None, :] <= j[:, None], axis=1), N_EXPERTS - 1)
    t = j - _take(sb_end - nsb_e, e_j)
    sb_blk0 = jnp.where(valid, (_take(pad_start, e_j) + t * MOE_SB_ROWS) // MOE_ROWS, 0)
    sb_nblk = jnp.where(valid, jnp.clip((_take(padded, e_j) - t * MOE_SB_ROWS) // MOE_ROWS,
                                        0, MOE_SB_ROWS // MOE_ROWS), 0)
    e_last = jnp.sum(jnp.where(j == n_sb - 1, e_j, 0))
    sb_e = jnp.where(valid, e_j, e_last)
    n_valid = jnp.stack([pad_end[-1] // MOE_ROWS, n_sb])
    i32 = lambda a: a.astype(jnp.int32)
    return i32(dest), i32(pad_end), (i32(sb_e), i32(sb_blk0), i32(sb_nblk), i32(n_valid)), n_slots


def _expand_blockdiag(w):
    nb, bs, _ = w.shape
    if bs == LANES:
        return w.astype(BF16)
    per = LANES // bs
    eye = jnp.eye(per, dtype=w.dtype)
    wt = w.reshape(nb // per, per, bs, bs)
    out = jnp.einsum('tpbc,pq->tpbqc', wt, eye)
    return out.reshape(nb // per, LANES, LANES).astype(BF16)


def _rope_tables(pos, half):
    freq = ROPE_BASE ** (-jnp.arange(half, dtype=F32) / half)
    ang = pos[:, None] * freq[None, :]
    cos, sin = jnp.cos(ang), jnp.sin(ang)
    return jnp.concatenate([cos, cos], axis=-1), jnp.concatenate([-sin, sin], axis=-1)


def kernel(x_prompt, x_sample, state_rglru_h, state_rglru_conv, state_ret, state_mlstm_C, state_mlstm_n,
           state_mlstm_m, state_mlstm_conv, meta_tokens, norm_mix, norm_ffn, norm_final, w_in, rg_conv_w,
           rg_conv_b, rg_wa, rg_ba, rg_wx, rg_bx, rg_lambda, ret_norm, m_conv_w, m_conv_b, m_wq, m_wk, m_wv,
           m_w_if, m_b_if, m_norm, m_skip, w_branch, w_out, moe_w_group, moe_b_group, moe_w_expert,
           moe_b_expert, moe_w_gate, moe_w_up, moe_w_down):
    bp, seq, d = x_prompt.shape
    bs = x_sample.shape[0]
    n_meta = meta_tokens.shape[0]
    depth = w_in.shape[0]
    d_rnn = state_rglru_h.shape[2]
    _, _, r_heads, r_dk, r_dv = state_ret.shape
    m_heads, m_dh = state_mlstm_C.shape[2], state_mlstm_C.shape[3]
    d_m = m_heads * m_dh
    assert seq % CHUNK == 0 and n_meta <= CHUNK and x_sample.shape[1] == 1
    pad = CHUNK - n_meta
    nch = 1 + seq // CHUNK
    rp = bp * nch * CHUNK
    r = rp + bs
    assert d_rnn == r_heads * r_dv == d_m
    xm_col = (d_rnn + 2 * r_heads * r_dk + 2 * r_heads * r_dv) // d_m
    gate_col0 = d_rnn + 2 * r_heads * r_dk + 2 * r_heads * r_dv + 2 * d_m

    pos_p = jnp.arange(nch * CHUNK, dtype=F32) - pad
    cos_p, sin_p = _rope_tables(pos_p, r_dk // 2)
    pos_s = jnp.full((1,), float(PAST_LEN), F32)
    cos_s, sin_s = _rope_tables(pos_s, r_dk // 2)

    tm = _pick_tile(r, 384, BF16_SUBLANES)
    assert rp % CHUNK == 0 and bs % BF16_SUBLANES == 0 and bs <= bp * CHUNK
    prompt_states, sample_states = [], []
    combine = None
    ret_s = c_s = None
    for l in range(depth):
        p = {'rg_conv_w': rg_conv_w[l], 'rg_conv_b': rg_conv_b[l].reshape(1, -1),
             'rg_wa': _expand_blockdiag(rg_wa[l]), 'rg_ba': rg_ba[l].reshape(1, -1),
             'rg_wx': _expand_blockdiag(rg_wx[l]), 'rg_bx': rg_bx[l].reshape(1, -1),
             'rg_lambda': rg_lambda[l].reshape(1, -1),
             'm_conv_w': m_conv_w[l], 'm_conv_b': m_conv_b[l].reshape(1, -1),
             'm_wq': _expand_blockdiag(m_wq[l]), 'm_wk': _expand_blockdiag(m_wk[l]),
             'm_wv': _expand_blockdiag(m_wv[l]),
             'm_w_if': jnp.pad(m_w_if[l], ((0, 0), (0, LANES - 2 * m_heads))).astype(BF16),
             'm_b_if': jnp.pad(m_b_if[l], (0, LANES - 2 * m_heads)).reshape(1, LANES),
             'm_norm': m_norm[l].reshape(1, -1), 'm_skip': m_skip[l].reshape(1, -1)}
        gn_ret = ret_norm[l].reshape(1, -1)

        if l == 0:
            x, h = _embed_norm_call(x_prompt, meta_tokens, x_sample.reshape(bs, d), norm_mix[l], nch,
                                    name=f'norm_mix{l}')
        else:
            x, h = _norm_call(x, norm_mix[l], tm=tm, grid=(r // tm,), row_tile=lambda i: i, want_x=True,
                              h_shape=(r, d), h_dtype=BF16, h_spec=pl.BlockSpec((tm, d), lambda i: (i, 0)),
                              combine=combine, name=f'norm_mix{l}')
        z = _mm_call(h, w_in, l, name=f'in_proj{l}')

        ya_p, rgh_p, rgc_p = _pa_call(z, bp, nch, pad, p, name=f'rglru_p{l}')
        yb_p, ret_p = _pb_call(z, bp, nch, pad, r_heads, r_dk, r_dv, cos_p, sin_p, gn_ret, name=f'ret_p{l}')
        yc_p, c_p, n_p, m_p, mc_p = _pc_call(z, bp, nch, pad, m_heads, xm_col, p, name=f'mlstm_p{l}')

        rg_cs = jnp.transpose(state_rglru_conv[l], (1, 0, 2))
        ya, rgh_s, rgc_s = _sa_call(z, rp, bs, ya_p, rg_cs, state_rglru_h[l], p, name=f'rglru_s{l}')
        yb, ret_s = _sb_call(z, rp, bs, yb_p, r_heads, r_dk, r_dv, state_ret, l, ret_s, cos_s, sin_s, gn_ret,
                             name=f'ret_s{l}')
        m_cs = jnp.transpose(state_mlstm_conv[l], (1, 0, 2))
        m_in = jnp.pad(state_mlstm_m[l], ((0, 0), (0, LANES - m_heads)))
        yc, c_s, n_s, m_s, mc_s = _sc_call(z, rp, bs, yc_p, m_heads, xm_col, m_cs, state_mlstm_C, l, c_s,
                                           state_mlstm_n[l].reshape(bs, d_m), m_in, p, name=f'mlstm_s{l}')

        prompt_states.append((rgh_p[:, 0], rgc_p, ret_p, c_p, n_p[:, :m_heads], m_p[:, 0, :m_heads], mc_p))
        sample_states.append((rgh_s, jnp.transpose(rgc_s, (1, 0, 2)), None, None,
                              n_s.reshape(bs, m_heads, m_dh), m_s[:, :m_heads], jnp.transpose(mc_s, (1, 0, 2))))

        merged = _merge_call((ya, yb, yc), z, w_branch, l, gate_col0, name=f'merge{l}')
        x = _mm_call(merged, w_out, l, res=x, name=f'out_proj{l}')

        wr = jnp.pad(jnp.concatenate([moe_w_group[l], moe_w_expert[l]], axis=1),
                     ((0, 0), (0, LANES - N_GROUPS - N_EXPERTS))).astype(BF16)
        br = jnp.pad(jnp.concatenate([moe_b_group[l], moe_b_expert[l]]), (0, LANES - N_GROUPS - N_EXPERTS))
        h2, ids, wts, tile_cnt = _norm_call(
            x, norm_ffn[l], tm=tm, grid=(r // tm,), row_tile=lambda i: i, want_x=False, h_shape=(r, d // 2),
            h_dtype=U32, h_spec=pl.BlockSpec((tm, d // 2), lambda i: (i, 0)), router=(wr, br.reshape(1, LANES)),
            name=f'norm_router{l}')
        dest, pad_end, sb_tables, n_slots = _route_tables(ids, tile_cnt, r, tm)
        xs = _dispatch_call(h2, dest, pad_end, n_slots, tm=tm, name=f'moe_dispatch{l}')
        ys = _ffn_call(xs, sb_tables, moe_w_gate, moe_w_up, moe_w_down, l, name=f'moe_ffn{l}')
        combine = (dest, wts, ys)

    (y_prompt,) = _norm_call(
        x, norm_final, tm=CHUNK, grid=(bp, seq // CHUNK), row_tile=lambda b, c: (1 + c) * bp + b, want_x=False,
        h_shape=(bp, seq, d), h_dtype=F32, h_spec=pl.BlockSpec((None, CHUNK, d), lambda b, c: (b, c, 0)),
        combine=combine, name='norm_final_p')
    ts = _pick_tile(bs, CHUNK, SUBLANES)
    assert rp % ts == 0
    (y_sample,) = _norm_call(
        x, norm_final, tm=ts, grid=(bs // ts,), row_tile=lambda i: rp // ts + i, want_x=False,
        h_shape=(bs, d), h_dtype=F32, h_spec=pl.BlockSpec((ts, d), lambda i: (i, 0)),
        combine=combine, name='norm_final_s')
    pn = [jnp.stack([s[i] for s in prompt_states], axis=0) for i in range(7)]
    sn = [None if sample_states[0][i] is None else jnp.stack([s[i] for s in sample_states], axis=0)
          for i in range(7)]
    sn[2], sn[3] = ret_s, c_s
    return (y_prompt, y_sample.reshape(bs, 1, d), *pn, *sn)
```
